```python
import jax, jax.numpy as jnp
from jax import lax
import numpy as np

D_MODEL = 1024
BATCH = 8
SEQ = 2048
DEPTH = 4

N_MIXERS = 2
N_LAYERS_A = (DEPTH + 1) // 2
N_LAYERS_B = DEPTH // 2
FOX_HEADS = 16
FOX_HEAD_DIM = D_MODEL // FOX_HEADS
Q_BLOCK = 128
FOX_IN = 4 * D_MODEL + FOX_HEADS
RET_HEADS = 4
RET_QK_DIM = D_MODEL // RET_HEADS
RET_V_DIM = 2 * D_MODEL // RET_HEADS
RET_CHUNK = 128
RET_IN = 2 * RET_HEADS * RET_QK_DIM + 2 * RET_HEADS * RET_V_DIM
N_GROUPS = 4
EXPERTS_PER_GROUP = 8
N_EXPERTS = N_GROUPS * EXPERTS_PER_GROUP
TOP_K = 2
EXPERT_FF = D_MODEL // 2
MOE_BLOCK = 128

EPS = 1e-6
NEG_INF = -1e30
ROPE_BASE = 10000.0

kernel_name = "fox_retnet_hier_moe_trunk"

F32 = jnp.float32


def rms_norm(x, g):
    xf = x.astype(F32)
    y = xf * lax.rsqrt(jnp.mean(xf * xf, axis=-1, keepdims=True) + EPS)
    return (y * g.astype(F32)).astype(x.dtype)


def forgetting_attention(h, w_in, b_f, w_out):
    B, S, _ = h.shape
    H, dh = FOX_HEADS, FOX_HEAD_DIM
    proj = h @ w_in
    q, k, v, gate = jnp.split(proj[..., :4 * D_MODEL], 4, axis=-1)
    f_logit = (proj[..., 4 * D_MODEL:] + b_f).astype(F32)
    cum = jnp.cumsum(jax.nn.log_sigmoid(f_logit), axis=1).transpose(0, 2, 1)
    heads = lambda t: t.reshape(B, S, H, dh).transpose(0, 2, 1, 3)
    q, k, v = heads(q), heads(k), heads(v)
    scale = dh ** -0.5
    k_pos = jnp.arange(S)

    def query_block(i):
        start = i * Q_BLOCK
        q_b = lax.dynamic_slice_in_dim(q, start, Q_BLOCK, axis=2)
        c_b = lax.dynamic_slice_in_dim(cum, start, Q_BLOCK, axis=2)
        s = jnp.einsum('bhqd,bhkd->bhqk', q_b, k).astype(F32) * scale
        s = s + c_b[..., :, None] - cum[..., None, :]
        q_pos = start + jnp.arange(Q_BLOCK)
        s = jnp.where(k_pos[None, :] <= q_pos[:, None], s, NEG_INF)
        p = jax.nn.softmax(s, axis=-1)
        return jnp.einsum('bhqk,bhkd->bhqd', p.astype(v.dtype), v)

    o = lax.map(query_block, jnp.arange(S // Q_BLOCK))
    o = o.transpose(1, 0, 3, 2, 4).reshape(B, S, D_MODEL)
    o = o * jax.nn.sigmoid(gate)
    return o @ w_out


def rotary(t, pos):
    d = t.shape[-1]
    inv = 1.0 / (ROPE_BASE ** jnp.linspace(0.0, 1.0, d // 2, dtype=F32))
    ang = pos.astype(F32)[:, None] * inv[None, :]
    cos, sin = jnp.cos(ang), jnp.sin(ang)
    t1, t2 = t[..., :d // 2].astype(F32), t[..., d // 2:].astype(F32)
    return jnp.concatenate([t1 * cos - t2 * sin, t1 * sin + t2 * cos], axis=-1)


def retention(h, w_in, gn_gain, w_out):
    B, S, _ = h.shape
    H, dk, dv, C = RET_HEADS, RET_QK_DIM, RET_V_DIM, RET_CHUNK
    proj = h @ w_in
    q, k, v, gate = jnp.split(proj, [H * dk, 2 * H * dk, 2 * H * dk + H * dv], axis=-1)
    heads = lambda t, d: t.reshape(B, S, H, d).transpose(0, 2, 1, 3)
    pos = jnp.arange(S)
    q = rotary(heads(q, dk), pos)
    k = rotary(heads(k, dk), pos) * (dk ** -0.5)
    v = heads(v, dv).astype(F32)

    log_g = jnp.log(1.0 - 2.0 ** (-5.0 - jnp.arange(H, dtype=F32)))
    idx = jnp.arange(C, dtype=F32)
    rel = idx[:, None] - idx[None, :]
    intra = jnp.where(rel >= 0, jnp.exp(log_g[:, None, None] * jnp.maximum(rel, 0.0)), 0.0)
    q_decay = jnp.exp(log_g[:, None] * (idx + 1.0))[None, :, :, None]
    k_decay = jnp.exp(log_g[:, None] * (C - 1.0 - idx))[None, :, :, None]
    chunk_decay = jnp.exp(log_g * C)[None, :, None, None]

    n_chunks = S // C
    to_chunks = lambda t: t.reshape(B, H, n_chunks, C, t.shape[-1]).transpose(2, 0, 1, 3, 4)

    def step(R, qkv):
        q_n, k_n, v_n = qkv
        s = jnp.einsum('bhid,bhjd->bhij', q_n, k_n) * intra
        o = jnp.einsum('bhij,bhje->bhie', s, v_n)
        o = o + jnp.einsum('bhid,bhde->bhie', q_n, R) * q_decay
        R = R * chunk_decay + jnp.einsum('bhjd,bhje->bhde', k_n * k_decay, v_n)
        return R, o

    R0 = jnp.zeros((B, H, dk, dv), F32)
    _, o = lax.scan(step, R0, (to_chunks(q), to_chunks(k), to_chunks(v)))
    o = o.transpose(1, 0, 3, 2, 4).reshape(B, S, H, dv)
    mu = jnp.mean(o, axis=-1, keepdims=True)
    var = jnp.mean(jnp.square(o - mu), axis=-1, keepdims=True)
    o = ((o - mu) * lax.rsqrt(var + EPS)).reshape(B, S, H * dv) * gn_gain.astype(F32)
    o = jax.nn.silu(gate.astype(F32)) * o
    return o.astype(h.dtype) @ w_out


def grouped_experts(xf, expert_id, gates, w_gu, w_down):
    T, D = xf.shape
    A = T * TOP_K
    flat_e = expert_id.reshape(A)
    order = jnp.argsort(flat_e)
    sorted_e = flat_e[order]
    counts = jnp.bincount(flat_e, length=N_EXPERTS)
    padded = (counts + MOE_BLOCK - 1) // MOE_BLOCK * MOE_BLOCK
    pad_end = jnp.cumsum(padded)
    pad_start = pad_end - padded
    start = jnp.cumsum(counts) - counts
    dest = pad_start[sorted_e] + jnp.arange(A) - start[sorted_e]
    n_blocks = (A + N_EXPERTS * (MOE_BLOCK - 1) + MOE_BLOCK - 1) // MOE_BLOCK
    n_rows = n_blocks * MOE_BLOCK
    tok_sorted = (order // TOP_K).astype(jnp.int32)
    row_tok = jnp.full((n_rows,), T, jnp.int32).at[dest].set(tok_sorted)
    x_pad = jnp.concatenate([xf, jnp.zeros((1, D), xf.dtype)], axis=0)
    x_rows = x_pad[row_tok].reshape(n_blocks, MOE_BLOCK, D)
    block_e = jnp.minimum(jnp.searchsorted(pad_end, jnp.arange(n_blocks) * MOE_BLOCK, side='right'),
                          N_EXPERTS - 1)

    def expert_block(args):
        xb, e = args
        a, b = jnp.split(xb @ w_gu[e], 2, axis=-1)
        return (jax.nn.silu(a) * b) @ w_down[e]

    y_rows = lax.map(expert_block, (x_rows, block_e)).reshape(n_rows, D)
    w_sorted = gates.reshape(A)[order].astype(y_rows.dtype)
    return jax.ops.segment_sum(y_rows[dest] * w_sorted[:, None], tok_sorted, num_segments=T)


def hier_moe(h, wr_g, br_g, wr_e, br_e, w_gu, w_down):
    B, S, D = h.shape
    T = B * S
    xf = h.reshape(T, D)
    p_group = jax.nn.softmax((xf @ wr_g).astype(F32) + br_g.astype(F32), axis=-1)
    pg, gi = lax.top_k(p_group, 1)
    e_logits = ((xf @ wr_e).astype(F32) + br_e.astype(F32)).reshape(T, N_GROUPS, EXPERTS_PER_GROUP)
    e_in = jnp.einsum('tge,tg->te', e_logits, jax.nn.one_hot(gi[:, 0], N_GROUPS, dtype=F32))
    pv, pi = lax.top_k(jax.nn.softmax(e_in, axis=-1), TOP_K)
    gates = pg * pv / jnp.sum(pv, axis=-1, keepdims=True)
    expert_id = gi * EXPERTS_PER_GROUP + pi
    return grouped_experts(xf, expert_id, gates, w_gu, w_down).reshape(B, S, D)


def setup_inputs(seed: int = 0) -> dict:
    key = jax.random.key(seed)
    ks = jax.random.split(key, 17)
    nrm = lambda k, shape, fan_in: jax.random.normal(k, shape, F32) * (fan_in ** -0.5)
    out_scale = (2.0 * DEPTH) ** -0.5
    return {
        "x": jax.random.normal(ks[0], (BATCH, SEQ, D_MODEL), F32),
        "fox_w_in": nrm(ks[1], (N_LAYERS_A, D_MODEL, FOX_IN), D_MODEL),
        "fox_b_f": 3.0 + 0.5 * jax.random.normal(ks[2], (N_LAYERS_A, FOX_HEADS), F32),
        "fox_w_out": nrm(ks[3], (N_LAYERS_A, D_MODEL, D_MODEL), D_MODEL) * out_scale,
        "ret_w_in": nrm(ks[4], (N_LAYERS_B, D_MODEL, RET_IN), D_MODEL),
        "ret_gn_gain": 1.0 + 0.02 * jax.random.normal(ks[5], (N_LAYERS_B, RET_HEADS * RET_V_DIM), F32),
        "ret_w_out": nrm(ks[6], (N_LAYERS_B, RET_HEADS * RET_V_DIM, D_MODEL), RET_HEADS * RET_V_DIM) * out_scale,
        "norm_mix": 1.0 + 0.02 * jax.random.normal(ks[7], (DEPTH, D_MODEL), F32),
        "norm_ffn": 1.0 + 0.02 * jax.random.normal(ks[8], (DEPTH, D_MODEL), F32),
        "router_group_w": nrm(ks[9], (DEPTH, D_MODEL, N_GROUPS), D_MODEL),
        "router_group_b": 0.01 * jax.random.normal(ks[10], (DEPTH, N_GROUPS), F32),
        "router_expert_w": nrm(ks[11], (DEPTH, D_MODEL, N_EXPERTS), D_MODEL),
        "router_expert_b": 0.01 * jax.random.normal(ks[12], (DEPTH, N_EXPERTS), F32),
        "expert_w_gu": nrm(ks[13], (DEPTH, N_EXPERTS, D_MODEL, 2 * EXPERT_FF), D_MODEL),
        "expert_w_down": nrm(ks[14], (DEPTH, N_EXPERTS, EXPERT_FF, D_MODEL), EXPERT_FF) * out_scale,
        "norm_final": 1.0 + 0.02 * jax.random.normal(ks[15], (D_MODEL,), F32),
    }


def reference(x, fox_w_in, fox_b_f, fox_w_out, ret_w_in, ret_gn_gain, ret_w_out,
              norm_mix, norm_ffn, router_group_w, router_group_b, router_expert_w,
              router_expert_b, expert_w_gu, expert_w_down, norm_final):
    h = x
    for i in range(DEPTH):
        hn = rms_norm(h, norm_mix[i])
        j = i // N_MIXERS
        if i % N_MIXERS == 0:
            h = h + forgetting_attention(hn, fox_w_in[j], fox_b_f[j], fox_w_out[j])
        else:
            h = h + retention(hn, ret_w_in[j], ret_gn_gain[j], ret_w_out[j])
        hn = rms_norm(h, norm_ffn[i])
        h = h + hier_moe(hn, router_group_w[i], router_group_b[i], router_expert_w[i],
                         router_expert_b[i], expert_w_gu[i], expert_w_down[i])
    return rms_norm(h, norm_final)
```

```python
import functools

import jax
import jax.numpy as jnp
from jax import lax
from jax.experimental import pallas as pl
from jax.experimental.pallas import tpu as pltpu

F32 = jnp.float32
BF16 = jnp.bfloat16
I32 = jnp.int32

FOX_HEADS = 16
RET_HEADS = 4
RET_CHUNK = 128
N_GROUPS = 4
EXPERTS_PER_GROUP = 8
N_EXPERTS = N_GROUPS * EXPERTS_PER_GROUP
TOP_K = 2
EPS = 1e-6
NEG_INF = -1e30
ROPE_BASE = 10000.0

LANES = 128
ROUTE_LANE0 = N_GROUPS
VMEM_LIMIT = 56 * 1024 * 1024
MOE_BLOCK = 256

_CONTRACT_LAST = (((1,), (1,)), ((), ()))
_CONTRACT_FIRST = (((0,), (0,)), ((), ()))


def _cparams(*sem):
    return pltpu.CompilerParams(dimension_semantics=sem, vmem_limit_bytes=VMEM_LIMIT)


def _sigmoid(x):
    return 1.0 / (1.0 + jnp.exp(-x))


def _rms(x, g, eps):
    ms = jnp.mean(x * x, axis=-1, keepdims=True)
    return x * lax.rsqrt(ms + eps) * g


def _split3(x):
    hi = x.astype(BF16)
    r = x - hi.astype(F32)
    mid = r.astype(BF16)
    lo = (r - mid.astype(F32)).astype(BF16)
    return hi, mid, lo


def _fox_inproj_kernel(h_ref, g_ref, w_ref, wf_ref, bf_ref, q_ref, kb_ref, v_ref, gate_ref, a_ref,
                       carry_ref, *, tiles_per_seq, scale):
    i = pl.program_id(0)
    tm, d = h_ref.shape
    xn = _rms(h_ref[...], g_ref[...], EPS).astype(BF16)

    q = jnp.dot(xn, w_ref[:, 0:d], preferred_element_type=F32)
    q_ref[...] = (q * scale).astype(BF16)
    v_ref[...] = jnp.dot(xn, w_ref[:, 2 * d:3 * d], preferred_element_type=F32).astype(BF16)
    gate_ref[...] = jnp.dot(xn, w_ref[:, 3 * d:4 * d], preferred_element_type=F32)

    f = jnp.dot(xn, wf_ref[...], preferred_element_type=F32) + bf_ref[...]
    ls = jnp.minimum(f, 0.0) - jnp.log1p(jnp.exp(-jnp.abs(f)))

    @pl.when(i % tiles_per_seq == 0)
    def _():
        carry_ref[...] = jnp.zeros_like(carry_ref)

    row = lax.broadcasted_iota(I32, (tm, tm), 0)
    col = lax.broadcasted_iota(I32, (tm, tm), 1)
    tri = jnp.where(col <= row, 1.0, 0.0).astype(BF16)
    hi, mid, lo = _split3(ls)
    cum = (jnp.dot(tri, hi, preferred_element_type=F32) + jnp.dot(tri, mid, preferred_element_type=F32)
           + jnp.dot(tri, lo, preferred_element_type=F32)) + carry_ref[...]
    carry_ref[...] = cum[tm - 1:tm, :]

    c_hi, c_mid, c_lo = (c.astype(F32) for c in _split3(cum))
    lane = lax.broadcasted_iota(I32, (tm, LANES), 1)
    a_ref[...] = jnp.where(lane < 16, c_hi, jnp.where(lane < 32, c_mid, jnp.where(
        lane < 48, c_lo, jnp.where(lane < 96, 1.0, 0.0)))).astype(BF16)
    bk = jnp.where(lane < 48, 1.0, jnp.where(lane < 64, -c_hi, jnp.where(
        lane < 80, -c_mid, jnp.where(lane < 96, -c_lo, 0.0)))).astype(BF16)

    k = jnp.dot(xn, w_ref[:, d:2 * d], preferred_element_type=F32).astype(BF16)
    for p in range(d // LANES):
        kb_ref[:, 2 * p * LANES:(2 * p + 1) * LANES] = k[:, p * LANES:(p + 1) * LANES]
        kb_ref[:, (2 * p + 1) * LANES:(2 * p + 2) * LANES] = bk


def _fox_inproj(h, g, w, wf, bfr, *, seq, tm=512):
    t, d = h.shape
    scale = float((d // FOX_HEADS) ** -0.5)
    kern = functools.partial(_fox_inproj_kernel, tiles_per_seq=seq // tm, scale=scale)
    row = lambda i: (i, 0)
    const = lambda i: (0, 0)
    return pl.pallas_call(
        kern,
        grid=(t // tm,),
        in_specs=[pl.BlockSpec((tm, d), row), pl.BlockSpec((1, d), const),
                  pl.BlockSpec((d, 4 * d), const), pl.BlockSpec((d, LANES), const),
                  pl.BlockSpec((1, LANES), const)],
        out_specs=[pl.BlockSpec((tm, d), row), pl.BlockSpec((tm, 2 * d), row),
                   pl.BlockSpec((tm, d), row), pl.BlockSpec((tm, d), row),
                   pl.BlockSpec((tm, LANES), row)],
        out_shape=[jax.ShapeDtypeStruct((t, d), BF16), jax.ShapeDtypeStruct((t, 2 * d), BF16),
                   jax.ShapeDtypeStruct((t, d), BF16), jax.ShapeDtypeStruct((t, d), F32),
                   jax.ShapeDtypeStruct((t, LANES), BF16)],
        scratch_shapes=[pltpu.VMEM((1, LANES), F32)],
        compiler_params=_cparams("arbitrary"),
        name="fox_inproj",
    )(h, g, w, wf, bfr)


def _fox_attn_kernel(q_ref, a_ref, kb_ref, v_ref, gate_ref, o_ref, *, tq, dh):
    seq = q_ref.shape[0]
    pair = pl.program_id(1)
    lane = lax.broadcasted_iota(I32, (tq, LANES), 1)
    row = lax.broadcasted_iota(I32, (tq, tq), 0)
    col = lax.broadcasted_iota(I32, (tq, tq), 1)
    causal = col <= row
    heads_per_block = LANES // dh
    qmask = [jnp.where((lane >= hh * dh) & (lane < (hh + 1) * dh), 1.0, 0.0).astype(BF16)
             for hh in range(heads_per_block)]
    amask = [jnp.where((lane < 96) & ((lane & 15) == pair * heads_per_block + hh), 1.0, 0.0).astype(BF16)
             for hh in range(heads_per_block)]
    for i in range(seq // tq):
        off = i * tq
        q = q_ref[off:off + tq, :]
        a = a_ref[off:off + tq, :]
        outs = []
        for hh in range(heads_per_block):
            qa = jnp.concatenate([q * qmask[hh], a * amask[hh]], axis=1)
            sd = lax.dot_general(qa, kb_ref[off:off + tq, :], _CONTRACT_LAST, preferred_element_type=F32)
            sd = jnp.where(causal, sd, NEG_INF)
            m = jnp.max(sd, axis=1, keepdims=True)
            if off > 0:
                so = lax.dot_general(qa, kb_ref[0:off, :], _CONTRACT_LAST, preferred_element_type=F32)
                m = jnp.maximum(m, jnp.max(so, axis=1, keepdims=True))
                eo = jnp.exp(so - m)
                ed = jnp.exp(sd - m)
                inv = 1.0 / (jnp.sum(eo, axis=1, keepdims=True) + jnp.sum(ed, axis=1, keepdims=True))
                o = (jnp.dot((eo * inv).astype(BF16), v_ref[0:off, :], preferred_element_type=F32)
                     + jnp.dot((ed * inv).astype(BF16), v_ref[off:off + tq, :], preferred_element_type=F32))
            else:
                ed = jnp.exp(sd - m)
                inv = 1.0 / jnp.sum(ed, axis=1, keepdims=True)
                o = jnp.dot((ed * inv).astype(BF16), v_ref[off:off + tq, :], preferred_element_type=F32)
            outs.append(o)
        o = outs[0]
        for hh in range(1, heads_per_block):
            o = jnp.where(lane < hh * dh, o, outs[hh])
        o_ref[off:off + tq, :] = (o * _sigmoid(gate_ref[off:off + tq, :])).astype(BF16)


def _fox_attn(q, a, kb, v, gate, *, batch, seq, tq=128):
    t, d = q.shape
    dh = d // FOX_HEADS
    pairs = d // LANES
    kern = functools.partial(_fox_attn_kernel, tq=tq, dh=dh)
    blk = lambda b, p: (b, p)
    return pl.pallas_call(
        kern,
        grid=(batch, pairs),
        in_specs=[pl.BlockSpec((seq, LANES), blk), pl.BlockSpec((seq, LANES), lambda b, p: (b, 0)),
                  pl.BlockSpec((seq, 2 * LANES), blk), pl.BlockSpec((seq, LANES), blk),
                  pl.BlockSpec((seq, LANES), blk)],
        out_specs=pl.BlockSpec((seq, LANES), blk),
        out_shape=jax.ShapeDtypeStruct((t, d), BF16),
        compiler_params=_cparams("arbitrary", "arbitrary"),
        name="fox_attn",
    )(q, a, kb, v, gate)


def _ret_inproj_kernel(h_ref, g_ref, w_ref, cos_ref, sin_ref, q_ref, k_ref, v_ref, gate_ref, *, kscale):
    tm, d = h_ref.shape
    xn = _rms(h_ref[...], g_ref[...], EPS).astype(BF16)
    cos = cos_ref[...]
    sin = sin_ref[...]
    dk = d // RET_HEADS
    half = dk // 2

    def rot(t, h):
        t1 = t[:, h * dk:h * dk + half]
        t2 = t[:, h * dk + half:(h + 1) * dk]
        return t1 * cos - t2 * sin, t1 * sin + t2 * cos

    q = jnp.dot(xn, w_ref[:, 0:d], preferred_element_type=F32)
    k = jnp.dot(xn, w_ref[:, d:2 * d], preferred_element_type=F32)
    for h in range(RET_HEADS):
        q1, q2 = rot(q, h)
        q_ref[:, h * dk:h * dk + half] = q1.astype(BF16)
        q_ref[:, h * dk + half:(h + 1) * dk] = q2.astype(BF16)
        k1, k2 = rot(k, h)
        k_ref[:, h * dk:h * dk + half] = k1 * kscale
        k_ref[:, h * dk + half:(h + 1) * dk] = k2 * kscale
    v_ref[...] = jnp.dot(xn, w_ref[:, 2 * d:4 * d], preferred_element_type=F32).astype(BF16)
    gate_ref[...] = jnp.dot(xn, w_ref[:, 4 * d:6 * d], preferred_element_type=F32)


def _ret_inproj(h, g, w, cos, sin, *, seq, tm=256):
    t, d = h.shape
    half = d // RET_HEADS // 2
    kern = functools.partial(_ret_inproj_kernel, kscale=float((d // RET_HEADS) ** -0.5))
    row = lambda i: (i, 0)
    const = lambda i: (0, 0)
    pos = lambda i: (i % (seq // tm), 0)
    return pl.pallas_call(
        kern,
        grid=(t // tm,),
        in_specs=[pl.BlockSpec((tm, d), row), pl.BlockSpec((1, d), const), pl.BlockSpec((d, 6 * d), const),
                  pl.BlockSpec((tm, half), pos), pl.BlockSpec((tm, half), pos)],
        out_specs=[pl.BlockSpec((tm, d), row), pl.BlockSpec((tm, d), row),
                   pl.BlockSpec((tm, 2 * d), row), pl.BlockSpec((tm, 2 * d), row)],
        out_shape=[jax.ShapeDtypeStruct((t, d), BF16), jax.ShapeDtypeStruct((t, d), F32),
                   jax.ShapeDtypeStruct((t, 2 * d), BF16), jax.ShapeDtypeStruct((t, 2 * d), F32)],
        compiler_params=_cparams("arbitrary"),
        name="ret_inproj",
    )(h, g, w, cos, sin)


def _retention_kernel(q_ref, k_ref, v_ref, gate_ref, gain_ref, intra_ref, qd_ref, kd_ref, cd_ref,
                      y_ref, r_ref, *, chunk):
    seq = q_ref.shape[0]
    r_ref[...] = jnp.zeros_like(r_ref)

    def body(c, carry):
        off = pl.multiple_of(c * chunk, chunk)
        qn = q_ref[pl.ds(off, chunk), :]
        k32 = k_ref[pl.ds(off, chunk), :]
        vn = v_ref[pl.ds(off, chunk), :]
        s = lax.dot_general(qn, k32.astype(BF16), _CONTRACT_LAST, preferred_element_type=F32) * intra_ref[...]
        o = jnp.dot(s.astype(BF16), vn, preferred_element_type=F32)
        r = r_ref[...]
        o = o + jnp.dot(qn, r.astype(BF16), preferred_element_type=F32) * qd_ref[...]
        kdec = (k32 * kd_ref[...]).astype(BF16)
        r_ref[...] = r * cd_ref[...] + lax.dot_general(kdec, vn, _CONTRACT_FIRST, preferred_element_type=F32)
        mu = jnp.mean(o, axis=-1, keepdims=True)
        dlt = o - mu
        var = jnp.mean(dlt * dlt, axis=-1, keepdims=True)
        on = dlt * lax.rsqrt(var + EPS) * gain_ref[...]
        g = gate_ref[pl.ds(off, chunk), :]
        y_ref[pl.ds(off, chunk), :] = ((g * _sigmoid(g)) * on).astype(BF16)
        return carry

    lax.fori_loop(0, seq // chunk, body, 0)


def _retention(q, k, v, gate, gain, intra, qd, kd, cd, *, batch, seq):
    t, d = q.shape
    dk = d // RET_HEADS
    dv = v.shape[1] // RET_HEADS
    c = RET_CHUNK
    kern = functools.partial(_retention_kernel, chunk=c)
    blk = lambda b, h: (b, h)
    hd = lambda b, h: (h, 0, 0)
    return pl.pallas_call(
        kern,
        grid=(batch, RET_HEADS),
        in_specs=[pl.BlockSpec((seq, dk), blk), pl.BlockSpec((seq, dk), blk), pl.BlockSpec((seq, dv), blk),
                  pl.BlockSpec((seq, dv), blk), pl.BlockSpec((1, dv), lambda b, h: (0, h)),
                  pl.BlockSpec((None, c, c), hd), pl.BlockSpec((None, c, 1), hd),
                  pl.BlockSpec((None, c, 1), hd), pl.BlockSpec((None, 1, 1), hd)],
        out_specs=pl.BlockSpec((seq, dv), blk),
        out_shape=jax.ShapeDtypeStruct((t, v.shape[1]), BF16),
        scratch_shapes=[pltpu.VMEM((dk, dv), F32)],
        compiler_params=_cparams("arbitrary", "arbitrary"),
        name="retention",
    )(q, k, v, gate, gain, intra, qd, kd, cd)


def _outproj_router_kernel(a_ref, w_ref, h_ref, g_ref, wrh_ref, wrl_ref, br_ref,
                           h1_ref, hn_ref, route_ref, cnt_ref):
    i = pl.program_id(0)
    tm = a_ref.shape[0]

    @pl.when(i == 0)
    def _():
        cnt_ref[...] = jnp.zeros_like(cnt_ref)

    h1 = h_ref[...] + jnp.dot(a_ref[...], w_ref[...], preferred_element_type=F32)
    h1_ref[...] = h1
    hn = _rms(h1, g_ref[...], EPS)
    hn_ref[...] = hn

    hh = hn.astype(BF16)
    hl = (hn - hh.astype(F32)).astype(BF16)
    lg = (jnp.dot(hh, wrh_ref[...], preferred_element_type=F32)
          + jnp.dot(hh, wrl_ref[...], preferred_element_type=F32)
          + jnp.dot(hl, wrh_ref[...], preferred_element_type=F32)) + br_ref[...]

    lane = lax.broadcasted_iota(I32, (tm, LANES), 1)
    lanef = lane.astype(F32)
    big = float(LANES)

    def softmax_masked(mask):
        mx = jnp.max(jnp.where(mask, lg, -jnp.inf), axis=1, keepdims=True)
        e = jnp.where(mask, jnp.exp(lg - mx), 0.0)
        return e / jnp.sum(e, axis=1, keepdims=True)

    def top1(p, mask):
        best = jnp.max(jnp.where(mask, p, -1.0), axis=1, keepdims=True)
        idx = jnp.min(jnp.where(mask & (p == best), lanef, big), axis=1, keepdims=True)
        return best, idx

    gmask = lane < N_GROUPS
    pg, gi = top1(softmax_masked(gmask), gmask)
    lo = ROUTE_LANE0 + gi * EXPERTS_PER_GROUP
    emask = (lanef >= lo) & (lanef < lo + EXPERTS_PER_GROUP)
    pe = softmax_masked(emask)
    p1, i1 = top1(pe, emask)
    p2, i2 = top1(pe, emask & (lanef != i1))
    den = p1 + p2
    g1 = pg * p1 / den
    g2 = pg * p2 / den

    oh1 = lanef == i1
    oh2 = lanef == i2
    both = jnp.where(oh1 | oh2, 1.0, 0.0)
    row = lax.broadcasted_iota(I32, (tm, tm), 0)
    col = lax.broadcasted_iota(I32, (tm, tm), 1)
    strict = jnp.where(col < row, 1.0, 0.0).astype(BF16)
    before = jnp.dot(strict, both.astype(BF16), preferred_element_type=F32) + cnt_ref[0:1, :]
    r1 = jnp.sum(jnp.where(oh1, before, 0.0), axis=1, keepdims=True)
    r2 = jnp.sum(jnp.where(oh2, before, 0.0), axis=1, keepdims=True)
    cnt_ref[0:1, :] = cnt_ref[0:1, :] + jnp.sum(both, axis=0, keepdims=True)

    route_ref[...] = jnp.where(lane == 0, i1 - ROUTE_LANE0, jnp.where(lane == 1, i2 - ROUTE_LANE0, jnp.where(
        lane == 2, r1, jnp.where(lane == 3, r2, jnp.where(lane == 4, g1, jnp.where(lane == 5, g2, 0.0))))))


def _outproj_router(a, w, h, g, wrh, wrl, br, *, tm=512):
    t, kdim = a.shape
    d = h.shape[1]
    row = lambda i: (i, 0)
    const = lambda i: (0, 0)
    return pl.pallas_call(
        _outproj_router_kernel,
        grid=(t // tm,),
        in_specs=[pl.BlockSpec((tm, kdim), row), pl.BlockSpec((kdim, d), const), pl.BlockSpec((tm, d), row),
                  pl.BlockSpec((1, d), const), pl.BlockSpec((d, LANES), const), pl.BlockSpec((d, LANES), const),
                  pl.BlockSpec((1, LANES), const)],
        out_specs=[pl.BlockSpec((tm, d), row), pl.BlockSpec((tm, d), row), pl.BlockSpec((tm, LANES), row),
                   pl.BlockSpec((8, LANES), const)],
        out_shape=[jax.ShapeDtypeStruct((t, d), F32), jax.ShapeDtypeStruct((t, d), F32),
                   jax.ShapeDtypeStruct((t, LANES), F32), jax.ShapeDtypeStruct((8, LANES), F32)],
        compiler_params=_cparams("arbitrary"),
        name="outproj_router",
    )(a, w, h, g, wrh, wrl, br)


def _dispatch_kernel(dest_ref, hn_ref, xs_in_ref, xs_ref, sem):
    del xs_in_ref
    tm = hn_ref.shape[0]

    def row_copy(t, k):
        return pltpu.make_async_copy(hn_ref.at[pl.ds(t, 1), :], xs_ref.at[pl.ds(dest_ref[k * tm + t], 1), :], sem)

    def issue(t, carry):
        for k in range(TOP_K):
            row_copy(t, k).start()
        return carry

    lax.fori_loop(0, tm, issue, 0, unroll=8)

    def drain(t, carry):
        for k in range(TOP_K):
            row_copy(t, k).wait()
        return carry

    lax.fori_loop(0, tm, drain, 0, unroll=8)


def _dispatch(dest_tiles, hn, n_rows, *, tm):
    t, d = hn.shape
    zeros = jnp.zeros((n_rows, d), F32)
    return pl.pallas_call(
        _dispatch_kernel,
        grid=(t // tm,),
        in_specs=[pl.BlockSpec((TOP_K * tm,), lambda i: (i,), memory_space=pltpu.SMEM),
                  pl.BlockSpec((tm, d), lambda i: (i, 0)),
                  pl.BlockSpec(memory_space=pl.ANY)],
        out_specs=pl.BlockSpec(memory_space=pl.ANY),
        out_shape=jax.ShapeDtypeStruct((n_rows, d), F32),
        scratch_shapes=[pltpu.SemaphoreType.DMA(())],
        input_output_aliases={2: 0},
        compiler_params=_cparams("arbitrary"),
        name="moe_dispatch",
    )(dest_tiles, hn, zeros)


def _experts_kernel(be_ref, nu_ref, x_ref, wgu_ref, wd_ref, y_ref):
    del be_ref
    i = pl.program_id(0)

    @pl.when(i < nu_ref[0])
    def _():
        x = x_ref[...].astype(BF16)
        hcat = jnp.dot(x, wgu_ref[...], preferred_element_type=F32)
        ff = hcat.shape[1] // 2
        a = hcat[:, :ff]
        b = hcat[:, ff:]
        act = ((a * _sigmoid(a)) * b).astype(BF16)
        y_ref[...] = jnp.dot(act, wd_ref[...], preferred_element_type=F32)

    @pl.when(i >= nu_ref[0])
    def _():
        y_ref[...] = jnp.zeros_like(y_ref)


def _experts(block_e, n_used, xs, wgu, wd, *, bm):
    n_rows, d = xs.shape
    ff2 = wgu.shape[2]
    grid_spec = pltpu.PrefetchScalarGridSpec(
        num_scalar_prefetch=2,
        grid=(n_rows // bm,),
        in_specs=[pl.BlockSpec((bm, d), lambda i, be, nu: (i, 0)),
                  pl.BlockSpec((None, d, ff2), lambda i, be, nu: (be[i], 0, 0)),
                  pl.BlockSpec((None, ff2 // 2, d), lambda i, be, nu: (be[i], 0, 0))],
        out_specs=pl.BlockSpec((bm, d), lambda i, be, nu: (i, 0)),
    )
    return pl.pallas_call(
        _experts_kernel,
        grid_spec=grid_spec,
        out_shape=jax.ShapeDtypeStruct((n_rows, d), F32),
        compiler_params=_cparams("arbitrary"),
        name="moe_experts",
    )(block_e, n_used, xs, wgu, wd)


def _combine_kernel(dest_ref, h1_ref, route_ref, gfin_ref, ys_ref, o_ref, buf_ref, sem, *, final_norm):
    tm = h1_ref.shape[0]

    def row_copy(t, k):
        return pltpu.make_async_copy(ys_ref.at[pl.ds(dest_ref[k * tm + t], 1), :],
                                     buf_ref.at[k, pl.ds(t, 1), :], sem)

    def issue(t, carry):
        for k in range(TOP_K):
            row_copy(t, k).start()
        return carry

    lax.fori_loop(0, tm, issue, 0, unroll=8)

    def drain(t, carry):
        for k in range(TOP_K):
            row_copy(t, k).wait()
        return carry

    lax.fori_loop(0, tm, drain, 0, unroll=8)

    g0 = route_ref[:, 4:5]
    g1 = route_ref[:, 5:6]
    out = h1_ref[...] + (buf_ref[0] * g0 + buf_ref[1] * g1)
    if final_norm:
        out = _rms(out, gfin_ref[...], EPS)
    o_ref[...] = out


def _combine(dest_tiles, h1, route, gfin, ys, *, tm, final_norm):
    t, d = h1.shape
    kern = functools.partial(_combine_kernel, final_norm=final_norm)
    return pl.pallas_call(
        kern,
        grid=(t // tm,),
        in_specs=[pl.BlockSpec((TOP_K * tm,), lambda i: (i,), memory_space=pltpu.SMEM),
                  pl.BlockSpec((tm, d), lambda i: (i, 0)),
                  pl.BlockSpec((tm, LANES), lambda i: (i, 0)),
                  pl.BlockSpec((1, d), lambda i: (0, 0)),
                  pl.BlockSpec(memory_space=pl.ANY)],
        out_specs=pl.BlockSpec((tm, d), lambda i: (i, 0)),
        out_shape=jax.ShapeDtypeStruct((t, d), F32),
        scratch_shapes=[pltpu.VMEM((TOP_K, tm, d), F32), pltpu.SemaphoreType.DMA(())],
        compiler_params=_cparams("arbitrary"),
        name="moe_combine",
    )(dest_tiles, h1, route, gfin, ys)


def _moe(a, w_out, h, g_ffn, wr_g, br_g, wr_e, br_e, w_gu, w_down, gfin, *, final_norm, tm_move=256):
    t, d = h.shape
    bm = MOE_BLOCK
    wr = jnp.concatenate([wr_g, wr_e, jnp.zeros((d, LANES - N_GROUPS - N_EXPERTS), F32)], axis=1)
    wrh = wr.astype(BF16)
    wrl = (wr - wrh.astype(F32)).astype(BF16)
    br = jnp.concatenate([br_g, br_e, jnp.zeros((LANES - N_GROUPS - N_EXPERTS,), F32)])[None, :]
    h1, hn, route, cnt = _outproj_router(a, w_out.astype(BF16), h, g_ffn[None, :], wrh, wrl, br)

    counts = cnt[0, ROUTE_LANE0:ROUTE_LANE0 + N_EXPERTS].astype(I32)
    padded = (counts + bm - 1) // bm * bm
    pad_end = jnp.cumsum(padded)
    pad_start = pad_end - padded
    n_blocks = (t * TOP_K + N_EXPERTS * (bm - 1) + bm - 1) // bm
    n_rows = n_blocks * bm
    block_e = jnp.minimum(jnp.searchsorted(pad_end, jnp.arange(n_blocks, dtype=I32) * bm, side="right"),
                          N_EXPERTS - 1).astype(I32)
    n_used = (pad_end[-1:] // bm).astype(I32)
    eid = route[:, 0:TOP_K].astype(I32)
    dest = pad_start[eid] + route[:, 2:2 + TOP_K].astype(I32)
    dest_tiles = dest.reshape(t // tm_move, tm_move, TOP_K).transpose(0, 2, 1).reshape(-1)

    xs = _dispatch(dest_tiles, hn, n_rows, tm=tm_move)
    ys = _experts(block_e, n_used, xs, w_gu.astype(BF16), w_down.astype(BF16), bm=bm)
    return _combine(dest_tiles, h1, route, gfin[None, :], ys, tm=tm_move, final_norm=final_norm)


def _fox_mixer(h, g, w_in, b_f, *, batch, seq):
    d = h.shape[1]
    nh = FOX_HEADS
    wf = w_in[:, 4 * d:4 * d + nh]
    wf = jnp.concatenate([wf] * 6 + [jnp.zeros((d, LANES - 6 * nh), F32)], axis=1).astype(BF16)
    bfr = jnp.concatenate([b_f] * 6 + [jnp.zeros((LANES - 6 * nh,), F32)])[None, :]
    q, kb, v, gate, a = _fox_inproj(h, g[None, :], w_in[:, :4 * d].astype(BF16), wf, bfr, seq=seq)
    return _fox_attn(q, a, kb, v, gate, batch=batch, seq=seq)


def _ret_mixer(h, g, w_in, gn_gain, *, batch, seq):
    d = h.shape[1]
    nh = RET_HEADS
    dk = d // nh
    c = RET_CHUNK
    inv = 1.0 / (ROPE_BASE ** jnp.linspace(0.0, 1.0, dk // 2, dtype=F32))
    ang = jnp.arange(seq).astype(F32)[:, None] * inv[None, :]
    cos, sin = jnp.cos(ang), jnp.sin(ang)
    log_g = jnp.log(1.0 - 2.0 ** (-5.0 - jnp.arange(nh, dtype=F32)))
    idx = jnp.arange(c, dtype=F32)
    rel = idx[:, None] - idx[None, :]
    intra = jnp.where(rel >= 0, jnp.exp(log_g[:, None, None] * jnp.maximum(rel, 0.0)), 0.0)
    qd = jnp.exp(log_g[:, None] * (idx + 1.0))[:, :, None]
    kd = jnp.exp(log_g[:, None] * (c - 1.0 - idx))[:, :, None]
    cd = jnp.exp(log_g * c)[:, None, None]
    q, k, v, gate = _ret_inproj(h, g[None, :], w_in.astype(BF16), cos, sin, seq=seq)
    return _retention(q, k, v, gate, gn_gain[None, :], intra, qd, kd, cd, batch=batch, seq=seq)


def kernel(x, fox_w_in, fox_b_f, fox_w_out, ret_w_in, ret_gn_gain, ret_w_out, norm_mix, norm_ffn,
           router_group_w, router_group_b, router_expert_w, router_expert_b, expert_w_gu, expert_w_down,
           norm_final):
    batch, seq, d = x.shape
    depth = norm_mix.shape[0]
    h = x.reshape(batch * seq, d)
    for i in range(depth):
        j = i // 2
        if i % 2 == 0:
            a = _fox_mixer(h, norm_mix[i], fox_w_in[j], fox_b_f[j], batch=batch, seq=seq)
            w_out = fox_w_out[j]
        else:
            a = _ret_mixer(h, norm_mix[i], ret_w_in[j], ret_gn_gain[j], batch=batch, seq=seq)
            w_out = ret_w_out[j]
        h = _moe(a, w_out, h, norm_ffn[i], router_group_w[i], router_group_b[i], router_expert_w[i],
                 router_expert_b[i], expert_w_gu[i], expert_w_down[i], norm_final,
                 final_norm=(i == depth - 1))
    return h.reshape(batch, seq, d)
```

```python
import functools

import jax
import jax.numpy as jnp
from jax import lax
from jax.experimental import pallas as pl
from jax.experimental.pallas import tpu as pltpu

F32 = jnp.float32
BF16 = jnp.bfloat16
I32 = jnp.int32

FOX_HEADS = 16
RET_HEADS = 4
RET_CHUNK = 128
N_GROUPS = 4
EXPERTS_PER_GROUP = 8
N_EXPERTS = N_GROUPS * EXPERTS_PER_GROUP
TOP_K = 2
EPS = 1e-6
NEG_INF = -1e30
ROPE_BASE = 10000.0

LANES = 128
SUBLANES = 8
ROUTE_LANE0 = N_GROUPS
VMEM_LIMIT = 56 * 1024 * 1024
MOE_BLOCK = 256
ROW_TILE = 512
WEIGHT_CHUNK = 512

_CONTRACT_LAST = (((1,), (1,)), ((), ()))
_CONTRACT_FIRST = (((0,), (0,)), ((), ()))


def _cparams(*sem):
    return pltpu.CompilerParams(dimension_semantics=sem, vmem_limit_bytes=VMEM_LIMIT)


def _sigmoid(x):
    return 1.0 / (1.0 + jnp.exp(-x))


def _rms(x, g, eps):
    ms = jnp.mean(x * x, axis=-1, keepdims=True)
    return x * lax.rsqrt(ms + eps) * g


def _split3(x):
    hi = x.astype(BF16)
    r = x - hi.astype(F32)
    mid = r.astype(BF16)
    lo = (r - mid.astype(F32)).astype(BF16)
    return hi, mid, lo


def _stage_weight(w_hbm, wb_ref, stage_ref, sem, ncols):
    chunk = stage_ref.shape[2]
    n = ncols // chunk

    def cp(c):
        return pltpu.make_async_copy(w_hbm.at[:, pl.ds(c * chunk, chunk)], stage_ref.at[c % 2], sem.at[c % 2])

    cp(0).start()
    for c in range(n):
        if c + 1 < n:
            cp(c + 1).start()
        cp(c).wait()
        wb_ref[:, c * chunk:(c + 1) * chunk] = stage_ref[c % 2].astype(BF16)


def _weight_scratch(kdim, ncols):
    return [pltpu.VMEM((kdim, ncols), BF16), pltpu.VMEM((2, kdim, WEIGHT_CHUNK), F32),
            pltpu.SemaphoreType.DMA((2,))]


def _fox_inproj_kernel(h_ref, g_ref, w_hbm, wf_ref, bf_ref, q_ref, kb_ref, v_ref, gate_ref, a_ref,
                       carry_ref, wb_ref, stage_ref, sem, *, layer, tiles_per_seq, scale):
    i = pl.program_id(0)
    tm, d = h_ref.shape

    @pl.when(i == 0)
    def _():
        _stage_weight(w_hbm.at[layer], wb_ref, stage_ref, sem, 4 * d)

    xn = _rms(h_ref[...], g_ref[...], EPS).astype(BF16)

    q = jnp.dot(xn, wb_ref[:, 0:d], preferred_element_type=F32)
    q_ref[...] = (q * scale).astype(BF16)
    v_ref[...] = jnp.dot(xn, wb_ref[:, 2 * d:3 * d], preferred_element_type=F32).astype(BF16)
    gate_ref[...] = jnp.dot(xn, wb_ref[:, 3 * d:4 * d], preferred_element_type=F32)

    f = jnp.dot(xn, wf_ref[...], preferred_element_type=F32) + bf_ref[...]
    ls = jnp.minimum(f, 0.0) - jnp.log1p(jnp.exp(-jnp.abs(f)))

    @pl.when(i % tiles_per_seq == 0)
    def _():
        carry_ref[...] = jnp.zeros_like(carry_ref)

    row = lax.broadcasted_iota(I32, (tm, tm), 0)
    col = lax.broadcasted_iota(I32, (tm, tm), 1)
    tri = jnp.where(col <= row, 1.0, 0.0).astype(BF16)
    hi, mid, lo = _split3(ls)
    cum = (jnp.dot(tri, hi, preferred_element_type=F32) + jnp.dot(tri, mid, preferred_element_type=F32)
           + jnp.dot(tri, lo, preferred_element_type=F32)) + carry_ref[...]
    carry_ref[...] = cum[tm - 1:tm, :]

    c_hi, c_mid, c_lo = (c.astype(F32) for c in _split3(cum))
    lane = lax.broadcasted_iota(I32, (tm, LANES), 1)
    a_ref[...] = jnp.where(lane < 16, c_hi, jnp.where(lane < 32, c_mid, jnp.where(
        lane < 48, c_lo, jnp.where(lane < 96, 1.0, 0.0)))).astype(BF16)
    bk = jnp.where(lane < 48, 1.0, jnp.where(lane < 64, -c_hi, jnp.where(
        lane < 80, -c_mid, jnp.where(lane < 96, -c_lo, 0.0)))).astype(BF16)

    k = jnp.dot(xn, wb_ref[:, d:2 * d], preferred_element_type=F32).astype(BF16)
    for p in range(d // LANES):
        kb_ref[:, 2 * p * LANES:(2 * p + 1) * LANES] = k[:, p * LANES:(p + 1) * LANES]
        kb_ref[:, (2 * p + 1) * LANES:(2 * p + 2) * LANES] = bk


def _fox_inproj(h, g, w_all, layer, wf, bfr, *, seq):
    t, d = h.shape
    tm = min(ROW_TILE, seq)
    scale = float((d // FOX_HEADS) ** -0.5)
    kern = functools.partial(_fox_inproj_kernel, layer=layer, tiles_per_seq=seq // tm, scale=scale)
    row = lambda i: (i, 0)
    const = lambda i: (0, 0)
    return pl.pallas_call(
        kern,
        grid=(t // tm,),
        in_specs=[pl.BlockSpec((tm, d), row), pl.BlockSpec((1, d), const),
                  pl.BlockSpec(memory_space=pl.ANY), pl.BlockSpec((d, LANES), const),
                  pl.BlockSpec((1, LANES), const)],
        out_specs=[pl.BlockSpec((tm, d), row), pl.BlockSpec((tm, 2 * d), row),
                   pl.BlockSpec((tm, d), row), pl.BlockSpec((tm, d), row),
                   pl.BlockSpec((tm, LANES), row)],
        out_shape=[jax.ShapeDtypeStruct((t, d), BF16), jax.ShapeDtypeStruct((t, 2 * d), BF16),
                   jax.ShapeDtypeStruct((t, d), BF16), jax.ShapeDtypeStruct((t, d), F32),
                   jax.ShapeDtypeStruct((t, LANES), BF16)],
        scratch_shapes=[pltpu.VMEM((1, LANES), F32)] + _weight_scratch(d, 4 * d),
        compiler_params=_cparams("arbitrary"),
        name="fox_inproj",
    )(h, g, w_all, wf, bfr)


def _fox_attn_kernel(q_ref, a_ref, kb_ref, v_ref, gate_ref, o_ref, *, tq, dh):
    seq = q_ref.shape[0]
    pair = pl.program_id(1)
    lane = lax.broadcasted_iota(I32, (tq, LANES), 1)
    row = lax.broadcasted_iota(I32, (tq, tq), 0)
    col = lax.broadcasted_iota(I32, (tq, tq), 1)
    causal = col <= row
    heads_per_block = LANES // dh
    qmask = [jnp.where((lane >= hh * dh) & (lane < (hh + 1) * dh), 1.0, 0.0).astype(BF16)
             for hh in range(heads_per_block)]
    amask = [jnp.where((lane < 96) & ((lane & 15) == pair * heads_per_block + hh), 1.0, 0.0).astype(BF16)
             for hh in range(heads_per_block)]
    for i in range(seq // tq):
        off = i * tq
        q = q_ref[off:off + tq, :]
        a = a_ref[off:off + tq, :]
        outs = []
        for hh in range(heads_per_block):
            qa = jnp.concatenate([q * qmask[hh], a * amask[hh]], axis=1)
            sd = lax.dot_general(qa, kb_ref[off:off + tq, :], _CONTRACT_LAST, preferred_element_type=F32)
            sd = jnp.where(causal, sd, NEG_INF)
            m = jnp.max(sd, axis=1, keepdims=True)
            if off > 0:
                so = lax.dot_general(qa, kb_ref[0:off, :], _CONTRACT_LAST, preferred_element_type=F32)
                m = jnp.maximum(m, jnp.max(so, axis=1, keepdims=True))
                eo = jnp.exp(so - m)
                ed = jnp.exp(sd - m)
                inv = 1.0 / (jnp.sum(eo, axis=1, keepdims=True) + jnp.sum(ed, axis=1, keepdims=True))
                o = (jnp.dot((eo * inv).astype(BF16), v_ref[0:off, :], preferred_element_type=F32)
                     + jnp.dot((ed * inv).astype(BF16), v_ref[off:off + tq, :], preferred_element_type=F32))
            else:
                ed = jnp.exp(sd - m)
                inv = 1.0 / jnp.sum(ed, axis=1, keepdims=True)
                o = jnp.dot((ed * inv).astype(BF16), v_ref[off:off + tq, :], preferred_element_type=F32)
            outs.append(o)
        o = outs[0]
        for hh in range(1, heads_per_block):
            o = jnp.where(lane < hh * dh, o, outs[hh])
        o_ref[off:off + tq, :] = (o * _sigmoid(gate_ref[off:off + tq, :])).astype(BF16)


def _fox_attn(q, a, kb, v, gate, *, batch, seq, tq=128):
    t, d = q.shape
    dh = d // FOX_HEADS
    pairs = d // LANES
    kern = functools.partial(_fox_attn_kernel, tq=tq, dh=dh)
    blk = lambda b, p: (b, p)
    return pl.pallas_call(
        kern,
        grid=(batch, pairs),
        in_specs=[pl.BlockSpec((seq, LANES), blk), pl.BlockSpec((seq, LANES), lambda b, p: (b, 0)),
                  pl.BlockSpec((seq, 2 * LANES), blk), pl.BlockSpec((seq, LANES), blk),
                  pl.BlockSpec((seq, LANES), blk)],
        out_specs=pl.BlockSpec((seq, LANES), blk),
        out_shape=jax.ShapeDtypeStruct((t, d), BF16),
        compiler_params=_cparams("arbitrary", "arbitrary"),
        name="fox_attn",
    )(q, a, kb, v, gate)


def _ret_inproj_kernel(h_ref, g_ref, w_hbm, cos_ref, sin_ref, q_ref, k_ref, v_ref, gate_ref,
                       wb_ref, stage_ref, sem, *, layer, kscale):
    tm, d = h_ref.shape

    @pl.when(pl.program_id(0) == 0)
    def _():
        _stage_weight(w_hbm.at[layer], wb_ref, stage_ref, sem, 6 * d)

    xn = _rms(h_ref[...], g_ref[...], EPS).astype(BF16)
    cos = cos_ref[...]
    sin = sin_ref[...]
    dk = d // RET_HEADS
    half = dk // 2

    def rot(t, h):
        t1 = t[:, h * dk:h * dk + half]
        t2 = t[:, h * dk + half:(h + 1) * dk]
        return t1 * cos - t2 * sin, t1 * sin + t2 * cos

    q = jnp.dot(xn, wb_ref[:, 0:d], preferred_element_type=F32)
    k = jnp.dot(xn, wb_ref[:, d:2 * d], preferred_element_type=F32)
    for h in range(RET_HEADS):
        q1, q2 = rot(q, h)
        q_ref[:, h * dk:h * dk + half] = q1.astype(BF16)
        q_ref[:, h * dk + half:(h + 1) * dk] = q2.astype(BF16)
        k1, k2 = rot(k, h)
        k_ref[:, h * dk:h * dk + half] = k1 * kscale
        k_ref[:, h * dk + half:(h + 1) * dk] = k2 * kscale
    v_ref[...] = jnp.dot(xn, wb_ref[:, 2 * d:4 * d], preferred_element_type=F32).astype(BF16)
    gate_ref[...] = jnp.dot(xn, wb_ref[:, 4 * d:6 * d], preferred_element_type=F32)


def _ret_inproj(h, g, w_all, layer, cos, sin, *, seq, tm=256):
    t, d = h.shape
    tm = min(tm, seq)
    half = d // RET_HEADS // 2
    kern = functools.partial(_ret_inproj_kernel, layer=layer, kscale=float((d // RET_HEADS) ** -0.5))
    row = lambda i: (i, 0)
    const = lambda i: (0, 0)
    pos = lambda i: (i % (seq // tm), 0)
    return pl.pallas_call(
        kern,
        grid=(t // tm,),
        in_specs=[pl.BlockSpec((tm, d), row), pl.BlockSpec((1, d), const), pl.BlockSpec(memory_space=pl.ANY),
                  pl.BlockSpec((tm, half), pos), pl.BlockSpec((tm, half), pos)],
        out_specs=[pl.BlockSpec((tm, d), row), pl.BlockSpec((tm, d), row),
                   pl.BlockSpec((tm, 2 * d), row), pl.BlockSpec((tm, 2 * d), row)],
        out_shape=[jax.ShapeDtypeStruct((t, d), BF16), jax.ShapeDtypeStruct((t, d), F32),
                   jax.ShapeDtypeStruct((t, 2 * d), BF16), jax.ShapeDtypeStruct((t, 2 * d), F32)],
        scratch_shapes=_weight_scratch(d, 6 * d),
        compiler_params=_cparams("arbitrary"),
        name="ret_inproj",
    )(h, g, w_all, cos, sin)


def _retention_kernel(q_ref, k_ref, v_ref, gate_ref, gain_ref, intra_ref, qd_ref, kd_ref, cd_ref,
                      y_ref, r_ref, *, chunk):
    seq = q_ref.shape[0]
    r_ref[...] = jnp.zeros_like(r_ref)

    def body(c, carry):
        off = pl.multiple_of(c * chunk, chunk)
        qn = q_ref[pl.ds(off, chunk), :]
        k32 = k_ref[pl.ds(off, chunk), :]
        vn = v_ref[pl.ds(off, chunk), :]
        s = lax.dot_general(qn, k32.astype(BF16), _CONTRACT_LAST, preferred_element_type=F32) * intra_ref[...]
        o = jnp.dot(s.astype(BF16), vn, preferred_element_type=F32)
        r = r_ref[...]
        o = o + jnp.dot(qn, r.astype(BF16), preferred_element_type=F32) * qd_ref[...]
        kdec = (k32 * kd_ref[...]).astype(BF16)
        r_ref[...] = r * cd_ref[...] + lax.dot_general(kdec, vn, _CONTRACT_FIRST, preferred_element_type=F32)
        mu = jnp.mean(o, axis=-1, keepdims=True)
        dlt = o - mu
        var = jnp.mean(dlt * dlt, axis=-1, keepdims=True)
        on = dlt * lax.rsqrt(var + EPS) * gain_ref[...]
        g = gate_ref[pl.ds(off, chunk), :]
        y_ref[pl.ds(off, chunk), :] = ((g * _sigmoid(g)) * on).astype(BF16)
        return carry

    lax.fori_loop(0, seq // chunk, body, 0)


def _retention(q, k, v, gate, gain, intra, qd, kd, cd, *, batch, seq):
    t, d = q.shape
    dk = d // RET_HEADS
    dv = v.shape[1] // RET_HEADS
    c = RET_CHUNK
    kern = functools.partial(_retention_kernel, chunk=c)
    blk = lambda b, h: (b, h)
    hd = lambda b, h: (h, 0, 0)
    return pl.pallas_call(
        kern,
        grid=(batch, RET_HEADS),
        in_specs=[pl.BlockSpec((seq, dk), blk), pl.BlockSpec((seq, dk), blk), pl.BlockSpec((seq, dv), blk),
                  pl.BlockSpec((seq, dv), blk), pl.BlockSpec((1, dv), lambda b, h: (0, h)),
                  pl.BlockSpec((None, c, c), hd), pl.BlockSpec((None, c, 1), hd),
                  pl.BlockSpec((None, c, 1), hd), pl.BlockSpec((None, 1, 1), hd)],
        out_specs=pl.BlockSpec((seq, dv), blk),
        out_shape=jax.ShapeDtypeStruct((t, v.shape[1]), BF16),
        scratch_shapes=[pltpu.VMEM((dk, dv), F32)],
        compiler_params=_cparams("arbitrary", "arbitrary"),
        name="retention",
    )(q, k, v, gate, gain, intra, qd, kd, cd)


def _outproj_router_kernel(a_ref, w_hbm, h_ref, g_ref, wrh_ref, wrl_ref, br_ref,
                           h1_ref, hn_ref, route_ref, meta_ref, cnt_ref, wb_ref, stage_ref, sem, *, layer):
    i = pl.program_id(0)
    tm = a_ref.shape[0]
    d = h_ref.shape[1]

    @pl.when(i == 0)
    def _():
        cnt_ref[...] = jnp.zeros_like(cnt_ref)
        _stage_weight(w_hbm.at[layer], wb_ref, stage_ref, sem, d)

    h1 = h_ref[...] + jnp.dot(a_ref[...], wb_ref[...], preferred_element_type=F32)
    h1_ref[...] = h1
    hn = _rms(h1, g_ref[...], EPS)
    hn_ref[...] = hn

    hh = hn.astype(BF16)
    hl = (hn - hh.astype(F32)).astype(BF16)
    lg = (jnp.dot(hh, wrh_ref[...], preferred_element_type=F32)
          + jnp.dot(hh, wrl_ref[...], preferred_element_type=F32)
          + jnp.dot(hl, wrh_ref[...], preferred_element_type=F32)) + br_ref[...]

    lane = lax.broadcasted_iota(I32, (tm, LANES), 1)
    lanef = lane.astype(F32)
    big = float(LANES)

    def softmax_masked(mask):
        mx = jnp.max(jnp.where(mask, lg, -jnp.inf), axis=1, keepdims=True)
        e = jnp.where(mask, jnp.exp(lg - mx), 0.0)
        return e / jnp.sum(e, axis=1, keepdims=True)

    def top1(p, mask):
        best = jnp.max(jnp.where(mask, p, -1.0), axis=1, keepdims=True)
        idx = jnp.min(jnp.where(mask & (p == best), lanef, big), axis=1, keepdims=True)
        return best, idx

    gmask = lane < N_GROUPS
    pg, gi = top1(softmax_masked(gmask), gmask)
    lo = ROUTE_LANE0 + gi * EXPERTS_PER_GROUP
    emask = (lanef >= lo) & (lanef < lo + EXPERTS_PER_GROUP)
    pe = softmax_masked(emask)
    p1, i1 = top1(pe, emask)
    p2, i2 = top1(pe, emask & (lanef != i1))
    den = p1 + p2
    g1 = pg * p1 / den
    g2 = pg * p2 / den

    oh1 = lanef == i1
    oh2 = lanef == i2
    both = jnp.where(oh1 | oh2, 1.0, 0.0)
    row = lax.broadcasted_iota(I32, (tm, tm), 0)
    col = lax.broadcasted_iota(I32, (tm, tm), 1)
    strict = jnp.where(col < row, 1.0, 0.0).astype(BF16)
    before = jnp.dot(strict, both.astype(BF16), preferred_element_type=F32) + cnt_ref[0:1, :]
    r1 = jnp.sum(jnp.where(oh1, before, 0.0), axis=1, keepdims=True)
    r2 = jnp.sum(jnp.where(oh2, before, 0.0), axis=1, keepdims=True)
    cnt_ref[0:1, :] = cnt_ref[0:1, :] + jnp.sum(both, axis=0, keepdims=True)

    e1 = i1 - ROUTE_LANE0
    e2 = i2 - ROUTE_LANE0
    route = jnp.where(lane == 0, e1, jnp.where(lane == 1, e2, jnp.where(
        lane == 2, r1, jnp.where(lane == 3, r2, jnp.where(lane == 4, g1, jnp.where(lane == 5, g2, 0.0))))))
    route_ref[...] = route
    meta_ref[...] = jnp.transpose(route)[0:SUBLANES, :].astype(I32)


def _outproj_router(a, w_all, layer, h, g, wrh, wrl, br, *, tm):
    t, kdim = a.shape
    d = h.shape[1]
    row = lambda i: (i, 0)
    const = lambda i: (0, 0)
    kern = functools.partial(_outproj_router_kernel, layer=layer)
    return pl.pallas_call(
        kern,
        grid=(t // tm,),
        in_specs=[pl.BlockSpec((tm, kdim), row), pl.BlockSpec(memory_space=pl.ANY), pl.BlockSpec((tm, d), row),
                  pl.BlockSpec((1, d), const), pl.BlockSpec((d, LANES), const), pl.BlockSpec((d, LANES), const),
                  pl.BlockSpec((1, LANES), const)],
        out_specs=[pl.BlockSpec((tm, d), row), pl.BlockSpec((tm, d), row), pl.BlockSpec((tm, LANES), row),
                   pl.BlockSpec((None, SUBLANES, tm), lambda i: (i, 0, 0)),
                   pl.BlockSpec((SUBLANES, LANES), const)],
        out_shape=[jax.ShapeDtypeStruct((t, d), F32), jax.ShapeDtypeStruct((t, d), F32),
                   jax.ShapeDtypeStruct((t, LANES), F32), jax.ShapeDtypeStruct((t // tm, SUBLANES, tm), I32),
                   jax.ShapeDtypeStruct((SUBLANES, LANES), F32)],
        scratch_shapes=_weight_scratch(kdim, d),
        compiler_params=_cparams("arbitrary"),
        name="outproj_router",
    )(a, w_all, h, g, wrh, wrl, br)


def _pad_copies(counts_ref, pstart_ref, zero_ref, xs_ref, sem, *, bm):
    out = []
    for e in range(N_EXPERTS):
        cnt = counts_ref[e]
        npad = (bm - (cnt & (bm - 1))) & (bm - 1)
        pos = pstart_ref[e] + cnt
        for r in range(SUBLANES - 1):
            out.append(((npad & (SUBLANES - 1)) > r,
                        pltpu.make_async_copy(zero_ref.at[pl.ds(0, 1), :], xs_ref.at[pl.ds(pos + r, 1), :], sem)))
        pos = pos + (npad & (SUBLANES - 1))
        bit = SUBLANES
        while bit < bm:
            dst = xs_ref.at[pl.ds(pl.multiple_of(pos, SUBLANES), bit), :]
            out.append(((npad & bit) != 0, pltpu.make_async_copy(zero_ref.at[pl.ds(0, bit), :], dst, sem)))
            pos = pos + (npad & bit)
            bit *= 2
    return out


def _dispatch_kernel(counts_ref, pstart_ref, nused_ref, meta_ref, hn_ref, xs_ref, zero_ref, sem, psem, *, bm):
    tm = hn_ref.shape[0]
    half = bm // 2

    @pl.when(pl.program_id(0) == 0)
    def _():
        zero_ref[...] = jnp.zeros_like(zero_ref)
        pads = _pad_copies(counts_ref, pstart_ref, zero_ref, xs_ref, psem, bm=bm)
        for pred, cp in pads:
            pl.when(pred)(cp.start)

        def tail_copy(j):
            return pltpu.make_async_copy(zero_ref, xs_ref.at[pl.ds(pl.multiple_of(j * half, half), half), :], psem)

        first = nused_ref[0] * 2
        last = xs_ref.shape[0] // half
        lax.fori_loop(first, last, lambda j, c: (tail_copy(j).start(), c)[1], 0)
        for pred, cp in pads:
            pl.when(pred)(cp.wait)
        lax.fori_loop(first, last, lambda j, c: (tail_copy(j).wait(), c)[1], 0)

    def row_copy(t, k):
        dst = pstart_ref[meta_ref[k, t]] + meta_ref[TOP_K + k, t]
        return pltpu.make_async_copy(hn_ref.at[pl.ds(t, 1), :], xs_ref.at[pl.ds(dst, 1), :], sem)

    def issue(t, carry):
        for k in range(TOP_K):
            row_copy(t, k).start(priority=k)
        return carry

    lax.fori_loop(0, tm, issue, 0, unroll=8)

    def drain(t, carry):
        for k in range(TOP_K):
            row_copy(t, k).wait()
        return carry

    lax.fori_loop(0, tm, drain, 0, unroll=8)


def _dispatch(counts, pstart, n_used, meta, hn, n_rows, *, bm):
    t, d = hn.shape
    tm = meta.shape[2]
    kern = functools.partial(_dispatch_kernel, bm=bm)
    grid_spec = pltpu.PrefetchScalarGridSpec(
        num_scalar_prefetch=3,
        grid=(t // tm,),
        in_specs=[pl.BlockSpec((None, SUBLANES, tm), lambda i, *_: (i, 0, 0), memory_space=pltpu.SMEM),
                  pl.BlockSpec((tm, d), lambda i, *_: (i, 0))],
        out_specs=pl.BlockSpec(memory_space=pl.ANY),
        scratch_shapes=[pltpu.VMEM((bm // 2, d), F32), pltpu.SemaphoreType.DMA(()), pltpu.SemaphoreType.DMA(())],
    )
    return pl.pallas_call(
        kern,
        grid_spec=grid_spec,
        out_shape=jax.ShapeDtypeStruct((n_rows, d), F32),
        compiler_params=_cparams("arbitrary"),
        name="moe_dispatch",
    )(counts, pstart, n_used, meta, hn)


def _experts_kernel(be_ref, nu_ref, x_ref, wgu_ref, wd_ref, y_ref, wgu_b, wd_b):
    i = pl.program_id(0)
    prev = be_ref[jnp.maximum(i - 1, 0)]

    @pl.when((i < nu_ref[0]) & ((i == 0) | (be_ref[i] != prev)))
    def _():
        wgu_b[...] = wgu_ref[...].astype(BF16)
        wd_b[...] = wd_ref[...].astype(BF16)

    @pl.when(i < nu_ref[0])
    def _():
        x = x_ref[...].astype(BF16)
        hcat = jnp.dot(x, wgu_b[...], preferred_element_type=F32)
        ff = hcat.shape[1] // 2
        a = hcat[:, :ff]
        b = hcat[:, ff:]
        act = ((a * _sigmoid(a)) * b).astype(BF16)
        y_ref[...] = jnp.dot(act, wd_b[...], preferred_element_type=F32)

    @pl.when(i >= nu_ref[0])
    def _():
        y_ref[...] = jnp.zeros_like(y_ref)


def _experts(block_e, n_used, xs, wgu_all, wd_all, layer, *, bm):
    n_rows, d = xs.shape
    ff2 = wgu_all.shape[3]
    grid_spec = pltpu.PrefetchScalarGridSpec(
        num_scalar_prefetch=2,
        grid=(n_rows // bm,),
        in_specs=[pl.BlockSpec((bm, d), lambda i, be, nu: (i, 0)),
                  pl.BlockSpec((None, None, d, ff2), lambda i, be, nu: (layer, be[i], 0, 0)),
                  pl.BlockSpec((None, None, ff2 // 2, d), lambda i, be, nu: (layer, be[i], 0, 0))],
        out_specs=pl.BlockSpec((bm, d), lambda i, be, nu: (i, 0)),
        scratch_shapes=[pltpu.VMEM((d, ff2), BF16), pltpu.VMEM((ff2 // 2, d), BF16)],
    )
    return pl.pallas_call(
        _experts_kernel,
        grid_spec=grid_spec,
        out_shape=jax.ShapeDtypeStruct((n_rows, d), F32),
        compiler_params=_cparams("arbitrary"),
        name="moe_experts",
    )(block_e, n_used, xs, wgu_all, wd_all)


def _combine_kernel(pstart_ref, meta_ref, h1_ref, route_ref, gfin_ref, ys_ref, o_ref, buf_ref, sem,
                    *, final_norm):
    tm = h1_ref.shape[0]

    def row_copy(t, k):
        src = pstart_ref[meta_ref[k, t]] + meta_ref[TOP_K + k, t]
        return pltpu.make_async_copy(ys_ref.at[pl.ds(src, 1), :], buf_ref.at[k, pl.ds(t, 1), :], sem)

    def issue(t, carry):
        for k in range(TOP_K):
            row_copy(t, k).start(priority=k)
        return carry

    lax.fori_loop(0, tm, issue, 0, unroll=8)

    def drain(t, carry):
        for k in range(TOP_K):
            row_copy(t, k).wait()
        return carry

    lax.fori_loop(0, tm, drain, 0, unroll=8)

    g0 = route_ref[:, 4:5]
    g1 = route_ref[:, 5:6]
    out = h1_ref[...] + (buf_ref[0] * g0 + buf_ref[1] * g1)
    if final_norm:
        out = _rms(out, gfin_ref[...], EPS)
    o_ref[...] = out


def _combine(pstart, meta, h1, route, gfin, ys, *, final_norm):
    t, d = h1.shape
    tm = meta.shape[2]
    kern = functools.partial(_combine_kernel, final_norm=final_norm)
    grid_spec = pltpu.PrefetchScalarGridSpec(
        num_scalar_prefetch=1,
        grid=(t // tm,),
        in_specs=[pl.BlockSpec((None, SUBLANES, tm), lambda i, p: (i, 0, 0), memory_space=pltpu.SMEM),
                  pl.BlockSpec((tm, d), lambda i, p: (i, 0)),
                  pl.BlockSpec((tm, LANES), lambda i, p: (i, 0)),
                  pl.BlockSpec((1, d), lambda i, p: (0, 0)),
                  pl.BlockSpec(memory_space=pl.ANY)],
        out_specs=pl.BlockSpec((tm, d), lambda i, p: (i, 0)),
        scratch_shapes=[pltpu.VMEM((TOP_K, tm, d), F32), pltpu.SemaphoreType.DMA(())],
    )
    return pl.pallas_call(
        kern,
        grid_spec=grid_spec,
        out_shape=jax.ShapeDtypeStruct((t, d), F32),
        compiler_params=_cparams("arbitrary"),
        name="moe_combine",
    )(pstart, meta, h1, route, gfin, ys)


def _moe(a, w_out_all, mixer_layer, h, layer, g_ffn, wr_g, br_g, wr_e, br_e, w_gu_all, w_down_all, gfin,
         *, final_norm):
    t, d = h.shape
    bm = MOE_BLOCK
    tm = min(ROW_TILE, t)
    wr = jnp.concatenate([wr_g, wr_e, jnp.zeros((d, LANES - N_GROUPS - N_EXPERTS), F32)], axis=1)
    wrh = wr.astype(BF16)
    wrl = (wr - wrh.astype(F32)).astype(BF16)
    br = jnp.concatenate([br_g, br_e, jnp.zeros((LANES - N_GROUPS - N_EXPERTS,), F32)])[None, :]
    h1, hn, route, meta, cnt = _outproj_router(a, w_out_all, mixer_layer, h, g_ffn[None, :], wrh, wrl, br, tm=tm)

    counts = cnt[0, ROUTE_LANE0:ROUTE_LANE0 + N_EXPERTS].astype(I32)
    nb = (counts + bm - 1) // bm
    nb_end = jnp.cumsum(nb)
    pstart = ((nb_end - nb) * bm).astype(I32)
    n_blocks = (t * TOP_K + N_EXPERTS * (bm - 1)) // bm
    j = jnp.arange(n_blocks, dtype=I32)
    block_e = jnp.minimum(jnp.sum((nb_end[None, :] <= j[:, None]).astype(I32), axis=1), N_EXPERTS - 1)
    n_used = nb_end[-1:].astype(I32)

    xs = _dispatch(counts, pstart, n_used, meta, hn, n_blocks * bm, bm=bm)
    ys = _experts(block_e, n_used, xs, w_gu_all, w_down_all, layer, bm=bm)
    return _combine(pstart, meta, h1, route, gfin[None, :], ys, final_norm=final_norm)


def _fox_mixer(h, g, w_in_all, layer, b_f, *, batch, seq):
    d = h.shape[1]
    nh = FOX_HEADS
    wf = w_in_all[layer, :, 4 * d:4 * d + nh]
    wf = jnp.concatenate([wf] * 6 + [jnp.zeros((d, LANES - 6 * nh), F32)], axis=1).astype(BF16)
    bfr = jnp.concatenate([b_f] * 6 + [jnp.zeros((LANES - 6 * nh,), F32)])[None, :]
    q, kb, v, gate, a = _fox_inproj(h, g[None, :], w_in_all, layer, wf, bfr, seq=seq)
    return _fox_attn(q, a, kb, v, gate, batch=batch, seq=seq)


def _ret_mixer(h, g, w_in_all, layer, gn_gain, *, batch, seq):
    d = h.shape[1]
    nh = RET_HEADS
    dk = d // nh
    c = RET_CHUNK
    inv = 1.0 / (ROPE_BASE ** jnp.linspace(0.0, 1.0, dk // 2, dtype=F32))
    ang = jnp.arange(seq).astype(F32)[:, None] * inv[None, :]
    cos, sin = jnp.cos(ang), jnp.sin(ang)
    log_g = jnp.log(1.0 - 2.0 ** (-5.0 - jnp.arange(nh, dtype=F32)))
    idx = jnp.arange(c, dtype=F32)
    rel = idx[:, None] - idx[None, :]
    intra = jnp.where(rel >= 0, jnp.exp(log_g[:, None, None] * jnp.maximum(rel, 0.0)), 0.0)
    qd = jnp.exp(log_g[:, None] * (idx + 1.0))[:, :, None]
    kd = jnp.exp(log_g[:, None] * (c - 1.0 - idx))[:, :, None]
    cd = jnp.exp(log_g * c)[:, None, None]
    q, k, v, gate = _ret_inproj(h, g[None, :], w_in_all, layer, cos, sin, seq=seq)
    return _retention(q, k, v, gate, gn_gain[None, :], intra, qd, kd, cd, batch=batch, seq=seq)


def kernel(x, fox_w_in, fox_b_f, fox_w_out, ret_w_in, ret_gn_gain, ret_w_out, norm_mix, norm_ffn,
           router_group_w, router_group_b, router_expert_w, router_expert_b, expert_w_gu, expert_w_down,
           norm_final):
    batch, seq, d = x.shape
    depth = norm_mix.shape[0]
    h = x.reshape(batch * seq, d)
    for i in range(depth):
        j = i // 2
        if i % 2 == 0:
            a = _fox_mixer(h, norm_mix[i], fox_w_in, j, fox_b_f[j], batch=batch, seq=seq)
            w_out_all = fox_w_out
        else:
            a = _ret_mixer(h, norm_mix[i], ret_w_in, j, ret_gn_gain[j], batch=batch, seq=seq)
            w_out_all = ret_w_out
        h = _moe(a, w_out_all, j, h, i, norm_ffn[i], router_group_w[i], router_group_b[i], router_expert_w[i],
                 router_expert_b[i], expert_w_gu, expert_w_down, norm_final, final_norm=(i == depth - 1))
    return h.reshape(batch, seq, d)
```

```python
import functools

import jax
import jax.numpy as jnp
from jax import lax
from jax.experimental import pallas as pl
from jax.experimental.pallas import tpu as pltpu

F32 = jnp.float32
BF16 = jnp.bfloat16
I32 = jnp.int32

FOX_HEADS = 16
RET_HEADS = 4
RET_CHUNK = 128
N_GROUPS = 4
EXPERTS_PER_GROUP = 8
N_EXPERTS = N_GROUPS * EXPERTS_PER_GROUP
TOP_K = 2
EPS = 1e-6
NEG_INF = -1e30
ROPE_BASE = 10000.0

LANES = 128
SUBLANES = 8
ROUTE_LANE0 = N_GROUPS
VMEM_LIMIT = 56 * 1024 * 1024
MOE_BLOCK = 256
ROW_TILE = 512
WEIGHT_CHUNK = 512

_CONTRACT_LAST = (((1,), (1,)), ((), ()))
_CONTRACT_FIRST = (((0,), (0,)), ((), ()))


def _cparams(*sem):
    return pltpu.CompilerParams(dimension_semantics=sem, vmem_limit_bytes=VMEM_LIMIT)


def _sigmoid(x):
    return 1.0 / (1.0 + jnp.exp(-x))


def _rms(x, g, eps):
    ms = jnp.mean(x * x, axis=-1, keepdims=True)
    return x * lax.rsqrt(ms + eps) * g


def _split3(x):
    hi = x.astype(BF16)
    r = x - hi.astype(F32)
    mid = r.astype(BF16)
    lo = (r - mid.astype(F32)).astype(BF16)
    return hi, mid, lo


def _stage_weight(w_hbm, wb_ref, stage_ref, sem, ncols):
    chunk = stage_ref.shape[2]
    n = ncols // chunk

    def cp(c):
        return pltpu.make_async_copy(w_hbm.at[:, pl.ds(c * chunk, chunk)], stage_ref.at[c % 2], sem.at[c % 2])

    cp(0).start()
    for c in range(n):
        if c + 1 < n:
            cp(c + 1).start()
        cp(c).wait()
        wb_ref[:, c * chunk:(c + 1) * chunk] = stage_ref[c % 2].astype(BF16)


def _weight_scratch(kdim, ncols):
    return [pltpu.VMEM((kdim, ncols), BF16), pltpu.VMEM((2, kdim, WEIGHT_CHUNK), F32),
            pltpu.SemaphoreType.DMA((2,))]


def _fox_inproj_kernel(h_ref, g_ref, w_hbm, wf_ref, bf_ref, q_ref, kb_ref, v_ref, gate_ref, a_ref,
                       carry_ref, wb_ref, stage_ref, sem, *, layer, tiles_per_seq, scale):
    i = pl.program_id(0)
    tm, d = h_ref.shape

    @pl.when(i == 0)
    def _():
        _stage_weight(w_hbm.at[layer], wb_ref, stage_ref, sem, 4 * d)

    xn = _rms(h_ref[...], g_ref[...], EPS).astype(BF16)

    q = jnp.dot(xn, wb_ref[:, 0:d], preferred_element_type=F32)
    q_ref[...] = (q * scale).astype(BF16)
    v_ref[...] = jnp.dot(xn, wb_ref[:, 2 * d:3 * d], preferred_element_type=F32).astype(BF16)
    gate_ref[...] = jnp.dot(xn, wb_ref[:, 3 * d:4 * d], preferred_element_type=F32)

    f = jnp.dot(xn, wf_ref[...], preferred_element_type=F32) + bf_ref[...]
    ls = jnp.minimum(f, 0.0) - jnp.log1p(jnp.exp(-jnp.abs(f)))

    @pl.when(i % tiles_per_seq == 0)
    def _():
        carry_ref[...] = jnp.zeros_like(carry_ref)

    row = lax.broadcasted_iota(I32, (tm, tm), 0)
    col = lax.broadcasted_iota(I32, (tm, tm), 1)
    tri = jnp.where(col <= row, 1.0, 0.0).astype(BF16)
    hi, mid, lo = _split3(ls)
    cum = (jnp.dot(tri, hi, preferred_element_type=F32) + jnp.dot(tri, mid, preferred_element_type=F32)
           + jnp.dot(tri, lo, preferred_element_type=F32)) + carry_ref[...]
    carry_ref[...] = cum[tm - 1:tm, :]

    c_hi, c_mid, c_lo = (c.astype(F32) for c in _split3(cum))
    lane = lax.broadcasted_iota(I32, (tm, LANES), 1)
    a_ref[...] = jnp.where(lane < 16, c_hi, jnp.where(lane < 32, c_mid, jnp.where(
        lane < 48, c_lo, jnp.where(lane < 96, 1.0, 0.0)))).astype(BF16)
    bk = jnp.where(lane < 48, 1.0, jnp.where(lane < 64, -c_hi, jnp.where(
        lane < 80, -c_mid, jnp.where(lane < 96, -c_lo, 0.0)))).astype(BF16)

    k = jnp.dot(xn, wb_ref[:, d:2 * d], preferred_element_type=F32).astype(BF16)
    for p in range(d // LANES):
        kb_ref[:, 2 * p * LANES:(2 * p + 1) * LANES] = k[:, p * LANES:(p + 1) * LANES]
        kb_ref[:, (2 * p + 1) * LANES:(2 * p + 2) * LANES] = bk


def _fox_inproj(h, g, w_all, layer, wf, bfr, *, seq):
    t, d = h.shape
    tm = min(ROW_TILE, seq)
    scale = float((d // FOX_HEADS) ** -0.5)
    kern = functools.partial(_fox_inproj_kernel, layer=layer, tiles_per_seq=seq // tm, scale=scale)
    row = lambda i: (i, 0)
    const = lambda i: (0, 0)
    return pl.pallas_call(
        kern,
        grid=(t // tm,),
        in_specs=[pl.BlockSpec((tm, d), row), pl.BlockSpec((1, d), const),
                  pl.BlockSpec(memory_space=pl.ANY), pl.BlockSpec((d, LANES), const),
                  pl.BlockSpec((1, LANES), const)],
        out_specs=[pl.BlockSpec((tm, d), row), pl.BlockSpec((tm, 2 * d), row),
                   pl.BlockSpec((tm, d), row), pl.BlockSpec((tm, d), row),
                   pl.BlockSpec((tm, LANES), row)],
        out_shape=[jax.ShapeDtypeStruct((t, d), BF16), jax.ShapeDtypeStruct((t, 2 * d), BF16),
                   jax.ShapeDtypeStruct((t, d), BF16), jax.ShapeDtypeStruct((t, d), F32),
                   jax.ShapeDtypeStruct((t, LANES), BF16)],
        scratch_shapes=[pltpu.VMEM((1, LANES), F32)] + _weight_scratch(d, 4 * d),
        compiler_params=_cparams("arbitrary"),
        name="fox_inproj",
    )(h, g, w_all, wf, bfr)


def _fox_attn_kernel(q_ref, a_ref, kb_ref, v_ref, gate_ref, o_ref, *, tq, dh, depth):
    seq = q_ref.shape[0]
    pair = pl.program_id(1)
    lane = lax.broadcasted_iota(I32, (tq, LANES), 1)
    krow = lax.broadcasted_iota(I32, (tq, tq), 0)
    qcol = lax.broadcasted_iota(I32, (tq, tq), 1)
    causal = krow <= qcol
    heads_per_block = LANES // dh
    qmask = [jnp.where((lane >= hh * dh) & (lane < (hh + 1) * dh), 1.0, 0.0).astype(BF16)
             for hh in range(heads_per_block)]
    amask = [jnp.where((lane < 96) & ((lane & 15) == pair * heads_per_block + hh), 1.0, 0.0).astype(BF16)
             for hh in range(heads_per_block)]
    vt = jnp.transpose(v_ref[...])
    chains = [(i, hh) for i in range(seq // tq) for hh in range(heads_per_block)]

    def scores(i, hh):
        off = i * tq
        qa = jnp.concatenate([q_ref[off:off + tq, :] * qmask[hh], a_ref[off:off + tq, :] * amask[hh]], axis=1)
        return lax.dot_general(kb_ref[0:off + tq, :], qa, _CONTRACT_LAST, preferred_element_type=F32)

    def finish(i, hh, st):
        off = i * tq
        sd = jnp.where(causal, st[off:off + tq, :], NEG_INF)
        st = jnp.concatenate([st[0:off, :], sd], axis=0) if off > 0 else sd
        m = jnp.max(st, axis=0, keepdims=True)
        e = jnp.exp(st - m)
        p = (e * (1.0 / jnp.sum(e, axis=0, keepdims=True))).astype(BF16)
        return jnp.dot(vt[hh * dh:(hh + 1) * dh, 0:off + tq], p, preferred_element_type=F32)

    outs = {}
    queue = [scores(*c) for c in chains[:depth]]
    for n, (i, hh) in enumerate(chains):
        cur = queue.pop(0)
        if n + depth < len(chains):
            queue.append(scores(*chains[n + depth]))
        outs[hh] = finish(i, hh, cur)
        if hh == heads_per_block - 1:
            off = i * tq
            o = jnp.transpose(jnp.concatenate([outs[h2] for h2 in range(heads_per_block)], axis=0))
            o_ref[off:off + tq, :] = (o * _sigmoid(gate_ref[off:off + tq, :])).astype(BF16)


def _fox_attn(q, a, kb, v, gate, *, batch, seq, tq=256, depth=3):
    t, d = q.shape
    dh = d // FOX_HEADS
    pairs = d // LANES
    kern = functools.partial(_fox_attn_kernel, tq=tq, dh=dh, depth=depth)
    blk = lambda b, p: (b, p)
    return pl.pallas_call(
        kern,
        grid=(batch, pairs),
        in_specs=[pl.BlockSpec((seq, LANES), blk), pl.BlockSpec((seq, LANES), lambda b, p: (b, 0)),
                  pl.BlockSpec((seq, 2 * LANES), blk), pl.BlockSpec((seq, LANES), blk),
                  pl.BlockSpec((seq, LANES), blk)],
        out_specs=pl.BlockSpec((seq, LANES), blk),
        out_shape=jax.ShapeDtypeStruct((t, d), BF16),
        compiler_params=_cparams("arbitrary", "arbitrary"),
        name="fox_attn",
    )(q, a, kb, v, gate)


def _ret_inproj_kernel(h_ref, g_ref, w_hbm, cos_ref, sin_ref, q_ref, k_ref, v_ref, gate_ref,
                       wb_ref, stage_ref, sem, *, layer, kscale):
    tm, d = h_ref.shape

    @pl.when(pl.program_id(0) == 0)
    def _():
        _stage_weight(w_hbm.at[layer], wb_ref, stage_ref, sem, 6 * d)

    xn = _rms(h_ref[...], g_ref[...], EPS).astype(BF16)
    cos = cos_ref[...]
    sin = sin_ref[...]
    dk = d // RET_HEADS
    half = dk // 2

    def rot(t, h):
        t1 = t[:, h * dk:h * dk + half]
        t2 = t[:, h * dk + half:(h + 1) * dk]
        return t1 * cos - t2 * sin, t1 * sin + t2 * cos

    q = jnp.dot(xn, wb_ref[:, 0:d], preferred_element_type=F32)
    k = jnp.dot(xn, wb_ref[:, d:2 * d], preferred_element_type=F32)
    for h in range(RET_HEADS):
        q1, q2 = rot(q, h)
        q_ref[:, h * dk:h * dk + half] = q1.astype(BF16)
        q_ref[:, h * dk + half:(h + 1) * dk] = q2.astype(BF16)
        k1, k2 = rot(k, h)
        k_ref[:, h * dk:h * dk + half] = k1 * kscale
        k_ref[:, h * dk + half:(h + 1) * dk] = k2 * kscale
    v_ref[...] = jnp.dot(xn, wb_ref[:, 2 * d:4 * d], preferred_element_type=F32).astype(BF16)
    gate_ref[...] = jnp.dot(xn, wb_ref[:, 4 * d:6 * d], preferred_element_type=F32)


def _ret_inproj(h, g, w_all, layer, cos, sin, *, seq, tm=256):
    t, d = h.shape
    tm = min(tm, seq)
    half = d // RET_HEADS // 2
    kern = functools.partial(_ret_inproj_kernel, layer=layer, kscale=float((d // RET_HEADS) ** -0.5))
    row = lambda i: (i, 0)
    const = lambda i: (0, 0)
    pos = lambda i: (i % (seq // tm), 0)
    return pl.pallas_call(
        kern,
        grid=(t // tm,),
        in_specs=[pl.BlockSpec((tm, d), row), pl.BlockSpec((1, d), const), pl.BlockSpec(memory_space=pl.ANY),
                  pl.BlockSpec((tm, half), pos), pl.BlockSpec((tm, half), pos)],
        out_specs=[pl.BlockSpec((tm, d), row), pl.BlockSpec((tm, d), row),
                   pl.BlockSpec((tm, 2 * d), row), pl.BlockSpec((tm, 2 * d), row)],
        out_shape=[jax.ShapeDtypeStruct((t, d), BF16), jax.ShapeDtypeStruct((t, d), F32),
                   jax.ShapeDtypeStruct((t, 2 * d), BF16), jax.ShapeDtypeStruct((t, 2 * d), F32)],
        scratch_shapes=_weight_scratch(d, 6 * d),
        compiler_params=_cparams("arbitrary"),
        name="ret_inproj",
    )(h, g, w_all, cos, sin)


def _retention_kernel(q_ref, k_ref, v_ref, gate_ref, gain_ref, intra_ref, qd_ref, kd_ref, cd_ref,
                      y_ref, r_ref, *, chunk):
    seq = q_ref.shape[0]
    r_ref[...] = jnp.zeros_like(r_ref)

    def body(c, carry):
        off = pl.multiple_of(c * chunk, chunk)
        qn = q_ref[pl.ds(off, chunk), :]
        k32 = k_ref[pl.ds(off, chunk), :]
        vn = v_ref[pl.ds(off, chunk), :]
        s = lax.dot_general(qn, k32.astype(BF16), _CONTRACT_LAST, preferred_element_type=F32) * intra_ref[...]
        o = jnp.dot(s.astype(BF16), vn, preferred_element_type=F32)
        r = r_ref[...]
        o = o + jnp.dot(qn, r.astype(BF16), preferred_element_type=F32) * qd_ref[...]
        kdec = (k32 * kd_ref[...]).astype(BF16)
        r_ref[...] = r * cd_ref[...] + lax.dot_general(kdec, vn, _CONTRACT_FIRST, preferred_element_type=F32)
        mu = jnp.mean(o, axis=-1, keepdims=True)
        dlt = o - mu
        var = jnp.mean(dlt * dlt, axis=-1, keepdims=True)
        on = dlt * lax.rsqrt(var + EPS) * gain_ref[...]
        g = gate_ref[pl.ds(off, chunk), :]
        y_ref[pl.ds(off, chunk), :] = ((g * _sigmoid(g)) * on).astype(BF16)
        return carry

    lax.fori_loop(0, seq // chunk, body, 0)


def _retention(q, k, v, gate, gain, intra, qd, kd, cd, *, batch, seq):
    t, d = q.shape
    dk = d // RET_HEADS
    dv = v.shape[1] // RET_HEADS
    c = RET_CHUNK
    kern = functools.partial(_retention_kernel, chunk=c)
    blk = lambda b, h: (b, h)
    hd = lambda b, h: (h, 0, 0)
    return pl.pallas_call(
        kern,
        grid=(batch, RET_HEADS),
        in_specs=[pl.BlockSpec((seq, dk), blk), pl.BlockSpec((seq, dk), blk), pl.BlockSpec((seq, dv), blk),
                  pl.BlockSpec((seq, dv), blk), pl.BlockSpec((1, dv), lambda b, h: (0, h)),
                  pl.BlockSpec((None, c, c), hd), pl.BlockSpec((None, c, 1), hd),
                  pl.BlockSpec((None, c, 1), hd), pl.BlockSpec((None, 1, 1), hd)],
        out_specs=pl.BlockSpec((seq, dv), blk),
        out_shape=jax.ShapeDtypeStruct((t, v.shape[1]), BF16),
        scratch_shapes=[pltpu.VMEM((dk, dv), F32)],
        compiler_params=_cparams("arbitrary", "arbitrary"),
        name="retention",
    )(q, k, v, gate, gain, intra, qd, kd, cd)


def _outproj_router_kernel(a_ref, w_hbm, h_ref, g_ref, wrh_ref, wrl_ref, br_ref,
                           h1_ref, hn_ref, route_ref, meta_ref, cnt_ref, wb_ref, stage_ref, sem, *, layer):
    i = pl.program_id(0)
    tm = a_ref.shape[0]
    d = h_ref.shape[1]

    @pl.when(i == 0)
    def _():
        cnt_ref[...] = jnp.zeros_like(cnt_ref)
        _stage_weight(w_hbm.at[layer], wb_ref, stage_ref, sem, d)

    h1 = h_ref[...] + jnp.dot(a_ref[...], wb_ref[...], preferred_element_type=F32)
    h1_ref[...] = h1
    hn = _rms(h1, g_ref[...], EPS)
    hn_ref[...] = hn

    hh = hn.astype(BF16)
    hl = (hn - hh.astype(F32)).astype(BF16)
    lg = (jnp.dot(hh, wrh_ref[...], preferred_element_type=F32)
          + jnp.dot(hh, wrl_ref[...], preferred_element_type=F32)
          + jnp.dot(hl, wrh_ref[...], preferred_element_type=F32)) + br_ref[...]

    lane = lax.broadcasted_iota(I32, (tm, LANES), 1)
    lanef = lane.astype(F32)
    big = float(LANES)

    def softmax_masked(mask):
        mx = jnp.max(jnp.where(mask, lg, -jnp.inf), axis=1, keepdims=True)
        e = jnp.where(mask, jnp.exp(lg - mx), 0.0)
        return e / jnp.sum(e, axis=1, keepdims=True)

    def top1(p, mask):
        best = jnp.max(jnp.where(mask, p, -1.0), axis=1, keepdims=True)
        idx = jnp.min(jnp.where(mask & (p == best), lanef, big), axis=1, keepdims=True)
        return best, idx

    gmask = lane < N_GROUPS
    pg, gi = top1(softmax_masked(gmask), gmask)
    lo = ROUTE_LANE0 + gi * EXPERTS_PER_GROUP
    emask = (lanef >= lo) & (lanef < lo + EXPERTS_PER_GROUP)
    pe = softmax_masked(emask)
    p1, i1 = top1(pe, emask)
    p2, i2 = top1(pe, emask & (lanef != i1))
    den = p1 + p2
    g1 = pg * p1 / den
    g2 = pg * p2 / den

    oh1 = lanef == i1
    oh2 = lanef == i2
    both = jnp.where(oh1 | oh2, 1.0, 0.0)
    row = lax.broadcasted_iota(I32, (tm, tm), 0)
    col = lax.broadcasted_iota(I32, (tm, tm), 1)
    strict = jnp.where(col < row, 1.0, 0.0).astype(BF16)
    before = jnp.dot(strict, both.astype(BF16), preferred_element_type=F32) + cnt_ref[0:1, :]
    r1 = jnp.sum(jnp.where(oh1, before, 0.0), axis=1, keepdims=True)
    r2 = jnp.sum(jnp.where(oh2, before, 0.0), axis=1, keepdims=True)
    cnt_ref[0:1, :] = cnt_ref[0:1, :] + jnp.sum(both, axis=0, keepdims=True)

    e1 = i1 - ROUTE_LANE0
    e2 = i2 - ROUTE_LANE0
    route = jnp.where(lane == 0, e1, jnp.where(lane == 1, e2, jnp.where(
        lane == 2, r1, jnp.where(lane == 3, r2, jnp.where(lane == 4, g1, jnp.where(lane == 5, g2, 0.0))))))
    route_ref[...] = route
    meta_ref[...] = jnp.transpose(route)[0:SUBLANES, :].astype(I32)


def _outproj_router(a, w_all, layer, h, g, wrh, wrl, br, *, tm):
    t, kdim = a.shape
    d = h.shape[1]
    row = lambda i: (i, 0)
    const = lambda i: (0, 0)
    kern = functools.partial(_outproj_router_kernel, layer=layer)
    return pl.pallas_call(
        kern,
        grid=(t // tm,),
        in_specs=[pl.BlockSpec((tm, kdim), row), pl.BlockSpec(memory_space=pl.ANY), pl.BlockSpec((tm, d), row),
                  pl.BlockSpec((1, d), const), pl.BlockSpec((d, LANES), const), pl.BlockSpec((d, LANES), const),
                  pl.BlockSpec((1, LANES), const)],
        out_specs=[pl.BlockSpec((tm, d), row), pl.BlockSpec((tm, d), row), pl.BlockSpec((tm, LANES), row),
                   pl.BlockSpec((None, SUBLANES, tm), lambda i: (i, 0, 0)),
                   pl.BlockSpec((SUBLANES, LANES), const)],
        out_shape=[jax.ShapeDtypeStruct((t, d), F32), jax.ShapeDtypeStruct((t, d), F32),
                   jax.ShapeDtypeStruct((t, LANES), F32), jax.ShapeDtypeStruct((t // tm, SUBLANES, tm), I32),
                   jax.ShapeDtypeStruct((SUBLANES, LANES), F32)],
        scratch_shapes=_weight_scratch(kdim, d),
        compiler_params=_cparams("arbitrary"),
        name="outproj_router",
    )(a, w_all, h, g, wrh, wrl, br)


def _pad_copies(counts_ref, pstart_ref, zero_ref, xs_ref, sem, *, bm):
    out = []
    for e in range(N_EXPERTS):
        cnt = counts_ref[e]
        npad = (bm - (cnt & (bm - 1))) & (bm - 1)
        pos = pstart_ref[e] + cnt
        for r in range(SUBLANES - 1):
            out.append(((npad & (SUBLANES - 1)) > r,
                        pltpu.make_async_copy(zero_ref.at[pl.ds(0, 1), :], xs_ref.at[pl.ds(pos + r, 1), :], sem)))
        pos = pos + (npad & (SUBLANES - 1))
        bit = SUBLANES
        while bit < bm:
            dst = xs_ref.at[pl.ds(pl.multiple_of(pos, SUBLANES), bit), :]
            out.append(((npad & bit) != 0, pltpu.make_async_copy(zero_ref.at[pl.ds(0, bit), :], dst, sem)))
            pos = pos + (npad & bit)
            bit *= 2
    return out


def _dispatch_kernel(counts_ref, pstart_ref, nused_ref, meta_ref, hn_ref, xs_ref, zero_ref, sem, psem, *, bm):
    tm = hn_ref.shape[0]
    half = bm // 2

    @pl.when(pl.program_id(0) == 0)
    def _():
        zero_ref[...] = jnp.zeros_like(zero_ref)
        pads = _pad_copies(counts_ref, pstart_ref, zero_ref, xs_ref, psem, bm=bm)
        for pred, cp in pads:
            pl.when(pred)(cp.start)

        def tail_copy(j):
            return pltpu.make_async_copy(zero_ref, xs_ref.at[pl.ds(pl.multiple_of(j * half, half), half), :], psem)

        first = nused_ref[0] * 2
        last = xs_ref.shape[0] // half
        lax.fori_loop(first, last, lambda j, c: (tail_copy(j).start(), c)[1], 0)
        for pred, cp in pads:
            pl.when(pred)(cp.wait)
        lax.fori_loop(first, last, lambda j, c: (tail_copy(j).wait(), c)[1], 0)

    def row_copy(t, k):
        dst = pstart_ref[meta_ref[k, t]] + meta_ref[TOP_K + k, t]
        return pltpu.make_async_copy(hn_ref.at[pl.ds(t, 1), :], xs_ref.at[pl.ds(dst, 1), :], sem)

    def issue(t, carry):
        for k in range(TOP_K):
            row_copy(t, k).start(priority=k)
        return carry

    lax.fori_loop(0, tm, issue, 0, unroll=8)

    def drain(t, carry):
        for k in range(TOP_K):
            row_copy(t, k).wait()
        return carry

    lax.fori_loop(0, tm, drain, 0, unroll=8)


def _dispatch(counts, pstart, n_used, meta, hn, n_rows, *, bm):
    t, d = hn.shape
    tm = meta.shape[2]
    kern = functools.partial(_dispatch_kernel, bm=bm)
    grid_spec = pltpu.PrefetchScalarGridSpec(
        num_scalar_prefetch=3,
        grid=(t // tm,),
        in_specs=[pl.BlockSpec((None, SUBLANES, tm), lambda i, *_: (i, 0, 0), memory_space=pltpu.SMEM),
                  pl.BlockSpec((tm, d), lambda i, *_: (i, 0))],
        out_specs=pl.BlockSpec(memory_space=pl.ANY),
        scratch_shapes=[pltpu.VMEM((bm // 2, d), F32), pltpu.SemaphoreType.DMA(()), pltpu.SemaphoreType.DMA(())],
    )
    return pl.pallas_call(
        kern,
        grid_spec=grid_spec,
        out_shape=jax.ShapeDtypeStruct((n_rows, d), F32),
        compiler_params=_cparams("arbitrary"),
        name="moe_dispatch",
    )(counts, pstart, n_used, meta, hn)


def _experts_kernel(be_ref, nu_ref, x_ref, wgu_ref, wd_ref, y_ref, wgu_b, wd_b):
    i = pl.program_id(0)
    prev = be_ref[jnp.maximum(i - 1, 0)]

    @pl.when((i < nu_ref[0]) & ((i == 0) | (be_ref[i] != prev)))
    def _():
        wgu_b[...] = wgu_ref[...].astype(BF16)
        wd_b[...] = wd_ref[...].astype(BF16)

    @pl.when(i < nu_ref[0])
    def _():
        x = x_ref[...].astype(BF16)
        hcat = jnp.dot(x, wgu_b[...], preferred_element_type=F32)
        ff = hcat.shape[1] // 2
        a = hcat[:, :ff]
        b = hcat[:, ff:]
        act = ((a * _sigmoid(a)) * b).astype(BF16)
        y_ref[...] = jnp.dot(act, wd_b[...], preferred_element_type=F32)

    @pl.when(i >= nu_ref[0])
    def _():
        y_ref[...] = jnp.zeros_like(y_ref)


def _experts(block_e, n_used, xs, wgu_all, wd_all, layer, *, bm):
    n_rows, d = xs.shape
    ff2 = wgu_all.shape[3]
    grid_spec = pltpu.PrefetchScalarGridSpec(
        num_scalar_prefetch=2,
        grid=(n_rows // bm,),
        in_specs=[pl.BlockSpec((bm, d), lambda i, be, nu: (i, 0)),
                  pl.BlockSpec((None, None, d, ff2), lambda i, be, nu: (layer, be[i], 0, 0)),
                  pl.BlockSpec((None, None, ff2 // 2, d), lambda i, be, nu: (layer, be[i], 0, 0))],
        out_specs=pl.BlockSpec((bm, d), lambda i, be, nu: (i, 0)),
        scratch_shapes=[pltpu.VMEM((d, ff2), BF16), pltpu.VMEM((ff2 // 2, d), BF16)],
    )
    return pl.pallas_call(
        _experts_kernel,
        grid_spec=grid_spec,
        out_shape=jax.ShapeDtypeStruct((n_rows, d), F32),
        compiler_params=_cparams("arbitrary"),
        name="moe_experts",
    )(block_e, n_used, xs, wgu_all, wd_all)


def _combine_kernel(pstart_ref, meta_ref, h1_ref, route_ref, gfin_ref, ys_ref, o_ref, buf_ref, sem,
                    *, final_norm):
    tm = h1_ref.shape[0]

    def row_copy(t, k):
        src = pstart_ref[meta_ref[k, t]] + meta_ref[TOP_K + k, t]
        return pltpu.make_async_copy(ys_ref.at[pl.ds(src, 1), :], buf_ref.at[k, pl.ds(t, 1), :], sem)

    def issue(t, carry):
        for k in range(TOP_K):
            row_copy(t, k).start(priority=k)
        return carry

    lax.fori_loop(0, tm, issue, 0, unroll=8)

    def drain(t, carry):
        for k in range(TOP_K):
            row_copy(t, k).wait()
        return carry

    lax.fori_loop(0, tm, drain, 0, unroll=8)

    g0 = route_ref[:, 4:5]
    g1 = route_ref[:, 5:6]
    out = h1_ref[...] + (buf_ref[0] * g0 + buf_ref[1] * g1)
    if final_norm:
        out = _rms(out, gfin_ref[...], EPS)
    o_ref[...] = out


def _combine(pstart, meta, h1, route, gfin, ys, *, final_norm):
    t, d = h1.shape
    tm = meta.shape[2]
    kern = functools.partial(_combine_kernel, final_norm=final_norm)
    grid_spec = pltpu.PrefetchScalarGridSpec(
        num_scalar_prefetch=1,
        grid=(t // tm,),
        in_specs=[pl.BlockSpec((None, SUBLANES, tm), lambda i, p: (i, 0, 0), memory_space=pltpu.SMEM),
                  pl.BlockSpec((tm, d), lambda i, p: (i, 0)),
                  pl.BlockSpec((tm, LANES), lambda i, p: (i, 0)),
                  pl.BlockSpec((1, d), lambda i, p: (0, 0)),
                  pl.BlockSpec(memory_space=pl.ANY)],
        out_specs=pl.BlockSpec((tm, d), lambda i, p: (i, 0)),
        scratch_shapes=[pltpu.VMEM((TOP_K, tm, d), F32), pltpu.SemaphoreType.DMA(())],
    )
    return pl.pallas_call(
        kern,
        grid_spec=grid_spec,
        out_shape=jax.ShapeDtypeStruct((t, d), F32),
        compiler_params=_cparams("arbitrary"),
        name="moe_combine",
    )(pstart, meta, h1, route, gfin, ys)


def _moe(a, w_out_all, mixer_layer, h, layer, g_ffn, wr_g, br_g, wr_e, br_e, w_gu_all, w_down_all, gfin,
         *, final_norm):
    t, d = h.shape
    bm = MOE_BLOCK
    tm = min(ROW_TILE, t)
    wr = jnp.concatenate([wr_g, wr_e, jnp.zeros((d, LANES - N_GROUPS - N_EXPERTS), F32)], axis=1)
    wrh = wr.astype(BF16)
    wrl = (wr - wrh.astype(F32)).astype(BF16)
    br = jnp.concatenate([br_g, br_e, jnp.zeros((LANES - N_GROUPS - N_EXPERTS,), F32)])[None, :]
    h1, hn, route, meta, cnt = _outproj_router(a, w_out_all, mixer_layer, h, g_ffn[None, :], wrh, wrl, br, tm=tm)

    counts = cnt[0, ROUTE_LANE0:ROUTE_LANE0 + N_EXPERTS].astype(I32)
    nb = (counts + bm - 1) // bm
    nb_end = jnp.cumsum(nb)
    pstart = ((nb_end - nb) * bm).astype(I32)
    n_blocks = (t * TOP_K + N_EXPERTS * (bm - 1)) // bm
    j = jnp.arange(n_blocks, dtype=I32)
    block_e = jnp.minimum(jnp.sum((nb_end[None, :] <= j[:, None]).astype(I32), axis=1), N_EXPERTS - 1)
    n_used = nb_end[-1:].astype(I32)

    xs = _dispatch(counts, pstart, n_used, meta, hn, n_blocks * bm, bm=bm)
    ys = _experts(block_e, n_used, xs, w_gu_all, w_down_all, layer, bm=bm)
    return _combine(pstart, meta, h1, route, gfin[None, :], ys, final_norm=final_norm)


def _fox_mixer(h, g, w_in_all, layer, b_f, *, batch, seq):
    d = h.shape[1]
    nh = FOX_HEADS
    wf = w_in_all[layer, :, 4 * d:4 * d + nh]
    wf = jnp.concatenate([wf] * 6 + [jnp.zeros((d, LANES - 6 * nh), F32)], axis=1).astype(BF16)
    bfr = jnp.concatenate([b_f] * 6 + [jnp.zeros((LANES - 6 * nh,), F32)])[None, :]
    q, kb, v, gate, a = _fox_inproj(h, g[None, :], w_in_all, layer, wf, bfr, seq=seq)
    return _fox_attn(q, a, kb, v, gate, batch=batch, seq=seq)


def _ret_mixer(h, g, w_in_all, layer, gn_gain, *, batch, seq):
    d = h.shape[1]
    nh = RET_HEADS
    dk = d // nh
    c = RET_CHUNK
    inv = 1.0 / (ROPE_BASE ** jnp.linspace(0.0, 1.0, dk // 2, dtype=F32))
    ang = jnp.arange(seq).astype(F32)[:, None] * inv[None, :]
    cos, sin = jnp.cos(ang), jnp.sin(ang)
    log_g = jnp.log(1.0 - 2.0 ** (-5.0 - jnp.arange(nh, dtype=F32)))
    idx = jnp.arange(c, dtype=F32)
    rel = idx[:, None] - idx[None, :]
    intra = jnp.where(rel >= 0, jnp.exp(log_g[:, None, None] * jnp.maximum(rel, 0.0)), 0.0)
    qd = jnp.exp(log_g[:, None] * (idx + 1.0))[:, :, None]
    kd = jnp.exp(log_g[:, None] * (c - 1.0 - idx))[:, :, None]
    cd = jnp.exp(log_g * c)[:, None, None]
    q, k, v, gate = _ret_inproj(h, g[None, :], w_in_all, layer, cos, sin, seq=seq)
    return _retention(q, k, v, gate, gn_gain[None, :], intra, qd, kd, cd, batch=batch, seq=seq)


def kernel(x, fox_w_in, fox_b_f, fox_w_out, ret_w_in, ret_gn_gain, ret_w_out, norm_mix, norm_ffn,
           router_group_w, router_group_b, router_expert_w, router_expert_b, expert_w_gu, expert_w_down,
           norm_final):
    batch, seq, d = x.shape
    depth = norm_mix.shape[0]
    h = x.reshape(batch * seq, d)
    for i in range(depth):
        j = i // 2
        if i % 2 == 0:
            a = _fox_mixer(h, norm_mix[i], fox_w_in, j, fox_b_f[j], batch=batch, seq=seq)
            w_out_all = fox_w_out
        else:
            a = _ret_mixer(h, norm_mix[i], ret_w_in, j, ret_gn_gain[j], batch=batch, seq=seq)
            w_out_all = ret_w_out
        h = _moe(a, w_out_all, j, h, i, norm_ffn[i], router_group_w[i], router_group_b[i], router_expert_w[i],
                 router_expert_b[i], expert_w_gu, expert_w_down, norm_final, final_norm=(i == depth - 1))
    return h.reshape(batch, seq, d)
```

```python
import functools

import jax
import jax.numpy as jnp
from jax import lax
from jax.experimental import pallas as pl
from jax.experimental.pallas import tpu as pltpu

F32 = jnp.float32
BF16 = jnp.bfloat16
I32 = jnp.int32

FOX_HEADS = 16
RET_HEADS = 4
RET_CHUNK = 128
N_GROUPS = 4
EXPERTS_PER_GROUP = 8
N_EXPERTS = N_GROUPS * EXPERTS_PER_GROUP
TOP_K = 2
EPS = 1e-6
NEG_INF = -1e30
ROPE_BASE = 10000.0

LANES = 128
SUBLANES = 8
ROUTE_LANE0 = N_GROUPS
VMEM_LIMIT = 56 * 1024 * 1024
MOE_BLOCK = 256
ROW_TILE = 512
WEIGHT_CHUNK = 512

_CONTRACT_LAST = (((1,), (1,)), ((), ()))
_CONTRACT_FIRST = (((0,), (0,)), ((), ()))


def _cparams(*sem):
    return pltpu.CompilerParams(dimension_semantics=sem, vmem_limit_bytes=VMEM_LIMIT)


def _sigmoid(x):
    return 1.0 / (1.0 + jnp.exp(-x))


def _rms(x, g, eps):
    ms = jnp.mean(x * x, axis=-1, keepdims=True)
    return x * lax.rsqrt(ms + eps) * g


def _split3(x):
    hi = x.astype(BF16)
    r = x - hi.astype(F32)
    mid = r.astype(BF16)
    lo = (r - mid.astype(F32)).astype(BF16)
    return hi, mid, lo


def _stage_weight(w_hbm, wb_ref, stage_ref, sem, ncols):
    chunk = stage_ref.shape[2]
    n = ncols // chunk

    def cp(c):
        return pltpu.make_async_copy(w_hbm.at[:, pl.ds(c * chunk, chunk)], stage_ref.at[c % 2], sem.at[c % 2])

    cp(0).start()
    for c in range(n):
        if c + 1 < n:
            cp(c + 1).start()
        cp(c).wait()
        wb_ref[:, c * chunk:(c + 1) * chunk] = stage_ref[c % 2].astype(BF16)


def _rows_to_slabs(ref, x):
    n, d = x.shape
    nslab = d // LANES
    for c in range(nslab):
        ref[pl.ds(c, n, stride=nslab), :] = x[:, c * LANES:(c + 1) * LANES]


def _slabs_to_rows(ref, n, nslab):
    return jnp.concatenate([ref[pl.ds(c, n, stride=nslab), :] for c in range(nslab)], axis=1)


def _slab_rows(ref, row, nrows, nslab):
    return ref.at[pl.ds(pl.multiple_of(row * nslab, nslab), nrows * nslab), :]


def _weight_scratch(kdim, ncols):
    return [pltpu.VMEM((kdim, ncols), BF16), pltpu.VMEM((2, kdim, WEIGHT_CHUNK), F32),
            pltpu.SemaphoreType.DMA((2,))]


def _fox_inproj_kernel(h_ref, g_ref, w_hbm, wf_ref, bf_ref, q_ref, kb_ref, v_ref, gate_ref, a_ref,
                       carry_ref, wb_ref, stage_ref, sem, *, layer, tiles_per_seq, scale):
    i = pl.program_id(0)
    tm, d = h_ref.shape

    @pl.when(i == 0)
    def _():
        _stage_weight(w_hbm.at[layer], wb_ref, stage_ref, sem, 4 * d)

    xn = _rms(h_ref[...], g_ref[...], EPS).astype(BF16)

    q = jnp.dot(xn, wb_ref[:, 0:d], preferred_element_type=F32)
    q_ref[...] = (q * scale).astype(BF16)
    v_ref[...] = jnp.dot(xn, wb_ref[:, 2 * d:3 * d], preferred_element_type=F32).astype(BF16)
    gate_ref[...] = jnp.dot(xn, wb_ref[:, 3 * d:4 * d], preferred_element_type=F32)

    f = jnp.dot(xn, wf_ref[...], preferred_element_type=F32) + bf_ref[...]
    ls = jnp.minimum(f, 0.0) - jnp.log1p(jnp.exp(-jnp.abs(f)))

    @pl.when(i % tiles_per_seq == 0)
    def _():
        carry_ref[...] = jnp.zeros_like(carry_ref)

    row = lax.broadcasted_iota(I32, (tm, tm), 0)
    col = lax.broadcasted_iota(I32, (tm, tm), 1)
    tri = jnp.where(col <= row, 1.0, 0.0).astype(BF16)
    hi, mid, lo = _split3(ls)
    cum = (jnp.dot(tri, hi, preferred_element_type=F32) + jnp.dot(tri, mid, preferred_element_type=F32)
           + jnp.dot(tri, lo, preferred_element_type=F32)) + carry_ref[...]
    carry_ref[...] = cum[tm - 1:tm, :]

    c_hi, c_mid, c_lo = (c.astype(F32) for c in _split3(cum))
    lane = lax.broadcasted_iota(I32, (tm, LANES), 1)
    a_ref[...] = jnp.where(lane < 16, c_hi, jnp.where(lane < 32, c_mid, jnp.where(
        lane < 48, c_lo, jnp.where(lane < 96, 1.0, 0.0)))).astype(BF16)
    bk = jnp.where(lane < 48, 1.0, jnp.where(lane < 64, -c_hi, jnp.where(
        lane < 80, -c_mid, jnp.where(lane < 96, -c_lo, 0.0)))).astype(BF16)

    k = jnp.dot(xn, wb_ref[:, d:2 * d], preferred_element_type=F32).astype(BF16)
    for p in range(d // LANES):
        kb_ref[:, 2 * p * LANES:(2 * p + 1) * LANES] = k[:, p * LANES:(p + 1) * LANES]
        kb_ref[:, (2 * p + 1) * LANES:(2 * p + 2) * LANES] = bk


def _fox_inproj(h, g, w_all, layer, wf, bfr, *, seq):
    t, d = h.shape
    tm = min(ROW_TILE, seq)
    scale = float((d // FOX_HEADS) ** -0.5)
    kern = functools.partial(_fox_inproj_kernel, layer=layer, tiles_per_seq=seq // tm, scale=scale)
    row = lambda i: (i, 0)
    const = lambda i: (0, 0)
    return pl.pallas_call(
        kern,
        grid=(t // tm,),
        in_specs=[pl.BlockSpec((tm, d), row), pl.BlockSpec((1, d), const),
                  pl.BlockSpec(memory_space=pl.ANY), pl.BlockSpec((d, LANES), const),
                  pl.BlockSpec((1, LANES), const)],
        out_specs=[pl.BlockSpec((tm, d), row), pl.BlockSpec((tm, 2 * d), row),
                   pl.BlockSpec((tm, d), row), pl.BlockSpec((tm, d), row),
                   pl.BlockSpec((tm, LANES), row)],
        out_shape=[jax.ShapeDtypeStruct((t, d), BF16), jax.ShapeDtypeStruct((t, 2 * d), BF16),
                   jax.ShapeDtypeStruct((t, d), BF16), jax.ShapeDtypeStruct((t, d), F32),
                   jax.ShapeDtypeStruct((t, LANES), BF16)],
        scratch_shapes=[pltpu.VMEM((1, LANES), F32)] + _weight_scratch(d, 4 * d),
        compiler_params=_cparams("arbitrary"),
        name="fox_inproj",
    )(h, g, w_all, wf, bfr)


def _fox_attn_kernel(q_ref, a_ref, kb_ref, v_ref, gate_ref, o_ref, *, tq, dh, depth):
    seq = q_ref.shape[0]
    pair = pl.program_id(1)
    lane = lax.broadcasted_iota(I32, (tq, LANES), 1)
    krow = lax.broadcasted_iota(I32, (tq, tq), 0)
    qcol = lax.broadcasted_iota(I32, (tq, tq), 1)
    causal = krow <= qcol
    heads_per_block = LANES // dh
    qmask = [jnp.where((lane >= hh * dh) & (lane < (hh + 1) * dh), 1.0, 0.0).astype(BF16)
             for hh in range(heads_per_block)]
    amask = [jnp.where((lane < 96) & ((lane & 15) == pair * heads_per_block + hh), 1.0, 0.0).astype(BF16)
             for hh in range(heads_per_block)]
    vt = jnp.transpose(v_ref[...])
    chains = [(i, hh) for i in range(seq // tq) for hh in range(heads_per_block)]

    def scores(i, hh):
        off = i * tq
        qa = jnp.concatenate([q_ref[off:off + tq, :] * qmask[hh], a_ref[off:off + tq, :] * amask[hh]], axis=1)
        return lax.dot_general(kb_ref[0:off + tq, :], qa, _CONTRACT_LAST, preferred_element_type=F32)

    def finish(i, hh, st):
        off = i * tq
        sd = jnp.where(causal, st[off:off + tq, :], NEG_INF)
        st = jnp.concatenate([st[0:off, :], sd], axis=0) if off > 0 else sd
        m = jnp.max(st, axis=0, keepdims=True)
        e = jnp.exp(st - m)
        p = (e * (1.0 / jnp.sum(e, axis=0, keepdims=True))).astype(BF16)
        return jnp.dot(vt[hh * dh:(hh + 1) * dh, 0:off + tq], p, preferred_element_type=F32)

    outs = {}
    queue = [scores(*c) for c in chains[:depth]]
    for n, (i, hh) in enumerate(chains):
        cur = queue.pop(0)
        if n + depth < len(chains):
            queue.append(scores(*chains[n + depth]))
        outs[hh] = finish(i, hh, cur)
        if hh == heads_per_block - 1:
            off = i * tq
            o = jnp.transpose(jnp.concatenate([outs[h2] for h2 in range(heads_per_block)], axis=0))
            o_ref[off:off + tq, :] = (o * _sigmoid(gate_ref[off:off + tq, :])).astype(BF16)


def _fox_attn(q, a, kb, v, gate, *, batch, seq, tq=256, depth=3):
    t, d = q.shape
    dh = d // FOX_HEADS
    pairs = d // LANES
    kern = functools.partial(_fox_attn_kernel, tq=tq, dh=dh, depth=depth)
    blk = lambda b, p: (b, p)
    return pl.pallas_call(
        kern,
        grid=(batch, pairs),
        in_specs=[pl.BlockSpec((seq, LANES), blk), pl.BlockSpec((seq, LANES), lambda b, p: (b, 0)),
                  pl.BlockSpec((seq, 2 * LANES), blk), pl.BlockSpec((seq, LANES), blk),
                  pl.BlockSpec((seq, LANES), blk)],
        out_specs=pl.BlockSpec((seq, LANES), blk),
        out_shape=jax.ShapeDtypeStruct((t, d), BF16),
        compiler_params=_cparams("arbitrary", "arbitrary"),
        name="fox_attn",
    )(q, a, kb, v, gate)


def _ret_inproj_kernel(h_ref, g_ref, w_hbm, cos_ref, sin_ref, q_ref, k_ref, v_ref, gate_ref,
                       wb_ref, stage_ref, sem, *, layer, kscale):
    tm, d = h_ref.shape

    @pl.when(pl.program_id(0) == 0)
    def _():
        _stage_weight(w_hbm.at[layer], wb_ref, stage_ref, sem, 6 * d)

    xn = _rms(h_ref[...], g_ref[...], EPS).astype(BF16)
    cos = cos_ref[...]
    sin = sin_ref[...]
    dk = d // RET_HEADS
    half = dk // 2

    def rot(t, h):
        t1 = t[:, h * dk:h * dk + half]
        t2 = t[:, h * dk + half:(h + 1) * dk]
        return t1 * cos - t2 * sin, t1 * sin + t2 * cos

    q = jnp.dot(xn, wb_ref[:, 0:d], preferred_element_type=F32)
    k = jnp.dot(xn, wb_ref[:, d:2 * d], preferred_element_type=F32)
    for h in range(RET_HEADS):
        q1, q2 = rot(q, h)
        q_ref[:, h * dk:h * dk + half] = q1.astype(BF16)
        q_ref[:, h * dk + half:(h + 1) * dk] = q2.astype(BF16)
        k1, k2 = rot(k, h)
        k_ref[:, h * dk:h * dk + half] = k1 * kscale
        k_ref[:, h * dk + half:(h + 1) * dk] = k2 * kscale
    v_ref[...] = jnp.dot(xn, wb_ref[:, 2 * d:4 * d], preferred_element_type=F32).astype(BF16)
    gate_ref[...] = jnp.dot(xn, wb_ref[:, 4 * d:6 * d], preferred_element_type=F32)


def _ret_inproj(h, g, w_all, layer, cos, sin, *, seq, tm=256):
    t, d = h.shape
    tm = min(tm, seq)
    half = d // RET_HEADS // 2
    kern = functools.partial(_ret_inproj_kernel, layer=layer, kscale=float((d // RET_HEADS) ** -0.5))
    row = lambda i: (i, 0)
    const = lambda i: (0, 0)
    pos = lambda i: (i % (seq // tm), 0)
    return pl.pallas_call(
        kern,
        grid=(t // tm,),
        in_specs=[pl.BlockSpec((tm, d), row), pl.BlockSpec((1, d), const), pl.BlockSpec(memory_space=pl.ANY),
                  pl.BlockSpec((tm, half), pos), pl.BlockSpec((tm, half), pos)],
        out_specs=[pl.BlockSpec((tm, d), row), pl.BlockSpec((tm, d), row),
                   pl.BlockSpec((tm, 2 * d), row), pl.BlockSpec((tm, 2 * d), row)],
        out_shape=[jax.ShapeDtypeStruct((t, d), BF16), jax.ShapeDtypeStruct((t, d), F32),
                   jax.ShapeDtypeStruct((t, 2 * d), BF16), jax.ShapeDtypeStruct((t, 2 * d), F32)],
        scratch_shapes=_weight_scratch(d, 6 * d),
        compiler_params=_cparams("arbitrary"),
        name="ret_inproj",
    )(h, g, w_all, cos, sin)


def _retention_kernel(q_ref, k_ref, v_ref, gate_ref, gain_ref, intra_ref, qd_ref, kd_ref, cd_ref,
                      y_ref, r_ref, *, chunk):
    seq = q_ref.shape[0]
    r_ref[...] = jnp.zeros_like(r_ref)

    def body(c):
        off = c * chunk
        qn = q_ref[pl.ds(off, chunk), :]
        k32 = k_ref[pl.ds(off, chunk), :]
        vn = v_ref[pl.ds(off, chunk), :]
        s = lax.dot_general(qn, k32.astype(BF16), _CONTRACT_LAST, preferred_element_type=F32) * intra_ref[...]
        o = jnp.dot(s.astype(BF16), vn, preferred_element_type=F32)
        r = r_ref[...]
        o = o + jnp.dot(qn, r.astype(BF16), preferred_element_type=F32) * qd_ref[...]
        kdec = (k32 * kd_ref[...]).astype(BF16)
        r_ref[...] = r * cd_ref[...] + lax.dot_general(kdec, vn, _CONTRACT_FIRST, preferred_element_type=F32)
        mu = jnp.mean(o, axis=-1, keepdims=True)
        dlt = o - mu
        var = jnp.mean(dlt * dlt, axis=-1, keepdims=True)
        on = dlt * lax.rsqrt(var + EPS) * gain_ref[...]
        g = gate_ref[pl.ds(off, chunk), :]
        y_ref[pl.ds(off, chunk), :] = ((g * _sigmoid(g)) * on).astype(BF16)

    for c in range(seq // chunk):
        body(c)


def _retention(q, k, v, gate, gain, intra, qd, kd, cd, *, batch, seq):
    t, d = q.shape
    dk = d // RET_HEADS
    dv = v.shape[1] // RET_HEADS
    c = RET_CHUNK
    kern = functools.partial(_retention_kernel, chunk=c)
    blk = lambda b, h: (b, h)
    hd = lambda b, h: (h, 0, 0)
    return pl.pallas_call(
        kern,
        grid=(batch, RET_HEADS),
        in_specs=[pl.BlockSpec((seq, dk), blk), pl.BlockSpec((seq, dk), blk), pl.BlockSpec((seq, dv), blk),
                  pl.BlockSpec((seq, dv), blk), pl.BlockSpec((1, dv), lambda b, h: (0, h)),
                  pl.BlockSpec((None, c, c), hd), pl.BlockSpec((None, c, 1), hd),
                  pl.BlockSpec((None, c, 1), hd), pl.BlockSpec((None, 1, 1), hd)],
        out_specs=pl.BlockSpec((seq, dv), blk),
        out_shape=jax.ShapeDtypeStruct((t, v.shape[1]), BF16),
        scratch_shapes=[pltpu.VMEM((dk, dv), F32)],
        compiler_params=_cparams("arbitrary", "arbitrary"),
        name="retention",
    )(q, k, v, gate, gain, intra, qd, kd, cd)


def _outproj_router_kernel(a_ref, w_hbm, h_ref, g_ref, wrh_ref, wrl_ref, br_ref,
                           h1_ref, hn_ref, route_ref, meta_ref, cnt_ref, wb_ref, stage_ref, sem, *, layer):
    i = pl.program_id(0)
    tm = a_ref.shape[0]
    d = h_ref.shape[1]

    @pl.when(i == 0)
    def _():
        cnt_ref[...] = jnp.zeros_like(cnt_ref)
        _stage_weight(w_hbm.at[layer], wb_ref, stage_ref, sem, d)

    h1 = h_ref[...] + jnp.dot(a_ref[...], wb_ref[...], preferred_element_type=F32)
    h1_ref[...] = h1
    hn = _rms(h1, g_ref[...], EPS)
    _rows_to_slabs(hn_ref, hn)

    hh = hn.astype(BF16)
    hl = (hn - hh.astype(F32)).astype(BF16)
    lg = (jnp.dot(hh, wrh_ref[...], preferred_element_type=F32)
          + jnp.dot(hh, wrl_ref[...], preferred_element_type=F32)
          + jnp.dot(hl, wrh_ref[...], preferred_element_type=F32)) + br_ref[...]

    lane = lax.broadcasted_iota(I32, (tm, LANES), 1)
    lanef = lane.astype(F32)
    big = float(LANES)

    def softmax_masked(mask):
        mx = jnp.max(jnp.where(mask, lg, -jnp.inf), axis=1, keepdims=True)
        e = jnp.where(mask, jnp.exp(lg - mx), 0.0)
        return e / jnp.sum(e, axis=1, keepdims=True)

    def top1(p, mask):
        best = jnp.max(jnp.where(mask, p, -1.0), axis=1, keepdims=True)
        idx = jnp.min(jnp.where(mask & (p == best), lanef, big), axis=1, keepdims=True)
        return best, idx

    gmask = lane < N_GROUPS
    pg, gi = top1(softmax_masked(gmask), gmask)
    lo = ROUTE_LANE0 + gi * EXPERTS_PER_GROUP
    emask = (lanef >= lo) & (lanef < lo + EXPERTS_PER_GROUP)
    pe = softmax_masked(emask)
    p1, i1 = top1(pe, emask)
    p2, i2 = top1(pe, emask & (lanef != i1))
    den = p1 + p2
    g1 = pg * p1 / den
    g2 = pg * p2 / den

    oh1 = lanef == i1
    oh2 = lanef == i2
    both = jnp.where(oh1 | oh2, 1.0, 0.0)
    row = lax.broadcasted_iota(I32, (tm, tm), 0)
    col = lax.broadcasted_iota(I32, (tm, tm), 1)
    strict = jnp.where(col < row, 1.0, 0.0).astype(BF16)
    before = jnp.dot(strict, both.astype(BF16), preferred_element_type=F32) + cnt_ref[0:1, :]
    r1 = jnp.sum(jnp.where(oh1, before, 0.0), axis=1, keepdims=True)
    r2 = jnp.sum(jnp.where(oh2, before, 0.0), axis=1, keepdims=True)
    cnt_ref[0:1, :] = cnt_ref[0:1, :] + jnp.sum(both, axis=0, keepdims=True)

    e1 = i1 - ROUTE_LANE0
    e2 = i2 - ROUTE_LANE0
    route = jnp.where(lane == 0, e1, jnp.where(lane == 1, e2, jnp.where(
        lane == 2, r1, jnp.where(lane == 3, r2, jnp.where(lane == 4, g1, jnp.where(lane == 5, g2, 0.0))))))
    route_ref[...] = route
    meta_ref[...] = jnp.transpose(route)[0:SUBLANES, :].astype(I32)


def _outproj_router(a, w_all, layer, h, g, wrh, wrl, br, *, tm):
    t, kdim = a.shape
    d = h.shape[1]
    row = lambda i: (i, 0)
    const = lambda i: (0, 0)
    kern = functools.partial(_outproj_router_kernel, layer=layer)
    return pl.pallas_call(
        kern,
        grid=(t // tm,),
        in_specs=[pl.BlockSpec((tm, kdim), row), pl.BlockSpec(memory_space=pl.ANY), pl.BlockSpec((tm, d), row),
                  pl.BlockSpec((1, d), const), pl.BlockSpec((d, LANES), const), pl.BlockSpec((d, LANES), const),
                  pl.BlockSpec((1, LANES), const)],
        out_specs=[pl.BlockSpec((tm, d), row), pl.BlockSpec((tm * d // LANES, LANES), row),
                   pl.BlockSpec((tm, LANES), row), pl.BlockSpec((None, SUBLANES, tm), lambda i: (i, 0, 0)),
                   pl.BlockSpec((SUBLANES, LANES), const)],
        out_shape=[jax.ShapeDtypeStruct((t, d), F32), jax.ShapeDtypeStruct((t * d // LANES, LANES), F32),
                   jax.ShapeDtypeStruct((t, LANES), F32), jax.ShapeDtypeStruct((t // tm, SUBLANES, tm), I32),
                   jax.ShapeDtypeStruct((SUBLANES, LANES), F32)],
        scratch_shapes=_weight_scratch(kdim, d),
        compiler_params=_cparams("arbitrary"),
        name="outproj_router",
    )(a, w_all, h, g, wrh, wrl, br)


def _pad_copies(counts_ref, pstart_ref, zero_ref, xs_ref, sem, *, bm):
    nslab = SUBLANES
    out = []
    for e in range(N_EXPERTS):
        cnt = counts_ref[e]
        npad = (bm - (cnt & (bm - 1))) & (bm - 1)
        pos = pstart_ref[e] + cnt
        bit = 1
        while bit < bm:
            out.append(((npad & bit) != 0, pltpu.make_async_copy(
                _slab_rows(zero_ref, 0, bit, nslab), _slab_rows(xs_ref, pos, bit, nslab), sem)))
            pos = pos + (npad & bit)
            bit *= 2
    return out


def _dispatch_kernel(counts_ref, pstart_ref, nused_ref, meta_ref, hn_ref, xs_ref, zero_ref, sem, psem, *, bm):
    nslab = SUBLANES
    tm = hn_ref.shape[0] // nslab
    half = bm // 2

    @pl.when(pl.program_id(0) == 0)
    def _():
        zero_ref[...] = jnp.zeros_like(zero_ref)
        pads = _pad_copies(counts_ref, pstart_ref, zero_ref, xs_ref, psem, bm=bm)
        for pred, cp in pads:
            pl.when(pred)(cp.start)

        def tail_copy(j):
            return pltpu.make_async_copy(zero_ref, _slab_rows(xs_ref, j * half, half, nslab), psem)

        first = nused_ref[0] * 2
        last = xs_ref.shape[0] // (half * nslab)
        lax.fori_loop(first, last, lambda j, c: (tail_copy(j).start(), c)[1], 0)
        for pred, cp in pads:
            pl.when(pred)(cp.wait)
        lax.fori_loop(first, last, lambda j, c: (tail_copy(j).wait(), c)[1], 0)

    def row_copy(t, k):
        dst = pstart_ref[meta_ref[k, t]] + meta_ref[TOP_K + k, t]
        return pltpu.make_async_copy(_slab_rows(hn_ref, t, 1, nslab), _slab_rows(xs_ref, dst, 1, nslab), sem)

    def issue(t, carry):
        for k in range(TOP_K):
            row_copy(t, k).start(priority=k)
        return carry

    lax.fori_loop(0, tm, issue, 0, unroll=8)

    def drain(t, carry):
        for k in range(TOP_K):
            row_copy(t, k).wait()
        return carry

    lax.fori_loop(0, tm, drain, 0, unroll=8)


def _dispatch(counts, pstart, n_used, meta, hn, n_rows, *, bm):
    nslab = SUBLANES
    t = hn.shape[0] // nslab
    tm = meta.shape[2]
    kern = functools.partial(_dispatch_kernel, bm=bm)
    grid_spec = pltpu.PrefetchScalarGridSpec(
        num_scalar_prefetch=3,
        grid=(t // tm,),
        in_specs=[pl.BlockSpec((None, SUBLANES, tm), lambda i, *_: (i, 0, 0), memory_space=pltpu.SMEM),
                  pl.BlockSpec((tm * nslab, LANES), lambda i, *_: (i, 0))],
        out_specs=pl.BlockSpec(memory_space=pl.ANY),
        scratch_shapes=[pltpu.VMEM((bm // 2 * nslab, LANES), F32), pltpu.SemaphoreType.DMA(()),
                        pltpu.SemaphoreType.DMA(())],
    )
    return pl.pallas_call(
        kern,
        grid_spec=grid_spec,
        out_shape=jax.ShapeDtypeStruct((n_rows * nslab, LANES), F32),
        compiler_params=_cparams("arbitrary"),
        name="moe_dispatch",
    )(counts, pstart, n_used, meta, hn)


def _experts_kernel(be_ref, nu_ref, x_ref, wgu_ref, wd_ref, y_ref, wgu_b, wd_b):
    i = pl.program_id(0)
    d = wgu_b.shape[0]
    bm = x_ref.shape[0] * LANES // d
    prev = be_ref[jnp.maximum(i - 1, 0)]

    @pl.when((i < nu_ref[0]) & ((i == 0) | (be_ref[i] != prev)))
    def _():
        wgu_b[...] = wgu_ref[...].astype(BF16)
        wd_b[...] = wd_ref[...].astype(BF16)

    @pl.when(i < nu_ref[0])
    def _():
        x = _slabs_to_rows(x_ref, bm, d // LANES).astype(BF16)
        hcat = jnp.dot(x, wgu_b[...], preferred_element_type=F32)
        ff = hcat.shape[1] // 2
        a = hcat[:, :ff]
        b = hcat[:, ff:]
        act = ((a * _sigmoid(a)) * b).astype(BF16)
        _rows_to_slabs(y_ref, jnp.dot(act, wd_b[...], preferred_element_type=F32))

    @pl.when(i >= nu_ref[0])
    def _():
        y_ref[...] = jnp.zeros_like(y_ref)


def _experts(block_e, n_used, xs, wgu_all, wd_all, layer, *, bm):
    d = wgu_all.shape[2]
    nslab = d // LANES
    n_rows = xs.shape[0] // nslab
    ff2 = wgu_all.shape[3]
    grid_spec = pltpu.PrefetchScalarGridSpec(
        num_scalar_prefetch=2,
        grid=(n_rows // bm,),
        in_specs=[pl.BlockSpec((bm * nslab, LANES), lambda i, be, nu: (i, 0)),
                  pl.BlockSpec((None, None, d, ff2), lambda i, be, nu: (layer, be[i], 0, 0)),
                  pl.BlockSpec((None, None, ff2 // 2, d), lambda i, be, nu: (layer, be[i], 0, 0))],
        out_specs=pl.BlockSpec((bm * nslab, LANES), lambda i, be, nu: (i, 0)),
        scratch_shapes=[pltpu.VMEM((d, ff2), BF16), pltpu.VMEM((ff2 // 2, d), BF16)],
    )
    return pl.pallas_call(
        _experts_kernel,
        grid_spec=grid_spec,
        out_shape=jax.ShapeDtypeStruct((n_rows * nslab, LANES), F32),
        compiler_params=_cparams("arbitrary"),
        name="moe_experts",
    )(block_e, n_used, xs, wgu_all, wd_all)


def _combine_kernel(pstart_ref, meta_ref, h1_ref, route_ref, gfin_ref, ys_ref, o_ref, buf_ref, sem,
                    *, final_norm):
    tm, d = h1_ref.shape
    nslab = d // LANES

    def row_copy(t, k):
        src = pstart_ref[meta_ref[k, t]] + meta_ref[TOP_K + k, t]
        return pltpu.make_async_copy(_slab_rows(ys_ref, src, 1, nslab), _slab_rows(buf_ref.at[k], t, 1, nslab), sem)

    def issue(t, carry):
        for k in range(TOP_K):
            row_copy(t, k).start(priority=k)
        return carry

    lax.fori_loop(0, tm, issue, 0, unroll=8)

    def drain(t, carry):
        for k in range(TOP_K):
            row_copy(t, k).wait()
        return carry

    lax.fori_loop(0, tm, drain, 0, unroll=8)

    g0 = route_ref[:, 4:5]
    g1 = route_ref[:, 5:6]
    out = h1_ref[...] + (_slabs_to_rows(buf_ref.at[0], tm, nslab) * g0 + _slabs_to_rows(buf_ref.at[1], tm, nslab) * g1)
    if final_norm:
        out = _rms(out, gfin_ref[...], EPS)
    o_ref[...] = out


def _combine(pstart, meta, h1, route, gfin, ys, *, final_norm):
    t, d = h1.shape
    tm = meta.shape[2]
    kern = functools.partial(_combine_kernel, final_norm=final_norm)
    grid_spec = pltpu.PrefetchScalarGridSpec(
        num_scalar_prefetch=1,
        grid=(t // tm,),
        in_specs=[pl.BlockSpec((None, SUBLANES, tm), lambda i, p: (i, 0, 0), memory_space=pltpu.SMEM),
                  pl.BlockSpec((tm, d), lambda i, p: (i, 0)),
                  pl.BlockSpec((tm, LANES), lambda i, p: (i, 0)),
                  pl.BlockSpec((1, d), lambda i, p: (0, 0)),
                  pl.BlockSpec(memory_space=pl.ANY)],
        out_specs=pl.BlockSpec((tm, d), lambda i, p: (i, 0)),
        scratch_shapes=[pltpu.VMEM((TOP_K, tm * d // LANES, LANES), F32), pltpu.SemaphoreType.DMA(())],
    )
    return pl.pallas_call(
        kern,
        grid_spec=grid_spec,
        out_shape=jax.ShapeDtypeStruct((t, d), F32),
        compiler_params=_cparams("arbitrary"),
        name="moe_combine",
    )(pstart, meta, h1, route, gfin, ys)


def _moe(a, w_out_all, mixer_layer, h, layer, g_ffn, wr_g, br_g, wr_e, br_e, w_gu_all, w_down_all, gfin,
         *, final_norm):
    t, d = h.shape
    bm = MOE_BLOCK
    tm = min(ROW_TILE, t)
    wr = jnp.concatenate([wr_g, wr_e, jnp.zeros((d, LANES - N_GROUPS - N_EXPERTS), F32)], axis=1)
    wrh = wr.astype(BF16)
    wrl = (wr - wrh.astype(F32)).astype(BF16)
    br = jnp.concatenate([br_g, br_e, jnp.zeros((LANES - N_GROUPS - N_EXPERTS,), F32)])[None, :]
    h1, hn, route, meta, cnt = _outproj_router(a, w_out_all, mixer_layer, h, g_ffn[None, :], wrh, wrl, br, tm=tm)

    counts = cnt[0, ROUTE_LANE0:ROUTE_LANE0 + N_EXPERTS].astype(I32)
    nb = (counts + bm - 1) // bm
    nb_end = jnp.cumsum(nb)
    pstart = ((nb_end - nb) * bm).astype(I32)
    n_blocks = (t * TOP_K + N_EXPERTS * (bm - 1)) // bm
    j = jnp.arange(n_blocks, dtype=I32)
    block_e = jnp.minimum(jnp.sum((nb_end[None, :] <= j[:, None]).astype(I32), axis=1), N_EXPERTS - 1)
    n_used = nb_end[-1:].astype(I32)

    xs = _dispatch(counts, pstart, n_used, meta, hn, n_blocks * bm, bm=bm)
    ys = _experts(block_e, n_used, xs, w_gu_all, w_down_all, layer, bm=bm)
    return _combine(pstart, meta, h1, route, gfin[None, :], ys, final_norm=final_norm)


def _fox_mixer(h, g, w_in_all, layer, b_f, *, batch, seq):
    d = h.shape[1]
    nh = FOX_HEADS
    wf = w_in_all[layer, :, 4 * d:4 * d + nh]
    wf = jnp.concatenate([wf] * 6 + [jnp.zeros((d, LANES - 6 * nh), F32)], axis=1).astype(BF16)
    bfr = jnp.concatenate([b_f] * 6 + [jnp.zeros((LANES - 6 * nh,), F32)])[None, :]
    q, kb, v, gate, a = _fox_inproj(h, g[None, :], w_in_all, layer, wf, bfr, seq=seq)
    return _fox_attn(q, a, kb, v, gate, batch=batch, seq=seq)


def _ret_mixer(h, g, w_in_all, layer, gn_gain, *, batch, seq):
    d = h.shape[1]
    nh = RET_HEADS
    dk = d // nh
    c = RET_CHUNK
    inv = 1.0 / (ROPE_BASE ** jnp.linspace(0.0, 1.0, dk // 2, dtype=F32))
    ang = jnp.arange(seq).astype(F32)[:, None] * inv[None, :]
    cos, sin = jnp.cos(ang), jnp.sin(ang)
    log_g = jnp.log(1.0 - 2.0 ** (-5.0 - jnp.arange(nh, dtype=F32)))
    idx = jnp.arange(c, dtype=F32)
    rel = idx[:, None] - idx[None, :]
    intra = jnp.where(rel >= 0, jnp.exp(log_g[:, None, None] * jnp.maximum(rel, 0.0)), 0.0)
    qd = jnp.exp(log_g[:, None] * (idx + 1.0))[:, :, None]
    kd = jnp.exp(log_g[:, None] * (c - 1.0 - idx))[:, :, None]
    cd = jnp.exp(log_g * c)[:, None, None]
    q, k, v, gate = _ret_inproj(h, g[None, :], w_in_all, layer, cos, sin, seq=seq)
    return _retention(q, k, v, gate, gn_gain[None, :], intra, qd, kd, cd, batch=batch, seq=seq)


def kernel(x, fox_w_in, fox_b_f, fox_w_out, ret_w_in, ret_gn_gain, ret_w_out, norm_mix, norm_ffn,
           router_group_w, router_group_b, router_expert_w, router_expert_b, expert_w_gu, expert_w_down,
           norm_final):
    batch, seq, d = x.shape
    depth = norm_mix.shape[0]
    h = x.reshape(batch * seq, d)
    for i in range(depth):
        j = i // 2
        if i % 2 == 0:
            a = _fox_mixer(h, norm_mix[i], fox_w_in, j, fox_b_f[j], batch=batch, seq=seq)
            w_out_all = fox_w_out
        else:
            a = _ret_mixer(h, norm_mix[i], ret_w_in, j, ret_gn_gain[j], batch=batch, seq=seq)
            w_out_all = ret_w_out
        h = _moe(a, w_out_all, j, h, i, norm_ffn[i], router_group_w[i], router_group_b[i], router_expert_w[i],
                 router_expert_b[i], expert_w_gu, expert_w_down, norm_final, final_norm=(i == depth - 1))
    return h.reshape(batch, seq, d)
```

```python
import functools
from typing import NamedTuple

import jax
import jax.numpy as jnp
from jax import lax
from jax.experimental import pallas as pl
from jax.experimental.pallas import tpu as pltpu

F32 = jnp.float32
BF16 = jnp.bfloat16
I32 = jnp.int32

FOX_HEADS = 16
RET_HEADS = 4
RET_CHUNK = 128
N_GROUPS = 4
EXPERTS_PER_GROUP = 8
N_EXPERTS = N_GROUPS * EXPERTS_PER_GROUP
TOP_K = 2
EPS = 1e-6
NEG_INF = -1e30
ROPE_BASE = 10000.0

LANES = 128
SUBLANES = 8
ROUTE_LANE0 = N_GROUPS
VMEM_LIMIT = 56 * 1024 * 1024
MOE_BLOCK = 512
ROW_TILE = 512
WEIGHT_CHUNK = 512

_CONTRACT_LAST = (((1,), (1,)), ((), ()))
_CONTRACT_FIRST = (((0,), (0,)), ((), ()))


def _cparams(*sem):
    return pltpu.CompilerParams(dimension_semantics=sem, vmem_limit_bytes=VMEM_LIMIT)


def _sigmoid(x):
    return 1.0 / (1.0 + jnp.exp(-x))


def _rms(x, g, eps):
    ms = jnp.mean(x * x, axis=-1, keepdims=True)
    return x * lax.rsqrt(ms + eps) * g


def _split3(x):
    hi = x.astype(BF16)
    r = x - hi.astype(F32)
    mid = r.astype(BF16)
    lo = (r - mid.astype(F32)).astype(BF16)
    return hi, mid, lo


def _stage_weight(w_hbm, wb_ref, stage_ref, sem, ncols):
    chunk = stage_ref.shape[2]
    n = ncols // chunk

    def cp(c):
        return pltpu.make_async_copy(w_hbm.at[:, pl.ds(c * chunk, chunk)], stage_ref.at[c % 2], sem.at[c % 2])

    cp(0).start()
    for c in range(n):
        if c + 1 < n:
            cp(c + 1).start()
        cp(c).wait()
        wb_ref[:, c * chunk:(c + 1) * chunk] = stage_ref[c % 2].astype(BF16)


def _rows_to_slabs(ref, x):
    n, d = x.shape
    nslab = d // LANES
    for c in range(nslab):
        ref[pl.ds(c, n, stride=nslab), :] = x[:, c * LANES:(c + 1) * LANES]


def _slabs_to_rows(ref, n, nslab):
    return jnp.concatenate([ref[pl.ds(c, n, stride=nslab), :] for c in range(nslab)], axis=1)


def _slab_rows(ref, row, nrows, nslab):
    return ref.at[pl.ds(pl.multiple_of(row * nslab, nslab), nrows * nslab), :]


class _Pending(NamedTuple):
    pstart: jax.Array
    meta: jax.Array
    h1: jax.Array
    route: jax.Array
    ys: jax.Array


def _issue_residual_gathers(pstart_ref, meta_cur, meta_nxt, ys_ref, buf_ref, gsem, *, tm, sub):
    i = pl.program_id(0)
    nslab = buf_ref.shape[2] // tm

    def issue(meta_ref, step):
        slot = step % 2
        off = (step % sub) * tm

        def body(t, carry):
            for k in range(TOP_K):
                src = pstart_ref[meta_ref[k, off + t]] + meta_ref[TOP_K + k, off + t]
                pltpu.make_async_copy(_slab_rows(ys_ref, src, 1, nslab), _slab_rows(buf_ref.at[slot, k], t, 1, nslab),
                                      gsem.at[slot]).start(priority=k)
            return carry

        lax.fori_loop(0, tm, body, 0, unroll=8)

    @pl.when(i == 0)
    def _():
        issue(meta_cur, i)

    @pl.when(i + 1 < pl.num_programs(0))
    def _():
        issue(meta_nxt, i + 1)


def _gathered_residual(h1_ref, route_ref, buf_ref, gsem):
    tm, d = h1_ref.shape
    nslab = d // LANES
    slot = pl.program_id(0) % 2
    pltpu.make_async_copy(buf_ref.at[slot], buf_ref.at[slot], gsem.at[slot]).wait()
    g0 = route_ref[:, 4:5]
    g1 = route_ref[:, 5:6]
    return h1_ref[...] + (_slabs_to_rows(buf_ref.at[slot, 0], tm, nslab) * g0
                          + _slabs_to_rows(buf_ref.at[slot, 1], tm, nslab) * g1)


def _residual_specs(src, tm, d):
    if not isinstance(src, _Pending):
        return False, 1, [src], [pl.BlockSpec((tm, d), lambda i, p: (i, 0))], [], [], []
    tile = src.meta.shape[2]
    sub = tile // tm
    last = src.h1.shape[0] // tm - 1
    smem = lambda f: pl.BlockSpec((None, SUBLANES, tile), f, memory_space=pltpu.SMEM)
    in_specs = [smem(lambda i, p: (i // sub, 0, 0)), smem(lambda i, p: (jnp.minimum(i + 1, last) // sub, 0, 0)),
                pl.BlockSpec((tm, d), lambda i, p: (i, 0)), pl.BlockSpec((tm, LANES), lambda i, p: (i, 0)),
                pl.BlockSpec(memory_space=pl.ANY)]
    scratch = [pltpu.VMEM((2, TOP_K, tm * d // LANES, LANES), F32), pltpu.SemaphoreType.DMA((2,))]
    out_specs = [pl.BlockSpec((tm, d), lambda i, p: (i, 0))]
    out_shape = [jax.ShapeDtypeStruct(src.h1.shape, F32)]
    return True, sub, [src.meta, src.meta, src.h1, src.route, src.ys], in_specs, out_specs, out_shape, scratch


def _weight_scratch(kdim, ncols):
    return [pltpu.VMEM((kdim, ncols), BF16), pltpu.VMEM((2, kdim, WEIGHT_CHUNK), F32),
            pltpu.SemaphoreType.DMA((2,))]


def _fox_inproj_kernel(pstart_ref, *refs, layer, tiles_per_seq, scale, fused, sub):
    i = pl.program_id(0)
    if fused:
        (meta_cur, meta_nxt, h1_ref, route_ref, ys_ref, g_ref, w_hbm, wf_ref, bf_ref, hout_ref, q_ref, kb_ref,
         v_ref, gate_ref, a_ref, carry_ref, wb_ref, stage_ref, sem, buf_ref, gsem) = refs
        _issue_residual_gathers(pstart_ref, meta_cur, meta_nxt, ys_ref, buf_ref, gsem, tm=h1_ref.shape[0], sub=sub)
    else:
        (h_ref, g_ref, w_hbm, wf_ref, bf_ref, q_ref, kb_ref, v_ref, gate_ref, a_ref,
         carry_ref, wb_ref, stage_ref, sem) = refs
    tm, d = q_ref.shape

    @pl.when(i == 0)
    def _():
        _stage_weight(w_hbm.at[layer], wb_ref, stage_ref, sem, 4 * d)

    if fused:
        x = _gathered_residual(h1_ref, route_ref, buf_ref, gsem)
        hout_ref[...] = x
    else:
        x = h_ref[...]
    xn = _rms(x, g_ref[...], EPS).astype(BF16)

    q = jnp.dot(xn, wb_ref[:, 0:d], preferred_element_type=F32)
    q_ref[...] = (q * scale).astype(BF16)
    v_ref[...] = jnp.dot(xn, wb_ref[:, 2 * d:3 * d], preferred_element_type=F32).astype(BF16)
    gate_ref[...] = jnp.dot(xn, wb_ref[:, 3 * d:4 * d], preferred_element_type=F32)

    f = jnp.dot(xn, wf_ref[...], preferred_element_type=F32) + bf_ref[...]
    ls = jnp.minimum(f, 0.0) - jnp.log1p(jnp.exp(-jnp.abs(f)))

    @pl.when(i % tiles_per_seq == 0)
    def _():
        carry_ref[...] = jnp.zeros_like(carry_ref)

    row = lax.broadcasted_iota(I32, (tm, tm), 0)
    col = lax.broadcasted_iota(I32, (tm, tm), 1)
    tri = jnp.where(col <= row, 1.0, 0.0).astype(BF16)
    hi, mid, lo = _split3(ls)
    cum = (jnp.dot(tri, hi, preferred_element_type=F32) + jnp.dot(tri, mid, preferred_element_type=F32)
           + jnp.dot(tri, lo, preferred_element_type=F32)) + carry_ref[...]
    carry_ref[...] = cum[tm - 1:tm, :]

    c_hi, c_mid, c_lo = (c.astype(F32) for c in _split3(cum))
    lane = lax.broadcasted_iota(I32, (tm, LANES), 1)
    a_ref[...] = jnp.where(lane < 16, c_hi, jnp.where(lane < 32, c_mid, jnp.where(
        lane < 48, c_lo, jnp.where(lane < 96, 1.0, 0.0)))).astype(BF16)
    bk = jnp.where(lane < 48, 1.0, jnp.where(lane < 64, -c_hi, jnp.where(
        lane < 80, -c_mid, jnp.where(lane < 96, -c_lo, 0.0)))).astype(BF16)

    k = jnp.dot(xn, wb_ref[:, d:2 * d], preferred_element_type=F32).astype(BF16)
    for p in range(d // LANES):
        kb_ref[:, 2 * p * LANES:(2 * p + 1) * LANES] = k[:, p * LANES:(p + 1) * LANES]
        kb_ref[:, (2 * p + 1) * LANES:(2 * p + 2) * LANES] = bk


def _fox_inproj(src, pstart, g, w_all, layer, wf, bfr, *, seq):
    t, d = src.h1.shape if isinstance(src, _Pending) else src.shape
    tm = min(ROW_TILE, seq)
    scale = float((d // FOX_HEADS) ** -0.5)
    fused, sub, operands, in_specs, out_specs, out_shape, scratch = _residual_specs(src, tm, d)
    kern = functools.partial(_fox_inproj_kernel, layer=layer, tiles_per_seq=seq // tm, scale=scale,
                             fused=fused, sub=sub)
    row = lambda i, p: (i, 0)
    const = lambda i, p: (0, 0)
    grid_spec = pltpu.PrefetchScalarGridSpec(
        num_scalar_prefetch=1,
        grid=(t // tm,),
        in_specs=in_specs + [pl.BlockSpec((1, d), const), pl.BlockSpec(memory_space=pl.ANY),
                             pl.BlockSpec((d, LANES), const), pl.BlockSpec((1, LANES), const)],
        out_specs=out_specs + [pl.BlockSpec((tm, d), row), pl.BlockSpec((tm, 2 * d), row),
                               pl.BlockSpec((tm, d), row), pl.BlockSpec((tm, d), row),
                               pl.BlockSpec((tm, LANES), row)],
        scratch_shapes=[pltpu.VMEM((1, LANES), F32)] + _weight_scratch(d, 4 * d) + scratch,
    )
    outs = pl.pallas_call(
        kern,
        grid_spec=grid_spec,
        out_shape=out_shape + [jax.ShapeDtypeStruct((t, d), BF16), jax.ShapeDtypeStruct((t, 2 * d), BF16),
                               jax.ShapeDtypeStruct((t, d), BF16), jax.ShapeDtypeStruct((t, d), F32),
                               jax.ShapeDtypeStruct((t, LANES), BF16)],
        compiler_params=_cparams("arbitrary"),
        name="fox_inproj",
    )(pstart, *operands, g, w_all, wf, bfr)
    return (outs if fused else [src] + list(outs))


def _fox_attn_kernel(q_ref, a_ref, kb_ref, v_ref, gate_ref, o_ref, *, tq, dh, depth):
    seq = q_ref.shape[0]
    pair = pl.program_id(1)
    lane = lax.broadcasted_iota(I32, (tq, LANES), 1)
    krow = lax.broadcasted_iota(I32, (tq, tq), 0)
    qcol = lax.broadcasted_iota(I32, (tq, tq), 1)
    causal = krow <= qcol
    heads_per_block = LANES // dh
    qmask = [jnp.where((lane >= hh * dh) & (lane < (hh + 1) * dh), 1.0, 0.0).astype(BF16)
             for hh in range(heads_per_block)]
    amask = [jnp.where((lane < 96) & ((lane & 15) == pair * heads_per_block + hh), 1.0, 0.0).astype(BF16)
             for hh in range(heads_per_block)]
    vt = jnp.transpose(v_ref[...])
    chains = [(i, hh) for i in range(seq // tq) for hh in range(heads_per_block)]

    def scores(i, hh):
        off = i * tq
        qa = jnp.concatenate([q_ref[off:off + tq, :] * qmask[hh], a_ref[off:off + tq, :] * amask[hh]], axis=1)
        return lax.dot_general(kb_ref[0:off + tq, :], qa, _CONTRACT_LAST, preferred_element_type=F32)

    def finish(i, hh, st):
        off = i * tq
        sd = jnp.where(causal, st[off:off + tq, :], NEG_INF)
        st = jnp.concatenate([st[0:off, :], sd], axis=0) if off > 0 else sd
        m = jnp.max(st, axis=0, keepdims=True)
        e = jnp.exp(st - m)
        p = (e * (1.0 / jnp.sum(e, axis=0, keepdims=True))).astype(BF16)
        return jnp.dot(vt[hh * dh:(hh + 1) * dh, 0:off + tq], p, preferred_element_type=F32)

    outs = {}
    queue = [scores(*c) for c in chains[:depth]]
    for n, (i, hh) in enumerate(chains):
        cur = queue.pop(0)
        if n + depth < len(chains):
            queue.append(scores(*chains[n + depth]))
        outs[hh] = finish(i, hh, cur)
        if hh == heads_per_block - 1:
            off = i * tq
            o = jnp.transpose(jnp.concatenate([outs[h2] for h2 in range(heads_per_block)], axis=0))
            o_ref[off:off + tq, :] = (o * _sigmoid(gate_ref[off:off + tq, :])).astype(BF16)


def _fox_attn(q, a, kb, v, gate, *, batch, seq, tq=256, depth=3):
    t, d = q.shape
    dh = d // FOX_HEADS
    pairs = d // LANES
    kern = functools.partial(_fox_attn_kernel, tq=tq, dh=dh, depth=depth)
    blk = lambda b, p: (b, p)
    return pl.pallas_call(
        kern,
        grid=(batch, pairs),
        in_specs=[pl.BlockSpec((seq, LANES), blk), pl.BlockSpec((seq, LANES), lambda b, p: (b, 0)),
                  pl.BlockSpec((seq, 2 * LANES), blk), pl.BlockSpec((seq, LANES), blk),
                  pl.BlockSpec((seq, LANES), blk)],
        out_specs=pl.BlockSpec((seq, LANES), blk),
        out_shape=jax.ShapeDtypeStruct((t, d), BF16),
        compiler_params=_cparams("arbitrary", "arbitrary"),
        name="fox_attn",
    )(q, a, kb, v, gate)


def _ret_inproj_kernel(pstart_ref, *refs, layer, kscale, fused, sub):
    if fused:
        (meta_cur, meta_nxt, h1_ref, route_ref, ys_ref, g_ref, w_hbm, cos_ref, sin_ref, hout_ref, q_ref, k_ref,
         v_ref, gate_ref, wb_ref, stage_ref, sem, buf_ref, gsem) = refs
        _issue_residual_gathers(pstart_ref, meta_cur, meta_nxt, ys_ref, buf_ref, gsem, tm=h1_ref.shape[0], sub=sub)
    else:
        h_ref, g_ref, w_hbm, cos_ref, sin_ref, q_ref, k_ref, v_ref, gate_ref, wb_ref, stage_ref, sem = refs
    tm, d = q_ref.shape

    @pl.when(pl.program_id(0) == 0)
    def _():
        _stage_weight(w_hbm.at[layer], wb_ref, stage_ref, sem, 6 * d)

    if fused:
        x = _gathered_residual(h1_ref, route_ref, buf_ref, gsem)
        hout_ref[...] = x
    else:
        x = h_ref[...]
    xn = _rms(x, g_ref[...], EPS).astype(BF16)
    cos = cos_ref[...]
    sin = sin_ref[...]
    dk = d // RET_HEADS
    half = dk // 2

    def rot(t, h):
        t1 = t[:, h * dk:h * dk + half]
        t2 = t[:, h * dk + half:(h + 1) * dk]
        return t1 * cos - t2 * sin, t1 * sin + t2 * cos

    q = jnp.dot(xn, wb_ref[:, 0:d], preferred_element_type=F32)
    k = jnp.dot(xn, wb_ref[:, d:2 * d], preferred_element_type=F32)
    for h in range(RET_HEADS):
        q1, q2 = rot(q, h)
        q_ref[:, h * dk:h * dk + half] = q1.astype(BF16)
        q_ref[:, h * dk + half:(h + 1) * dk] = q2.astype(BF16)
        k1, k2 = rot(k, h)
        k_ref[:, h * dk:h * dk + half] = k1 * kscale
        k_ref[:, h * dk + half:(h + 1) * dk] = k2 * kscale
    v_ref[...] = jnp.dot(xn, wb_ref[:, 2 * d:4 * d], preferred_element_type=F32).astype(BF16)
    gate_ref[...] = jnp.dot(xn, wb_ref[:, 4 * d:6 * d], preferred_element_type=F32)


def _ret_inproj(src, pstart, g, w_all, layer, cos, sin, *, seq, tm=256):
    t, d = src.h1.shape if isinstance(src, _Pending) else src.shape
    tm = min(tm, seq)
    half = d // RET_HEADS // 2
    fused, sub, operands, in_specs, out_specs, out_shape, scratch = _residual_specs(src, tm, d)
    kern = functools.partial(_ret_inproj_kernel, layer=layer, kscale=float((d // RET_HEADS) ** -0.5),
                             fused=fused, sub=sub)
    row = lambda i, p: (i, 0)
    const = lambda i, p: (0, 0)
    pos = lambda i, p: (i % (seq // tm), 0)
    grid_spec = pltpu.PrefetchScalarGridSpec(
        num_scalar_prefetch=1,
        grid=(t // tm,),
        in_specs=in_specs + [pl.BlockSpec((1, d), const), pl.BlockSpec(memory_space=pl.ANY),
                             pl.BlockSpec((tm, half), pos), pl.BlockSpec((tm, half), pos)],
        out_specs=out_specs + [pl.BlockSpec((tm, d), row), pl.BlockSpec((tm, d), row),
                               pl.BlockSpec((tm, 2 * d), row), pl.BlockSpec((tm, 2 * d), row)],
        scratch_shapes=_weight_scratch(d, 6 * d) + scratch,
    )
    outs = pl.pallas_call(
        kern,
        grid_spec=grid_spec,
        out_shape=out_shape + [jax.ShapeDtypeStruct((t, d), BF16), jax.ShapeDtypeStruct((t, d), F32),
                               jax.ShapeDtypeStruct((t, 2 * d), BF16), jax.ShapeDtypeStruct((t, 2 * d), F32)],
        compiler_params=_cparams("arbitrary"),
        name="ret_inproj",
    )(pstart, *operands, g, w_all, cos, sin)
    return (outs if fused else [src] + list(outs))


def _retention_kernel(q_ref, k_ref, v_ref, gate_ref, gain_ref, intra_ref, qd_ref, kd_ref, cd_ref,
                      y_ref, r_ref, *, chunk):
    seq = q_ref.shape[0]
    r_ref[...] = jnp.zeros_like(r_ref)

    def body(c):
        off = c * chunk
        qn = q_ref[pl.ds(off, chunk), :]
        k32 = k_ref[pl.ds(off, chunk), :]
        vn = v_ref[pl.ds(off, chunk), :]
        s = lax.dot_general(qn, k32.astype(BF16), _CONTRACT_LAST, preferred_element_type=F32) * intra_ref[...]
        o = jnp.dot(s.astype(BF16), vn, preferred_element_type=F32)
        r = r_ref[...]
        o = o + jnp.dot(qn, r.astype(BF16), preferred_element_type=F32) * qd_ref[...]
        kdec = (k32 * kd_ref[...]).astype(BF16)
        r_ref[...] = r * cd_ref[...] + lax.dot_general(kdec, vn, _CONTRACT_FIRST, preferred_element_type=F32)
        mu = jnp.mean(o, axis=-1, keepdims=True)
        dlt = o - mu
        var = jnp.mean(dlt * dlt, axis=-1, keepdims=True)
        on = dlt * lax.rsqrt(var + EPS) * gain_ref[...]
        g = gate_ref[pl.ds(off, chunk), :]
        y_ref[pl.ds(off, chunk), :] = ((g * _sigmoid(g)) * on).astype(BF16)

    for c in range(seq // chunk):
        body(c)


def _retention(q, k, v, gate, gain, intra, qd, kd, cd, *, batch, seq):
    t, d = q.shape
    dk = d // RET_HEADS
    dv = v.shape[1] // RET_HEADS
    c = RET_CHUNK
    kern = functools.partial(_retention_kernel, chunk=c)
    blk = lambda b, h: (b, h)
    hd = lambda b, h: (h, 0, 0)
    return pl.pallas_call(
        kern,
        grid=(batch, RET_HEADS),
        in_specs=[pl.BlockSpec((seq, dk), blk), pl.BlockSpec((seq, dk), blk), pl.BlockSpec((seq, dv), blk),
                  pl.BlockSpec((seq, dv), blk), pl.BlockSpec((1, dv), lambda b, h: (0, h)),
                  pl.BlockSpec((None, c, c), hd), pl.BlockSpec((None, c, 1), hd),
                  pl.BlockSpec((None, c, 1), hd), pl.BlockSpec((None, 1, 1), hd)],
        out_specs=pl.BlockSpec((seq, dv), blk),
        out_shape=jax.ShapeDtypeStruct((t, v.shape[1]), BF16),
        scratch_shapes=[pltpu.VMEM((dk, dv), F32)],
        compiler_params=_cparams("arbitrary", "arbitrary"),
        name="retention",
    )(q, k, v, gate, gain, intra, qd, kd, cd)


def _outproj_router_kernel(a_ref, w_hbm, h_ref, g_ref, wrh_ref, wrl_ref, br_ref,
                           h1_ref, hn_ref, route_ref, meta_ref, cnt_ref, wb_ref, stage_ref, sem, *, layer):
    i = pl.program_id(0)
    tm = a_ref.shape[0]
    d = h_ref.shape[1]

    @pl.when(i == 0)
    def _():
        cnt_ref[...] = jnp.zeros_like(cnt_ref)
        _stage_weight(w_hbm.at[layer], wb_ref, stage_ref, sem, d)

    h1 = h_ref[...] + jnp.dot(a_ref[...], wb_ref[...], preferred_element_type=F32)
    h1_ref[...] = h1
    hn = _rms(h1, g_ref[...], EPS)
    _rows_to_slabs(hn_ref, hn)

    hh = hn.astype(BF16)
    hl = (hn - hh.astype(F32)).astype(BF16)
    lg = (jnp.dot(hh, wrh_ref[...], preferred_element_type=F32)
          + jnp.dot(hh, wrl_ref[...], preferred_element_type=F32)
          + jnp.dot(hl, wrh_ref[...], preferred_element_type=F32)) + br_ref[...]

    lane = lax.broadcasted_iota(I32, (tm, LANES), 1)
    lanef = lane.astype(F32)
    big = float(LANES)

    def softmax_masked(mask):
        mx = jnp.max(jnp.where(mask, lg, -jnp.inf), axis=1, keepdims=True)
        e = jnp.where(mask, jnp.exp(lg - mx), 0.0)
        return e / jnp.sum(e, axis=1, keepdims=True)

    def top1(p, mask):
        best = jnp.max(jnp.where(mask, p, -1.0), axis=1, keepdims=True)
        idx = jnp.min(jnp.where(mask & (p == best), lanef, big), axis=1, keepdims=True)
        return best, idx

    gmask = lane < N_GROUPS
    pg, gi = top1(softmax_masked(gmask), gmask)
    lo = ROUTE_LANE0 + gi * EXPERTS_PER_GROUP
    emask = (lanef >= lo) & (lanef < lo + EXPERTS_PER_GROUP)
    pe = softmax_masked(emask)
    p1, i1 = top1(pe, emask)
    p2, i2 = top1(pe, emask & (lanef != i1))
    den = p1 + p2
    g1 = pg * p1 / den
    g2 = pg * p2 / den

    oh1 = lanef == i1
    oh2 = lanef == i2
    both = jnp.where(oh1 | oh2, 1.0, 0.0)
    row = lax.broadcasted_iota(I32, (tm, tm), 0)
    col = lax.broadcasted_iota(I32, (tm, tm), 1)
    strict = jnp.where(col < row, 1.0, 0.0).astype(BF16)
    before = jnp.dot(strict, both.astype(BF16), preferred_element_type=F32) + cnt_ref[0:1, :]
    r1 = jnp.sum(jnp.where(oh1, before, 0.0), axis=1, keepdims=True)
    r2 = jnp.sum(jnp.where(oh2, before, 0.0), axis=1, keepdims=True)
    cnt_ref[0:1, :] = cnt_ref[0:1, :] + jnp.sum(both, axis=0, keepdims=True)

    e1 = i1 - ROUTE_LANE0
    e2 = i2 - ROUTE_LANE0
    route = jnp.where(lane == 0, e1, jnp.where(lane == 1, e2, jnp.where(
        lane == 2, r1, jnp.where(lane == 3, r2, jnp.where(lane == 4, g1, jnp.where(lane == 5, g2, 0.0))))))
    route_ref[...] = route
    meta_ref[...] = jnp.transpose(route)[0:SUBLANES, :].astype(I32)


def _outproj_router(a, w_all, layer, h, g, wrh, wrl, br, *, tm):
    t, kdim = a.shape
    d = h.shape[1]
    row = lambda i: (i, 0)
    const = lambda i: (0, 0)
    kern = functools.partial(_outproj_router_kernel, layer=layer)
    return pl.pallas_call(
        kern,
        grid=(t // tm,),
        in_specs=[pl.BlockSpec((tm, kdim), row), pl.BlockSpec(memory_space=pl.ANY), pl.BlockSpec((tm, d), row),
                  pl.BlockSpec((1, d), const), pl.BlockSpec((d, LANES), const), pl.BlockSpec((d, LANES), const),
                  pl.BlockSpec((1, LANES), const)],
        out_specs=[pl.BlockSpec((tm, d), row), pl.BlockSpec((tm * d // LANES, LANES), row),
                   pl.BlockSpec((tm, LANES), row), pl.BlockSpec((None, SUBLANES, tm), lambda i: (i, 0, 0)),
                   pl.BlockSpec((SUBLANES, LANES), const)],
        out_shape=[jax.ShapeDtypeStruct((t, d), F32), jax.ShapeDtypeStruct((t * d // LANES, LANES), F32),
                   jax.ShapeDtypeStruct((t, LANES), F32), jax.ShapeDtypeStruct((t // tm, SUBLANES, tm), I32),
                   jax.ShapeDtypeStruct((SUBLANES, LANES), F32)],
        scratch_shapes=_weight_scratch(kdim, d),
        compiler_params=_cparams("arbitrary"),
        name="outproj_router",
    )(a, w_all, h, g, wrh, wrl, br)


def _pad_copies(counts_ref, pstart_ref, zero_ref, xs_ref, sem, *, bm):
    nslab = SUBLANES
    out = []
    for e in range(N_EXPERTS):
        cnt = counts_ref[e]
        npad = (bm - (cnt & (bm - 1))) & (bm - 1)
        pos = pstart_ref[e] + cnt
        bit = 1
        while bit < bm:
            out.append(((npad & bit) != 0, pltpu.make_async_copy(
                _slab_rows(zero_ref, 0, bit, nslab), _slab_rows(xs_ref, pos, bit, nslab), sem)))
            pos = pos + (npad & bit)
            bit *= 2
    return out


def _dispatch_kernel(counts_ref, pstart_ref, nused_ref, meta_ref, hn_ref, xs_ref, zero_ref, sem, psem, *, bm):
    nslab = SUBLANES
    tm = hn_ref.shape[0] // nslab
    half = bm // 2

    @pl.when(pl.program_id(0) == 0)
    def _():
        zero_ref[...] = jnp.zeros_like(zero_ref)
        pads = _pad_copies(counts_ref, pstart_ref, zero_ref, xs_ref, psem, bm=bm)
        for pred, cp in pads:
            pl.when(pred)(cp.start)

        def tail_copy(j):
            return pltpu.make_async_copy(zero_ref, _slab_rows(xs_ref, j * half, half, nslab), psem)

        first = nused_ref[0] * 2
        last = xs_ref.shape[0] // (half * nslab)
        lax.fori_loop(first, last, lambda j, c: (tail_copy(j).start(), c)[1], 0)
        for pred, cp in pads:
            pl.when(pred)(cp.wait)
        lax.fori_loop(first, last, lambda j, c: (tail_copy(j).wait(), c)[1], 0)

    def row_copy(t, k):
        dst = pstart_ref[meta_ref[k, t]] + meta_ref[TOP_K + k, t]
        return pltpu.make_async_copy(_slab_rows(hn_ref, t, 1, nslab), _slab_rows(xs_ref, dst, 1, nslab), sem)

    def issue(t, carry):
        for k in range(TOP_K):
            row_copy(t, k).start(priority=k)
        return carry

    lax.fori_loop(0, tm, issue, 0, unroll=8)

    def drain(t, carry):
        for k in range(TOP_K):
            row_copy(t, k).wait()
        return carry

    lax.fori_loop(0, tm, drain, 0, unroll=8)


def _dispatch(counts, pstart, n_used, meta, hn, n_rows, *, bm):
    nslab = SUBLANES
    t = hn.shape[0] // nslab
    tm = meta.shape[2]
    kern = functools.partial(_dispatch_kernel, bm=bm)
    grid_spec = pltpu.PrefetchScalarGridSpec(
        num_scalar_prefetch=3,
        grid=(t // tm,),
        in_specs=[pl.BlockSpec((None, SUBLANES, tm), lambda i, *_: (i, 0, 0), memory_space=pltpu.SMEM),
                  pl.BlockSpec((tm * nslab, LANES), lambda i, *_: (i, 0))],
        out_specs=pl.BlockSpec(memory_space=pl.ANY),
        scratch_shapes=[pltpu.VMEM((bm // 2 * nslab, LANES), F32), pltpu.SemaphoreType.DMA(()),
                        pltpu.SemaphoreType.DMA(())],
    )
    return pl.pallas_call(
        kern,
        grid_spec=grid_spec,
        out_shape=jax.ShapeDtypeStruct((n_rows * nslab, LANES), F32),
        compiler_params=_cparams("arbitrary"),
        name="moe_dispatch",
    )(counts, pstart, n_used, meta, hn)


def _experts_kernel(be_ref, nu_ref, x_ref, wgu_ref, wd_ref, y_ref, wgu_b, wd_b):
    i = pl.program_id(0)
    d = wgu_b.shape[0]
    bm = x_ref.shape[0] * LANES // d
    prev = be_ref[jnp.maximum(i - 1, 0)]

    @pl.when((i < nu_ref[0]) & ((i == 0) | (be_ref[i] != prev)))
    def _():
        wgu_b[...] = wgu_ref[...].astype(BF16)
        wd_b[...] = wd_ref[...].astype(BF16)

    @pl.when(i < nu_ref[0])
    def _():
        x = _slabs_to_rows(x_ref, bm, d // LANES).astype(BF16)
        hcat = jnp.dot(x, wgu_b[...], preferred_element_type=F32)
        ff = hcat.shape[1] // 2
        a = hcat[:, :ff]
        b = hcat[:, ff:]
        act = ((a * _sigmoid(a)) * b).astype(BF16)
        _rows_to_slabs(y_ref, jnp.dot(act, wd_b[...], preferred_element_type=F32))

    @pl.when(i >= nu_ref[0])
    def _():
        y_ref[...] = jnp.zeros_like(y_ref)


def _experts(block_e, n_used, xs, wgu_all, wd_all, layer, *, bm):
    d = wgu_all.shape[2]
    nslab = d // LANES
    n_rows = xs.shape[0] // nslab
    ff2 = wgu_all.shape[3]
    grid_spec = pltpu.PrefetchScalarGridSpec(
        num_scalar_prefetch=2,
        grid=(n_rows // bm,),
        in_specs=[pl.BlockSpec((bm * nslab, LANES), lambda i, be, nu: (i, 0)),
                  pl.BlockSpec((None, None, d, ff2), lambda i, be, nu: (layer, be[i], 0, 0)),
                  pl.BlockSpec((None, None, ff2 // 2, d), lambda i, be, nu: (layer, be[i], 0, 0))],
        out_specs=pl.BlockSpec((bm * nslab, LANES), lambda i, be, nu: (i, 0)),
        scratch_shapes=[pltpu.VMEM((d, ff2), BF16), pltpu.VMEM((ff2 // 2, d), BF16)],
    )
    return pl.pallas_call(
        _experts_kernel,
        grid_spec=grid_spec,
        out_shape=jax.ShapeDtypeStruct((n_rows * nslab, LANES), F32),
        compiler_params=_cparams("arbitrary"),
        name="moe_experts",
    )(block_e, n_used, xs, wgu_all, wd_all)


def _combine_kernel(pstart_ref, meta_ref, h1_ref, route_ref, gfin_ref, ys_ref, o_ref, buf_ref, sem,
                    *, final_norm):
    tm, d = h1_ref.shape
    nslab = d // LANES

    def row_copy(t, k):
        src = pstart_ref[meta_ref[k, t]] + meta_ref[TOP_K + k, t]
        return pltpu.make_async_copy(_slab_rows(ys_ref, src, 1, nslab), _slab_rows(buf_ref.at[k], t, 1, nslab), sem)

    def issue(t, carry):
        for k in range(TOP_K):
            row_copy(t, k).start(priority=k)
        return carry

    lax.fori_loop(0, tm, issue, 0, unroll=8)

    def drain(t, carry):
        for k in range(TOP_K):
            row_copy(t, k).wait()
        return carry

    lax.fori_loop(0, tm, drain, 0, unroll=8)

    g0 = route_ref[:, 4:5]
    g1 = route_ref[:, 5:6]
    out = h1_ref[...] + (_slabs_to_rows(buf_ref.at[0], tm, nslab) * g0 + _slabs_to_rows(buf_ref.at[1], tm, nslab) * g1)
    if final_norm:
        out = _rms(out, gfin_ref[...], EPS)
    o_ref[...] = out


def _combine(pstart, meta, h1, route, gfin, ys, *, final_norm):
    t, d = h1.shape
    tm = meta.shape[2]
    kern = functools.partial(_combine_kernel, final_norm=final_norm)
    grid_spec = pltpu.PrefetchScalarGridSpec(
        num_scalar_prefetch=1,
        grid=(t // tm,),
        in_specs=[pl.BlockSpec((None, SUBLANES, tm), lambda i, p: (i, 0, 0), memory_space=pltpu.SMEM),
                  pl.BlockSpec((tm, d), lambda i, p: (i, 0)),
                  pl.BlockSpec((tm, LANES), lambda i, p: (i, 0)),
                  pl.BlockSpec((1, d), lambda i, p: (0, 0)),
                  pl.BlockSpec(memory_space=pl.ANY)],
        out_specs=pl.BlockSpec((tm, d), lambda i, p: (i, 0)),
        scratch_shapes=[pltpu.VMEM((TOP_K, tm * d // LANES, LANES), F32), pltpu.SemaphoreType.DMA(())],
    )
    return pl.pallas_call(
        kern,
        grid_spec=grid_spec,
        out_shape=jax.ShapeDtypeStruct((t, d), F32),
        compiler_params=_cparams("arbitrary"),
        name="moe_combine",
    )(pstart, meta, h1, route, gfin, ys)


def _moe(a, w_out_all, mixer_layer, h, layer, g_ffn, wr_g, br_g, wr_e, br_e, w_gu_all, w_down_all, gfin,
         *, final_norm):
    t, d = h.shape
    bm = MOE_BLOCK
    tm = min(ROW_TILE, t)
    wr = jnp.concatenate([wr_g, wr_e, jnp.zeros((d, LANES - N_GROUPS - N_EXPERTS), F32)], axis=1)
    wrh = wr.astype(BF16)
    wrl = (wr - wrh.astype(F32)).astype(BF16)
    br = jnp.concatenate([br_g, br_e, jnp.zeros((LANES - N_GROUPS - N_EXPERTS,), F32)])[None, :]
    h1, hn, route, meta, cnt = _outproj_router(a, w_out_all, mixer_layer, h, g_ffn[None, :], wrh, wrl, br, tm=tm)

    counts = cnt[0, ROUTE_LANE0:ROUTE_LANE0 + N_EXPERTS].astype(I32)
    nb = (counts + bm - 1) // bm
    nb_end = jnp.cumsum(nb)
    pstart = ((nb_end - nb) * bm).astype(I32)
    n_blocks = (t * TOP_K + N_EXPERTS * (bm - 1)) // bm
    j = jnp.arange(n_blocks, dtype=I32)
    block_e = jnp.minimum(jnp.sum((nb_end[None, :] <= j[:, None]).astype(I32), axis=1), N_EXPERTS - 1)
    n_used = nb_end[-1:].astype(I32)

    xs = _dispatch(counts, pstart, n_used, meta, hn, n_blocks * bm, bm=bm)
    ys = _experts(block_e, n_used, xs, w_gu_all, w_down_all, layer, bm=bm)
    if final_norm:
        return _combine(pstart, meta, h1, route, gfin[None, :], ys, final_norm=True)
    return _Pending(pstart, meta, h1, route, ys)


def _src_pstart(src):
    return src.pstart if isinstance(src, _Pending) else jnp.zeros((N_EXPERTS,), I32)


def _fox_mixer(src, g, w_in_all, layer, b_f, *, batch, seq):
    d = w_in_all.shape[1]
    nh = FOX_HEADS
    wf = w_in_all[layer, :, 4 * d:4 * d + nh]
    wf = jnp.concatenate([wf] * 6 + [jnp.zeros((d, LANES - 6 * nh), F32)], axis=1).astype(BF16)
    bfr = jnp.concatenate([b_f] * 6 + [jnp.zeros((LANES - 6 * nh,), F32)])[None, :]
    h, q, kb, v, gate, a = _fox_inproj(src, _src_pstart(src), g[None, :], w_in_all, layer, wf, bfr, seq=seq)
    return h, _fox_attn(q, a, kb, v, gate, batch=batch, seq=seq)


def _ret_mixer(src, g, w_in_all, layer, gn_gain, *, batch, seq):
    d = w_in_all.shape[1]
    nh = RET_HEADS
    dk = d // nh
    c = RET_CHUNK
    inv = 1.0 / (ROPE_BASE ** jnp.linspace(0.0, 1.0, dk // 2, dtype=F32))
    ang = jnp.arange(seq).astype(F32)[:, None] * inv[None, :]
    cos, sin = jnp.cos(ang), jnp.sin(ang)
    log_g = jnp.log(1.0 - 2.0 ** (-5.0 - jnp.arange(nh, dtype=F32)))
    idx = jnp.arange(c, dtype=F32)
    rel = idx[:, None] - idx[None, :]
    intra = jnp.where(rel >= 0, jnp.exp(log_g[:, None, None] * jnp.maximum(rel, 0.0)), 0.0)
    qd = jnp.exp(log_g[:, None] * (idx + 1.0))[:, :, None]
    kd = jnp.exp(log_g[:, None] * (c - 1.0 - idx))[:, :, None]
    cd = jnp.exp(log_g * c)[:, None, None]
    h, q, k, v, gate = _ret_inproj(src, _src_pstart(src), g[None, :], w_in_all, layer, cos, sin, seq=seq)
    return h, _retention(q, k, v, gate, gn_gain[None, :], intra, qd, kd, cd, batch=batch, seq=seq)


def kernel(x, fox_w_in, fox_b_f, fox_w_out, ret_w_in, ret_gn_gain, ret_w_out, norm_mix, norm_ffn,
           router_group_w, router_group_b, router_expert_w, router_expert_b, expert_w_gu, expert_w_down,
           norm_final):
    batch, seq, d = x.shape
    depth = norm_mix.shape[0]
    src = x.reshape(batch * seq, d)
    for i in range(depth):
        j = i // 2
        if i % 2 == 0:
            h, a = _fox_mixer(src, norm_mix[i], fox_w_in, j, fox_b_f[j], batch=batch, seq=seq)
            w_out_all = fox_w_out
        else:
            h, a = _ret_mixer(src, norm_mix[i], ret_w_in, j, ret_gn_gain[j], batch=batch, seq=seq)
            w_out_all = ret_w_out
        src = _moe(a, w_out_all, j, h, i, norm_ffn[i], router_group_w[i], router_group_b[i], router_expert_w[i],
                   router_expert_b[i], expert_w_gu, expert_w_down, norm_final, final_norm=(i == depth - 1))
    return src.reshape(batch, seq, d)
```

```python
import functools
from typing import NamedTuple

import jax
import jax.numpy as jnp
from jax import lax
from jax.experimental import pallas as pl
from jax.experimental.pallas import tpu as pltpu

F32 = jnp.float32
BF16 = jnp.bfloat16
I32 = jnp.int32

FOX_HEADS = 16
RET_HEADS = 4
RET_CHUNK = 128
N_GROUPS = 4
EXPERTS_PER_GROUP = 8
N_EXPERTS = N_GROUPS * EXPERTS_PER_GROUP
TOP_K = 2
EPS = 1e-6
NEG_INF = -1e30
ROPE_BASE = 10000.0

LANES = 128
SUBLANES = 8
ROUTE_LANE0 = N_GROUPS
VMEM_LIMIT = 56 * 1024 * 1024
MOE_BLOCK = 512
ROW_TILE = 512
WEIGHT_CHUNK = 512

_CONTRACT_LAST = (((1,), (1,)), ((), ()))
_CONTRACT_FIRST = (((0,), (0,)), ((), ()))


def _cparams(*sem):
    return pltpu.CompilerParams(dimension_semantics=sem, vmem_limit_bytes=VMEM_LIMIT)


def _sigmoid(x):
    return 1.0 / (1.0 + jnp.exp(-x))


def _rms(x, g, eps):
    ms = jnp.mean(x * x, axis=-1, keepdims=True)
    return x * lax.rsqrt(ms + eps) * g


def _split3(x):
    hi = x.astype(BF16)
    r = x - hi.astype(F32)
    mid = r.astype(BF16)
    lo = (r - mid.astype(F32)).astype(BF16)
    return hi, mid, lo


def _stage_weight(w_hbm, wb_ref, stage_ref, sem, ncols):
    chunk = stage_ref.shape[2]
    n = ncols // chunk

    def cp(c):
        return pltpu.make_async_copy(w_hbm.at[:, pl.ds(c * chunk, chunk)], stage_ref.at[c % 2], sem.at[c % 2])

    cp(0).start()
    for c in range(n):
        if c + 1 < n:
            cp(c + 1).start()
        cp(c).wait()
        wb_ref[:, c * chunk:(c + 1) * chunk] = stage_ref[c % 2].astype(BF16)


def _rows_to_slabs(ref, x):
    n, d = x.shape
    nslab = d // LANES
    for c in range(nslab):
        ref[pl.ds(c, n, stride=nslab), :] = x[:, c * LANES:(c + 1) * LANES]


def _slabs_to_rows(ref, n, nslab):
    return jnp.concatenate([ref[pl.ds(c, n, stride=nslab), :] for c in range(nslab)], axis=1)


def _slab_rows(ref, row, nrows, nslab):
    return ref.at[pl.ds(pl.multiple_of(row * nslab, nslab), nrows * nslab), :]


class _Pending(NamedTuple):
    meta: jax.Array
    h1: jax.Array
    route: jax.Array
    ys: jax.Array


def _issue_residual_gathers(meta_cur, meta_nxt, ys_ref, buf_ref, gsem, *, tm, sub):
    i = pl.program_id(0)
    nslab = buf_ref.shape[2] // tm

    def issue(meta_ref, step):
        slot = step % 2
        off = (step % sub) * tm

        def body(t, carry):
            for k in range(TOP_K):
                src = meta_ref[TOP_K + k, off + t]
                pltpu.make_async_copy(_slab_rows(ys_ref, src, 1, nslab), _slab_rows(buf_ref.at[slot, k], t, 1, nslab),
                                      gsem.at[slot]).start(priority=k)
            return carry

        lax.fori_loop(0, tm, body, 0, unroll=8)

    @pl.when(i == 0)
    def _():
        issue(meta_cur, i)

    @pl.when(i + 1 < pl.num_programs(0))
    def _():
        issue(meta_nxt, i + 1)


def _gathered_residual(h1_ref, route_ref, buf_ref, gsem):
    tm, d = h1_ref.shape
    nslab = d // LANES
    slot = pl.program_id(0) % 2
    pltpu.make_async_copy(buf_ref.at[slot], buf_ref.at[slot], gsem.at[slot]).wait()
    g0 = route_ref[:, 4:5]
    g1 = route_ref[:, 5:6]
    return h1_ref[...] + (_slabs_to_rows(buf_ref.at[slot, 0], tm, nslab) * g0
                          + _slabs_to_rows(buf_ref.at[slot, 1], tm, nslab) * g1)


def _residual_specs(src, tm, d):
    if not isinstance(src, _Pending):
        return False, 1, [src], [pl.BlockSpec((tm, d), lambda i: (i, 0))], [], [], []
    tile = src.meta.shape[2]
    sub = tile // tm
    last = src.h1.shape[0] // tm - 1
    smem = lambda f: pl.BlockSpec((None, SUBLANES, tile), f, memory_space=pltpu.SMEM)
    in_specs = [smem(lambda i: (i // sub, 0, 0)), smem(lambda i: (jnp.minimum(i + 1, last) // sub, 0, 0)),
                pl.BlockSpec((tm, d), lambda i: (i, 0)), pl.BlockSpec((tm, LANES), lambda i: (i, 0)),
                pl.BlockSpec(memory_space=pl.ANY)]
    scratch = [pltpu.VMEM((2, TOP_K, tm * d // LANES, LANES), F32), pltpu.SemaphoreType.DMA((2,))]
    out_specs = [pl.BlockSpec((tm, d), lambda i: (i, 0))]
    out_shape = [jax.ShapeDtypeStruct(src.h1.shape, F32)]
    return True, sub, [src.meta, src.meta, src.h1, src.route, src.ys], in_specs, out_specs, out_shape, scratch


def _weight_scratch(kdim, ncols):
    return [pltpu.VMEM((kdim, ncols), BF16), pltpu.VMEM((2, kdim, WEIGHT_CHUNK), F32),
            pltpu.SemaphoreType.DMA((2,))]


def _fox_inproj_kernel(*refs, layer, tiles_per_seq, scale, fused, sub):
    i = pl.program_id(0)
    if fused:
        (meta_cur, meta_nxt, h1_ref, route_ref, ys_ref, g_ref, w_hbm, wf_ref, bf_ref, hout_ref, q_ref, kb_ref,
         v_ref, gate_ref, a_ref, carry_ref, wb_ref, stage_ref, sem, buf_ref, gsem) = refs
        _issue_residual_gathers(meta_cur, meta_nxt, ys_ref, buf_ref, gsem, tm=h1_ref.shape[0], sub=sub)
    else:
        (h_ref, g_ref, w_hbm, wf_ref, bf_ref, q_ref, kb_ref, v_ref, gate_ref, a_ref,
         carry_ref, wb_ref, stage_ref, sem) = refs
    tm, d = q_ref.shape

    @pl.when(i == 0)
    def _():
        _stage_weight(w_hbm.at[layer], wb_ref, stage_ref, sem, 4 * d)

    if fused:
        x = _gathered_residual(h1_ref, route_ref, buf_ref, gsem)
        hout_ref[...] = x
    else:
        x = h_ref[...]
    xn = _rms(x, g_ref[...], EPS).astype(BF16)

    q = jnp.dot(xn, wb_ref[:, 0:d], preferred_element_type=F32)
    q_ref[...] = (q * scale).astype(BF16)
    v_ref[...] = jnp.dot(xn, wb_ref[:, 2 * d:3 * d], preferred_element_type=F32).astype(BF16)
    gate_ref[...] = jnp.dot(xn, wb_ref[:, 3 * d:4 * d], preferred_element_type=F32)

    f = jnp.dot(xn, wf_ref[...], preferred_element_type=F32) + bf_ref[...]
    ls = jnp.minimum(f, 0.0) - jnp.log1p(jnp.exp(-jnp.abs(f)))

    @pl.when(i % tiles_per_seq == 0)
    def _():
        carry_ref[...] = jnp.zeros_like(carry_ref)

    row = lax.broadcasted_iota(I32, (tm, tm), 0)
    col = lax.broadcasted_iota(I32, (tm, tm), 1)
    tri = jnp.where(col <= row, 1.0, 0.0).astype(BF16)
    hi, mid, lo = _split3(ls)
    cum = (jnp.dot(tri, hi, preferred_element_type=F32) + jnp.dot(tri, mid, preferred_element_type=F32)
           + jnp.dot(tri, lo, preferred_element_type=F32)) + carry_ref[...]
    carry_ref[...] = cum[tm - 1:tm, :]

    c_hi, c_mid, c_lo = (c.astype(F32) for c in _split3(cum))
    lane = lax.broadcasted_iota(I32, (tm, LANES), 1)
    a_ref[...] = jnp.where(lane < 16, c_hi, jnp.where(lane < 32, c_mid, jnp.where(
        lane < 48, c_lo, jnp.where(lane < 96, 1.0, 0.0)))).astype(BF16)
    bk = jnp.where(lane < 48, 1.0, jnp.where(lane < 64, -c_hi, jnp.where(
        lane < 80, -c_mid, jnp.where(lane < 96, -c_lo, 0.0)))).astype(BF16)

    k = jnp.dot(xn, wb_ref[:, d:2 * d], preferred_element_type=F32).astype(BF16)
    for p in range(d // LANES):
        kb_ref[:, 2 * p * LANES:(2 * p + 1) * LANES] = k[:, p * LANES:(p + 1) * LANES]
        kb_ref[:, (2 * p + 1) * LANES:(2 * p + 2) * LANES] = bk


def _fox_inproj(src, g, w_all, layer, wf, bfr, *, seq):
    t, d = src.h1.shape if isinstance(src, _Pending) else src.shape
    tm = min(ROW_TILE, seq)
    scale = float((d // FOX_HEADS) ** -0.5)
    fused, sub, operands, in_specs, out_specs, out_shape, scratch = _residual_specs(src, tm, d)
    kern = functools.partial(_fox_inproj_kernel, layer=layer, tiles_per_seq=seq // tm, scale=scale,
                             fused=fused, sub=sub)
    row = lambda i: (i, 0)
    const = lambda i: (0, 0)
    grid_spec = pltpu.PrefetchScalarGridSpec(
        num_scalar_prefetch=0,
        grid=(t // tm,),
        in_specs=in_specs + [pl.BlockSpec((1, d), const), pl.BlockSpec(memory_space=pl.ANY),
                             pl.BlockSpec((d, LANES), const), pl.BlockSpec((1, LANES), const)],
        out_specs=out_specs + [pl.BlockSpec((tm, d), row), pl.BlockSpec((tm, 2 * d), row),
                               pl.BlockSpec((tm, d), row), pl.BlockSpec((tm, d), row),
                               pl.BlockSpec((tm, LANES), row)],
        scratch_shapes=[pltpu.VMEM((1, LANES), F32)] + _weight_scratch(d, 4 * d) + scratch,
    )
    outs = pl.pallas_call(
        kern,
        grid_spec=grid_spec,
        out_shape=out_shape + [jax.ShapeDtypeStruct((t, d), BF16), jax.ShapeDtypeStruct((t, 2 * d), BF16),
                               jax.ShapeDtypeStruct((t, d), BF16), jax.ShapeDtypeStruct((t, d), F32),
                               jax.ShapeDtypeStruct((t, LANES), BF16)],
        compiler_params=_cparams("arbitrary"),
        name="fox_inproj",
    )(*operands, g, w_all, wf, bfr)
    return (outs if fused else [src] + list(outs))


def _fox_attn_kernel(q_ref, a_ref, kb_ref, v_ref, gate_ref, o_ref, *, tq, dh, depth):
    seq = q_ref.shape[0]
    pair = pl.program_id(1)
    lane = lax.broadcasted_iota(I32, (tq, LANES), 1)
    krow = lax.broadcasted_iota(I32, (tq, tq), 0)
    qcol = lax.broadcasted_iota(I32, (tq, tq), 1)
    causal = krow <= qcol
    heads_per_block = LANES // dh
    qmask = [jnp.where((lane >= hh * dh) & (lane < (hh + 1) * dh), 1.0, 0.0).astype(BF16)
             for hh in range(heads_per_block)]
    amask = [jnp.where((lane < 96) & ((lane & 15) == pair * heads_per_block + hh), 1.0, 0.0).astype(BF16)
             for hh in range(heads_per_block)]
    vt = jnp.transpose(v_ref[...])
    chains = [(i, hh) for i in range(seq // tq) for hh in range(heads_per_block)]

    def scores(i, hh):
        off = i * tq
        qa = jnp.concatenate([q_ref[off:off + tq, :] * qmask[hh], a_ref[off:off + tq, :] * amask[hh]], axis=1)
        return lax.dot_general(kb_ref[0:off + tq, :], qa, _CONTRACT_LAST, preferred_element_type=F32)

    def finish(i, hh, st):
        off = i * tq
        sd = jnp.where(causal, st[off:off + tq, :], NEG_INF)
        st = jnp.concatenate([st[0:off, :], sd], axis=0) if off > 0 else sd
        m = jnp.max(st, axis=0, keepdims=True)
        e = jnp.exp(st - m)
        p = (e * (1.0 / jnp.sum(e, axis=0, keepdims=True))).astype(BF16)
        return jnp.dot(vt[hh * dh:(hh + 1) * dh, 0:off + tq], p, preferred_element_type=F32)

    outs = {}
    queue = [scores(*c) for c in chains[:depth]]
    for n, (i, hh) in enumerate(chains):
        cur = queue.pop(0)
        if n + depth < len(chains):
            queue.append(scores(*chains[n + depth]))
        outs[hh] = finish(i, hh, cur)
        if hh == heads_per_block - 1:
            off = i * tq
            o = jnp.transpose(jnp.concatenate([outs[h2] for h2 in range(heads_per_block)], axis=0))
            o_ref[off:off + tq, :] = (o * _sigmoid(gate_ref[off:off + tq, :])).astype(BF16)


def _fox_attn(q, a, kb, v, gate, *, batch, seq, tq=256, depth=3):
    t, d = q.shape
    dh = d // FOX_HEADS
    pairs = d // LANES
    kern = functools.partial(_fox_attn_kernel, tq=tq, dh=dh, depth=depth)
    blk = lambda b, p: (b, p)
    return pl.pallas_call(
        kern,
        grid=(batch, pairs),
        in_specs=[pl.BlockSpec((seq, LANES), blk), pl.BlockSpec((seq, LANES), lambda b, p: (b, 0)),
                  pl.BlockSpec((seq, 2 * LANES), blk), pl.BlockSpec((seq, LANES), blk),
                  pl.BlockSpec((seq, LANES), blk)],
        out_specs=pl.BlockSpec((seq, LANES), blk),
        out_shape=jax.ShapeDtypeStruct((t, d), BF16),
        compiler_params=_cparams("arbitrary", "arbitrary"),
        name="fox_attn",
    )(q, a, kb, v, gate)


def _ret_inproj_kernel(*refs, layer, kscale, fused, sub):
    if fused:
        (meta_cur, meta_nxt, h1_ref, route_ref, ys_ref, g_ref, w_hbm, cos_ref, sin_ref, hout_ref, q_ref, k_ref,
         v_ref, gate_ref, wb_ref, stage_ref, sem, buf_ref, gsem) = refs
        _issue_residual_gathers(meta_cur, meta_nxt, ys_ref, buf_ref, gsem, tm=h1_ref.shape[0], sub=sub)
    else:
        h_ref, g_ref, w_hbm, cos_ref, sin_ref, q_ref, k_ref, v_ref, gate_ref, wb_ref, stage_ref, sem = refs
    tm, d = q_ref.shape

    @pl.when(pl.program_id(0) == 0)
    def _():
        _stage_weight(w_hbm.at[layer], wb_ref, stage_ref, sem, 6 * d)

    if fused:
        x = _gathered_residual(h1_ref, route_ref, buf_ref, gsem)
        hout_ref[...] = x
    else:
        x = h_ref[...]
    xn = _rms(x, g_ref[...], EPS).astype(BF16)
    cos = cos_ref[...]
    sin = sin_ref[...]
    dk = d // RET_HEADS
    half = dk // 2

    def rot(t, h):
        t1 = t[:, h * dk:h * dk + half]
        t2 = t[:, h * dk + half:(h + 1) * dk]
        return t1 * cos - t2 * sin, t1 * sin + t2 * cos

    q = jnp.dot(xn, wb_ref[:, 0:d], preferred_element_type=F32)
    k = jnp.dot(xn, wb_ref[:, d:2 * d], preferred_element_type=F32)
    for h in range(RET_HEADS):
        q1, q2 = rot(q, h)
        q_ref[:, h * dk:h * dk + half] = q1.astype(BF16)
        q_ref[:, h * dk + half:(h + 1) * dk] = q2.astype(BF16)
        k1, k2 = rot(k, h)
        k_ref[:, h * dk:h * dk + half] = k1 * kscale
        k_ref[:, h * dk + half:(h + 1) * dk] = k2 * kscale
    v_ref[...] = jnp.dot(xn, wb_ref[:, 2 * d:4 * d], preferred_element_type=F32).astype(BF16)
    gate_ref[...] = jnp.dot(xn, wb_ref[:, 4 * d:6 * d], preferred_element_type=F32)


def _ret_inproj(src, g, w_all, layer, cos, sin, *, seq, tm=256):
    t, d = src.h1.shape if isinstance(src, _Pending) else src.shape
    tm = min(tm, seq)
    half = d // RET_HEADS // 2
    fused, sub, operands, in_specs, out_specs, out_shape, scratch = _residual_specs(src, tm, d)
    kern = functools.partial(_ret_inproj_kernel, layer=layer, kscale=float((d // RET_HEADS) ** -0.5),
                             fused=fused, sub=sub)
    row = lambda i: (i, 0)
    const = lambda i: (0, 0)
    pos = lambda i: (i % (seq // tm), 0)
    grid_spec = pltpu.PrefetchScalarGridSpec(
        num_scalar_prefetch=0,
        grid=(t // tm,),
        in_specs=in_specs + [pl.BlockSpec((1, d), const), pl.BlockSpec(memory_space=pl.ANY),
                             pl.BlockSpec((tm, half), pos), pl.BlockSpec((tm, half), pos)],
        out_specs=out_specs + [pl.BlockSpec((tm, d), row), pl.BlockSpec((tm, d), row),
                               pl.BlockSpec((tm, 2 * d), row), pl.BlockSpec((tm, 2 * d), row)],
        scratch_shapes=_weight_scratch(d, 6 * d) + scratch,
    )
    outs = pl.pallas_call(
        kern,
        grid_spec=grid_spec,
        out_shape=out_shape + [jax.ShapeDtypeStruct((t, d), BF16), jax.ShapeDtypeStruct((t, d), F32),
                               jax.ShapeDtypeStruct((t, 2 * d), BF16), jax.ShapeDtypeStruct((t, 2 * d), F32)],
        compiler_params=_cparams("arbitrary"),
        name="ret_inproj",
    )(*operands, g, w_all, cos, sin)
    return (outs if fused else [src] + list(outs))


def _retention_kernel(q_ref, k_ref, v_ref, gate_ref, gain_ref, intra_ref, qd_ref, kd_ref, cd_ref,
                      y_ref, r_ref, *, chunk):
    seq = q_ref.shape[0]
    r_ref[...] = jnp.zeros_like(r_ref)

    def body(c):
        off = c * chunk
        qn = q_ref[pl.ds(off, chunk), :]
        k32 = k_ref[pl.ds(off, chunk), :]
        vn = v_ref[pl.ds(off, chunk), :]
        s = lax.dot_general(qn, k32.astype(BF16), _CONTRACT_LAST, preferred_element_type=F32) * intra_ref[...]
        o = jnp.dot(s.astype(BF16), vn, preferred_element_type=F32)
        r = r_ref[...]
        o = o + jnp.dot(qn, r.astype(BF16), preferred_element_type=F32) * qd_ref[...]
        kdec = (k32 * kd_ref[...]).astype(BF16)
        r_ref[...] = r * cd_ref[...] + lax.dot_general(kdec, vn, _CONTRACT_FIRST, preferred_element_type=F32)
        mu = jnp.mean(o, axis=-1, keepdims=True)
        dlt = o - mu
        var = jnp.mean(dlt * dlt, axis=-1, keepdims=True)
        on = dlt * lax.rsqrt(var + EPS) * gain_ref[...]
        g = gate_ref[pl.ds(off, chunk), :]
        y_ref[pl.ds(off, chunk), :] = ((g * _sigmoid(g)) * on).astype(BF16)

    for c in range(seq // chunk):
        body(c)


def _retention(q, k, v, gate, gain, intra, qd, kd, cd, *, batch, seq):
    t, d = q.shape
    dk = d // RET_HEADS
    dv = v.shape[1] // RET_HEADS
    c = RET_CHUNK
    kern = functools.partial(_retention_kernel, chunk=c)
    blk = lambda b, h: (b, h)
    hd = lambda b, h: (h, 0, 0)
    return pl.pallas_call(
        kern,
        grid=(batch, RET_HEADS),
        in_specs=[pl.BlockSpec((seq, dk), blk), pl.BlockSpec((seq, dk), blk), pl.BlockSpec((seq, dv), blk),
                  pl.BlockSpec((seq, dv), blk), pl.BlockSpec((1, dv), lambda b, h: (0, h)),
                  pl.BlockSpec((None, c, c), hd), pl.BlockSpec((None, c, 1), hd),
                  pl.BlockSpec((None, c, 1), hd), pl.BlockSpec((None, 1, 1), hd)],
        out_specs=pl.BlockSpec((seq, dv), blk),
        out_shape=jax.ShapeDtypeStruct((t, v.shape[1]), BF16),
        scratch_shapes=[pltpu.VMEM((dk, dv), F32)],
        compiler_params=_cparams("arbitrary", "arbitrary"),
        name="retention",
    )(q, k, v, gate, gain, intra, qd, kd, cd)


def _outproj_router_kernel(a_ref, w_hbm, h_ref, g_ref, wrh_ref, wrl_ref, br_ref,
                           h1_ref, hn_ref, route_ref, meta_ref, cnt_ref, wb_ref, stage_ref, sem, *, layer):
    i = pl.program_id(0)
    tm = a_ref.shape[0]
    d = h_ref.shape[1]

    @pl.when(i == 0)
    def _():
        cnt_ref[...] = jnp.zeros_like(cnt_ref)
        _stage_weight(w_hbm.at[layer], wb_ref, stage_ref, sem, d)

    h1 = h_ref[...] + jnp.dot(a_ref[...], wb_ref[...], preferred_element_type=F32)
    h1_ref[...] = h1
    hn = _rms(h1, g_ref[...], EPS)
    _rows_to_slabs(hn_ref, hn)

    hh = hn.astype(BF16)
    hl = (hn - hh.astype(F32)).astype(BF16)
    lg = (jnp.dot(hh, wrh_ref[...], preferred_element_type=F32)
          + jnp.dot(hh, wrl_ref[...], preferred_element_type=F32)
          + jnp.dot(hl, wrh_ref[...], preferred_element_type=F32)) + br_ref[...]

    lane = lax.broadcasted_iota(I32, (tm, LANES), 1)
    lanef = lane.astype(F32)
    big = float(LANES)

    def softmax_masked(mask):
        mx = jnp.max(jnp.where(mask, lg, -jnp.inf), axis=1, keepdims=True)
        e = jnp.where(mask, jnp.exp(lg - mx), 0.0)
        return e / jnp.sum(e, axis=1, keepdims=True)

    def top1(p, mask):
        best = jnp.max(jnp.where(mask, p, -1.0), axis=1, keepdims=True)
        idx = jnp.min(jnp.where(mask & (p == best), lanef, big), axis=1, keepdims=True)
        return best, idx

    gmask = lane < N_GROUPS
    pg, gi = top1(softmax_masked(gmask), gmask)
    lo = ROUTE_LANE0 + gi * EXPERTS_PER_GROUP
    emask = (lanef >= lo) & (lanef < lo + EXPERTS_PER_GROUP)
    pe = softmax_masked(emask)
    p1, i1 = top1(pe, emask)
    p2, i2 = top1(pe, emask & (lanef != i1))
    den = p1 + p2
    g1 = pg * p1 / den
    g2 = pg * p2 / den

    oh1 = lanef == i1
    oh2 = lanef == i2
    both = jnp.where(oh1 | oh2, 1.0, 0.0)
    row = lax.broadcasted_iota(I32, (tm, tm), 0)
    col = lax.broadcasted_iota(I32, (tm, tm), 1)
    strict = jnp.where(col < row, 1.0, 0.0).astype(BF16)
    before = jnp.dot(strict, both.astype(BF16), preferred_element_type=F32) + cnt_ref[0:1, :]
    r1 = jnp.sum(jnp.where(oh1, before, 0.0), axis=1, keepdims=True)
    r2 = jnp.sum(jnp.where(oh2, before, 0.0), axis=1, keepdims=True)
    cnt_ref[0:1, :] = cnt_ref[0:1, :] + jnp.sum(both, axis=0, keepdims=True)

    e1 = i1 - ROUTE_LANE0
    e2 = i2 - ROUTE_LANE0
    route = jnp.where(lane == 0, e1, jnp.where(lane == 1, e2, jnp.where(
        lane == 2, r1, jnp.where(lane == 3, r2, jnp.where(lane == 4, g1, jnp.where(lane == 5, g2, 0.0))))))
    route_ref[...] = route
    meta_ref[...] = jnp.transpose(route)[0:SUBLANES, :].astype(I32)


def _outproj_router(a, w_all, layer, h, g, wrh, wrl, br, *, tm):
    t, kdim = a.shape
    d = h.shape[1]
    row = lambda i: (i, 0)
    const = lambda i: (0, 0)
    kern = functools.partial(_outproj_router_kernel, layer=layer)
    return pl.pallas_call(
        kern,
        grid=(t // tm,),
        in_specs=[pl.BlockSpec((tm, kdim), row), pl.BlockSpec(memory_space=pl.ANY), pl.BlockSpec((tm, d), row),
                  pl.BlockSpec((1, d), const), pl.BlockSpec((d, LANES), const), pl.BlockSpec((d, LANES), const),
                  pl.BlockSpec((1, LANES), const)],
        out_specs=[pl.BlockSpec((tm, d), row), pl.BlockSpec((tm * d // LANES, LANES), row),
                   pl.BlockSpec((tm, LANES), row), pl.BlockSpec((None, SUBLANES, tm), lambda i: (i, 0, 0)),
                   pl.BlockSpec((SUBLANES, LANES), const)],
        out_shape=[jax.ShapeDtypeStruct((t, d), F32), jax.ShapeDtypeStruct((t * d // LANES, LANES), F32),
                   jax.ShapeDtypeStruct((t, LANES), F32), jax.ShapeDtypeStruct((t // tm, SUBLANES, tm), I32),
                   jax.ShapeDtypeStruct((SUBLANES, LANES), F32)],
        scratch_shapes=_weight_scratch(kdim, d),
        compiler_params=_cparams("arbitrary"),
        name="outproj_router",
    )(a, w_all, h, g, wrh, wrl, br)


def _pad_copies(counts_ref, pstart_ref, zero_ref, xs_ref, sem, *, bm):
    nslab = SUBLANES
    out = []
    for e in range(N_EXPERTS):
        cnt = counts_ref[e]
        npad = (bm - (cnt & (bm - 1))) & (bm - 1)
        pos = pstart_ref[e] + cnt
        bit = 1
        while bit < bm:
            out.append(((npad & bit) != 0, pltpu.make_async_copy(
                _slab_rows(zero_ref, 0, bit, nslab), _slab_rows(xs_ref, pos, bit, nslab), sem)))
            pos = pos + (npad & bit)
            bit *= 2
    return out


def _dispatch_kernel(counts_ref, pstart_ref, nused_ref, meta_ref, hn_ref, xs_ref, zero_ref, sem, psem, *, bm):
    nslab = SUBLANES
    tm = hn_ref.shape[0] // nslab
    half = bm // 2

    @pl.when(pl.program_id(0) == 0)
    def _():
        zero_ref[...] = jnp.zeros_like(zero_ref)
        pads = _pad_copies(counts_ref, pstart_ref, zero_ref, xs_ref, psem, bm=bm)
        for pred, cp in pads:
            pl.when(pred)(cp.start)

        def tail_copy(j):
            return pltpu.make_async_copy(zero_ref, _slab_rows(xs_ref, j * half, half, nslab), psem)

        first = nused_ref[0] * 2
        last = xs_ref.shape[0] // (half * nslab)
        lax.fori_loop(first, last, lambda j, c: (tail_copy(j).start(), c)[1], 0)
        for pred, cp in pads:
            pl.when(pred)(cp.wait)
        lax.fori_loop(first, last, lambda j, c: (tail_copy(j).wait(), c)[1], 0)

    def row_copy(t, k):
        dst = meta_ref[TOP_K + k, t]
        return pltpu.make_async_copy(_slab_rows(hn_ref, t, 1, nslab), _slab_rows(xs_ref, dst, 1, nslab), sem)

    def issue(t, carry):
        for k in range(TOP_K):
            row_copy(t, k).start(priority=k)
        return carry

    lax.fori_loop(0, tm, issue, 0, unroll=8)

    def drain(t, carry):
        for k in range(TOP_K):
            row_copy(t, k).wait()
        return carry

    lax.fori_loop(0, tm, drain, 0, unroll=8)


def _dispatch(counts, pstart, n_used, meta, hn, n_rows, *, bm):
    nslab = SUBLANES
    t = hn.shape[0] // nslab
    tm = meta.shape[2]
    kern = functools.partial(_dispatch_kernel, bm=bm)
    grid_spec = pltpu.PrefetchScalarGridSpec(
        num_scalar_prefetch=3,
        grid=(t // tm,),
        in_specs=[pl.BlockSpec((None, SUBLANES, tm), lambda i, *_: (i, 0, 0), memory_space=pltpu.SMEM),
                  pl.BlockSpec((tm * nslab, LANES), lambda i, *_: (i, 0))],
        out_specs=pl.BlockSpec(memory_space=pl.ANY),
        scratch_shapes=[pltpu.VMEM((bm // 2 * nslab, LANES), F32), pltpu.SemaphoreType.DMA(()),
                        pltpu.SemaphoreType.DMA(())],
    )
    return pl.pallas_call(
        kern,
        grid_spec=grid_spec,
        out_shape=jax.ShapeDtypeStruct((n_rows * nslab, LANES), F32),
        compiler_params=_cparams("arbitrary"),
        name="moe_dispatch",
    )(counts, pstart, n_used, meta, hn)


def _experts_kernel(be_ref, nu_ref, x_ref, wgu_ref, wd_ref, y_ref, wgu_b, wd_b):
    i = pl.program_id(0)
    d = wgu_b.shape[0]
    bm = x_ref.shape[0] * LANES // d
    prev = be_ref[jnp.maximum(i - 1, 0)]

    @pl.when((i < nu_ref[0]) & ((i == 0) | (be_ref[i] != prev)))
    def _():
        wgu_b[...] = wgu_ref[...].astype(BF16)
        wd_b[...] = wd_ref[...].astype(BF16)

    @pl.when(i < nu_ref[0])
    def _():
        x = _slabs_to_rows(x_ref, bm, d // LANES).astype(BF16)
        hcat = jnp.dot(x, wgu_b[...], preferred_element_type=F32)
        ff = hcat.shape[1] // 2
        a = hcat[:, :ff]
        b = hcat[:, ff:]
        act = ((a * _sigmoid(a)) * b).astype(BF16)
        _rows_to_slabs(y_ref, jnp.dot(act, wd_b[...], preferred_element_type=F32))

    @pl.when(i >= nu_ref[0])
    def _():
        y_ref[...] = jnp.zeros_like(y_ref)


def _experts(block_e, n_used, xs, wgu_all, wd_all, layer, *, bm):
    d = wgu_all.shape[2]
    nslab = d // LANES
    n_rows = xs.shape[0] // nslab
    ff2 = wgu_all.shape[3]
    grid_spec = pltpu.PrefetchScalarGridSpec(
        num_scalar_prefetch=2,
        grid=(n_rows // bm,),
        in_specs=[pl.BlockSpec((bm * nslab, LANES), lambda i, be, nu: (i, 0)),
                  pl.BlockSpec((None, None, d, ff2), lambda i, be, nu: (layer, be[i], 0, 0)),
                  pl.BlockSpec((None, None, ff2 // 2, d), lambda i, be, nu: (layer, be[i], 0, 0))],
        out_specs=pl.BlockSpec((bm * nslab, LANES), lambda i, be, nu: (i, 0)),
        scratch_shapes=[pltpu.VMEM((d, ff2), BF16), pltpu.VMEM((ff2 // 2, d), BF16)],
    )
    return pl.pallas_call(
        _experts_kernel,
        grid_spec=grid_spec,
        out_shape=jax.ShapeDtypeStruct((n_rows * nslab, LANES), F32),
        compiler_params=_cparams("arbitrary"),
        name="moe_experts",
    )(block_e, n_used, xs, wgu_all, wd_all)


def _combine_kernel(meta_ref, h1_ref, route_ref, gfin_ref, ys_ref, o_ref, buf_ref, sem, *, final_norm):
    tm, d = h1_ref.shape
    nslab = d // LANES

    def row_copy(t, k):
        src = meta_ref[TOP_K + k, t]
        return pltpu.make_async_copy(_slab_rows(ys_ref, src, 1, nslab), _slab_rows(buf_ref.at[k], t, 1, nslab), sem)

    def issue(t, carry):
        for k in range(TOP_K):
            row_copy(t, k).start(priority=k)
        return carry

    lax.fori_loop(0, tm, issue, 0, unroll=8)

    def drain(t, carry):
        for k in range(TOP_K):
            row_copy(t, k).wait()
        return carry

    lax.fori_loop(0, tm, drain, 0, unroll=8)

    g0 = route_ref[:, 4:5]
    g1 = route_ref[:, 5:6]
    out = h1_ref[...] + (_slabs_to_rows(buf_ref.at[0], tm, nslab) * g0 + _slabs_to_rows(buf_ref.at[1], tm, nslab) * g1)
    if final_norm:
        out = _rms(out, gfin_ref[...], EPS)
    o_ref[...] = out


def _combine(meta, h1, route, gfin, ys, *, final_norm):
    t, d = h1.shape
    tm = meta.shape[2]
    kern = functools.partial(_combine_kernel, final_norm=final_norm)
    grid_spec = pltpu.PrefetchScalarGridSpec(
        num_scalar_prefetch=0,
        grid=(t // tm,),
        in_specs=[pl.BlockSpec((None, SUBLANES, tm), lambda i: (i, 0, 0), memory_space=pltpu.SMEM),
                  pl.BlockSpec((tm, d), lambda i: (i, 0)),
                  pl.BlockSpec((tm, LANES), lambda i: (i, 0)),
                  pl.BlockSpec((1, d), lambda i: (0, 0)),
                  pl.BlockSpec(memory_space=pl.ANY)],
        out_specs=pl.BlockSpec((tm, d), lambda i: (i, 0)),
        scratch_shapes=[pltpu.VMEM((TOP_K, tm * d // LANES, LANES), F32), pltpu.SemaphoreType.DMA(())],
    )
    return pl.pallas_call(
        kern,
        grid_spec=grid_spec,
        out_shape=jax.ShapeDtypeStruct((t, d), F32),
        compiler_params=_cparams("arbitrary"),
        name="moe_combine",
    )(meta, h1, route, gfin, ys)


def _moe(a, w_out_all, mixer_layer, h, layer, g_ffn, wr_g, br_g, wr_e, br_e, w_gu_all, w_down_all, gfin,
         *, final_norm):
    t, d = h.shape
    bm = MOE_BLOCK
    tm = min(ROW_TILE, t)
    wr = jnp.concatenate([wr_g, wr_e, jnp.zeros((d, LANES - N_GROUPS - N_EXPERTS), F32)], axis=1)
    wrh = wr.astype(BF16)
    wrl = (wr - wrh.astype(F32)).astype(BF16)
    br = jnp.concatenate([br_g, br_e, jnp.zeros((LANES - N_GROUPS - N_EXPERTS,), F32)])[None, :]
    h1, hn, route, meta, cnt = _outproj_router(a, w_out_all, mixer_layer, h, g_ffn[None, :], wrh, wrl, br, tm=tm)

    counts = cnt[0, ROUTE_LANE0:ROUTE_LANE0 + N_EXPERTS].astype(I32)
    nb = (counts + bm - 1) // bm
    nb_end = jnp.cumsum(nb)
    pstart = ((nb_end - nb) * bm).astype(I32)
    n_blocks = (t * TOP_K + N_EXPERTS * (bm - 1)) // bm
    j = jnp.arange(n_blocks, dtype=I32)
    block_e = jnp.minimum(jnp.sum((nb_end[None, :] <= j[:, None]).astype(I32), axis=1), N_EXPERTS - 1)
    n_used = nb_end[-1:].astype(I32)

    eid = meta[:, 0:TOP_K, :]
    base = jnp.sum(jnp.where(eid[..., None] == jnp.arange(N_EXPERTS, dtype=I32), pstart, 0), axis=-1)
    meta = jnp.concatenate([eid, meta[:, TOP_K:2 * TOP_K, :] + base, meta[:, 2 * TOP_K:, :]], axis=1)

    xs = _dispatch(counts, pstart, n_used, meta, hn, n_blocks * bm, bm=bm)
    ys = _experts(block_e, n_used, xs, w_gu_all, w_down_all, layer, bm=bm)
    if final_norm:
        return _combine(meta, h1, route, gfin[None, :], ys, final_norm=True)
    return _Pending(meta, h1, route, ys)


def _fox_mixer(src, g, w_in_all, layer, b_f, *, batch, seq):
    d = w_in_all.shape[1]
    nh = FOX_HEADS
    wf = w_in_all[layer, :, 4 * d:4 * d + nh]
    wf = jnp.concatenate([wf] * 6 + [jnp.zeros((d, LANES - 6 * nh), F32)], axis=1).astype(BF16)
    bfr = jnp.concatenate([b_f] * 6 + [jnp.zeros((LANES - 6 * nh,), F32)])[None, :]
    h, q, kb, v, gate, a = _fox_inproj(src, g[None, :], w_in_all, layer, wf, bfr, seq=seq)
    return h, _fox_attn(q, a, kb, v, gate, batch=batch, seq=seq)


def _ret_mixer(src, g, w_in_all, layer, gn_gain, *, batch, seq):
    d = w_in_all.shape[1]
    nh = RET_HEADS
    dk = d // nh
    c = RET_CHUNK
    inv = 1.0 / (ROPE_BASE ** jnp.linspace(0.0, 1.0, dk // 2, dtype=F32))
    ang = jnp.arange(seq).astype(F32)[:, None] * inv[None, :]
    cos, sin = jnp.cos(ang), jnp.sin(ang)
    log_g = jnp.log(1.0 - 2.0 ** (-5.0 - jnp.arange(nh, dtype=F32)))
    idx = jnp.arange(c, dtype=F32)
    rel = idx[:, None] - idx[None, :]
    intra = jnp.where(rel >= 0, jnp.exp(log_g[:, None, None] * jnp.maximum(rel, 0.0)), 0.0)
    qd = jnp.exp(log_g[:, None] * (idx + 1.0))[:, :, None]
    kd = jnp.exp(log_g[:, None] * (c - 1.0 - idx))[:, :, None]
    cd = jnp.exp(log_g * c)[:, None, None]
    h, q, k, v, gate = _ret_inproj(src, g[None, :], w_in_all, layer, cos, sin, seq=seq)
    return h, _retention(q, k, v, gate, gn_gain[None, :], intra, qd, kd, cd, batch=batch, seq=seq)


def kernel(x, fox_w_in, fox_b_f, fox_w_out, ret_w_in, ret_gn_gain, ret_w_out, norm_mix, norm_ffn,
           router_group_w, router_group_b, router_expert_w, router_expert_b, expert_w_gu, expert_w_down,
           norm_final):
    batch, seq, d = x.shape
    depth = norm_mix.shape[0]
    src = x.reshape(batch * seq, d)
    for i in range(depth):
        j = i // 2
        if i % 2 == 0:
            h, a = _fox_mixer(src, norm_mix[i], fox_w_in, j, fox_b_f[j], batch=batch, seq=seq)
            w_out_all = fox_w_out
        else:
            h, a = _ret_mixer(src, norm_mix[i], ret_w_in, j, ret_gn_gain[j], batch=batch, seq=seq)
            w_out_all = ret_w_out
        src = _moe(a, w_out_all, j, h, i, norm_ffn[i], router_group_w[i], router_group_b[i], router_expert_w[i],
                   router_expert_b[i], expert_w_gu, expert_w_down, norm_final, final_norm=(i == depth - 1))
    return src.reshape(batch, seq, d)
```

```python
import functools
from typing import NamedTuple

import jax
import jax.numpy as jnp
from jax import lax
from jax.experimental import pallas as pl
from jax.experimental.pallas import tpu as pltpu

F32 = jnp.float32
BF16 = jnp.bfloat16
I32 = jnp.int32

FOX_HEADS = 16
RET_HEADS = 4
RET_CHUNK = 128
N_GROUPS = 4
EXPERTS_PER_GROUP = 8
N_EXPERTS = N_GROUPS * EXPERTS_PER_GROUP
TOP_K = 2
EPS = 1e-6
NEG_INF = -1e30
ROPE_BASE = 10000.0

LANES = 128
SUBLANES = 8
ROUTE_LANE0 = N_GROUPS
VMEM_LIMIT = 56 * 1024 * 1024
MOE_BLOCK = 512
ROW_TILE = 512
WEIGHT_CHUNK = 512

_CONTRACT_LAST = (((1,), (1,)), ((), ()))
_CONTRACT_FIRST = (((0,), (0,)), ((), ()))


def _cparams(*sem):
    return pltpu.CompilerParams(dimension_semantics=sem, vmem_limit_bytes=VMEM_LIMIT)


def _sigmoid(x):
    return 1.0 / (1.0 + jnp.exp(-x))


def _rms(x, g, eps):
    ms = jnp.mean(x * x, axis=-1, keepdims=True)
    return x * lax.rsqrt(ms + eps) * g


def _split3(x):
    hi = x.astype(BF16)
    r = x - hi.astype(F32)
    mid = r.astype(BF16)
    lo = (r - mid.astype(F32)).astype(BF16)
    return hi, mid, lo


def _stage_weight(w_hbm, wb_ref, stage_ref, sem, ncols):
    chunk = stage_ref.shape[2]
    n = ncols // chunk

    def cp(c):
        return pltpu.make_async_copy(w_hbm.at[:, pl.ds(c * chunk, chunk)], stage_ref.at[c % 2], sem.at[c % 2])

    cp(0).start()
    for c in range(n):
        if c + 1 < n:
            cp(c + 1).start()
        cp(c).wait()
        wb_ref[:, c * chunk:(c + 1) * chunk] = stage_ref[c % 2].astype(BF16)


def _rows_to_slabs(ref, x):
    n, d = x.shape
    nslab = d // LANES
    for c in range(nslab):
        ref[pl.ds(c, n, stride=nslab), :] = x[:, c * LANES:(c + 1) * LANES]


def _slabs_to_rows(ref, n, nslab):
    return jnp.concatenate([ref[pl.ds(c, n, stride=nslab), :] for c in range(nslab)], axis=1)


def _slab_rows(ref, row, nrows, nslab):
    return ref.at[pl.ds(pl.multiple_of(row * nslab, nslab), nrows * nslab), :]


def _slab_at(ref, first, nslab):
    return ref.at[pl.ds(pl.multiple_of(first, nslab), nslab), :]


class _Pending(NamedTuple):
    meta: jax.Array
    h1: jax.Array
    route: jax.Array
    ys: jax.Array


def _issue_residual_gathers(meta_cur, meta_nxt, ys_ref, buf_ref, gsem, *, tm, sub):
    i = pl.program_id(0)
    nslab = buf_ref.shape[2] // tm

    def issue(meta_ref, step):
        slot = step % 2
        off = (step % sub) * tm

        def body(t, carry):
            for k in range(TOP_K):
                src = meta_ref[TOP_K + k, off + t]
                pltpu.make_async_copy(_slab_at(ys_ref, src, nslab), _slab_rows(buf_ref.at[slot, k], t, 1, nslab),
                                      gsem.at[slot]).start(priority=k)
            return carry

        lax.fori_loop(0, tm, body, 0, unroll=8)

    @pl.when(i == 0)
    def _():
        issue(meta_cur, i)

    @pl.when(i + 1 < pl.num_programs(0))
    def _():
        issue(meta_nxt, i + 1)


def _gathered_residual(h1_ref, route_ref, buf_ref, gsem):
    tm, d = h1_ref.shape
    nslab = d // LANES
    slot = pl.program_id(0) % 2
    pltpu.make_async_copy(buf_ref.at[slot], buf_ref.at[slot], gsem.at[slot]).wait()
    g0 = route_ref[:, 4:5]
    g1 = route_ref[:, 5:6]
    return h1_ref[...] + (_slabs_to_rows(buf_ref.at[slot, 0], tm, nslab) * g0
                          + _slabs_to_rows(buf_ref.at[slot, 1], tm, nslab) * g1)


def _residual_specs(src, tm, d):
    if not isinstance(src, _Pending):
        return False, 1, [src], [pl.BlockSpec((tm, d), lambda i: (i, 0))], [], [], []
    tile = src.meta.shape[2]
    sub = tile // tm
    last = src.h1.shape[0] // tm - 1
    smem = lambda f: pl.BlockSpec((None, SUBLANES, tile), f, memory_space=pltpu.SMEM)
    in_specs = [smem(lambda i: (i // sub, 0, 0)), smem(lambda i: (jnp.minimum(i + 1, last) // sub, 0, 0)),
                pl.BlockSpec((tm, d), lambda i: (i, 0)), pl.BlockSpec((tm, LANES), lambda i: (i, 0)),
                pl.BlockSpec(memory_space=pl.ANY)]
    scratch = [pltpu.VMEM((2, TOP_K, tm * d // LANES, LANES), F32), pltpu.SemaphoreType.DMA((2,))]
    out_specs = [pl.BlockSpec((tm, d), lambda i: (i, 0))]
    out_shape = [jax.ShapeDtypeStruct(src.h1.shape, F32)]
    return True, sub, [src.meta, src.meta, src.h1, src.route, src.ys], in_specs, out_specs, out_shape, scratch


def _weight_scratch(kdim, ncols):
    return [pltpu.VMEM((kdim, ncols), BF16), pltpu.VMEM((2, kdim, WEIGHT_CHUNK), F32),
            pltpu.SemaphoreType.DMA((2,))]


def _fox_inproj_kernel(*refs, layer, tiles_per_seq, scale, fused, sub):
    i = pl.program_id(0)
    if fused:
        (meta_cur, meta_nxt, h1_ref, route_ref, ys_ref, g_ref, w_hbm, wf_ref, bf_ref, hout_ref, q_ref, kb_ref,
         v_ref, gate_ref, a_ref, carry_ref, wb_ref, stage_ref, sem, buf_ref, gsem) = refs
        _issue_residual_gathers(meta_cur, meta_nxt, ys_ref, buf_ref, gsem, tm=h1_ref.shape[0], sub=sub)
    else:
        (h_ref, g_ref, w_hbm, wf_ref, bf_ref, q_ref, kb_ref, v_ref, gate_ref, a_ref,
         carry_ref, wb_ref, stage_ref, sem) = refs
    tm, d = q_ref.shape

    @pl.when(i == 0)
    def _():
        _stage_weight(w_hbm.at[layer], wb_ref, stage_ref, sem, 4 * d)

    if fused:
        x = _gathered_residual(h1_ref, route_ref, buf_ref, gsem)
        hout_ref[...] = x
    else:
        x = h_ref[...]
    xn = _rms(x, g_ref[...], EPS).astype(BF16)

    q = jnp.dot(xn, wb_ref[:, 0:d], preferred_element_type=F32)
    q_ref[...] = (q * scale).astype(BF16)
    v_ref[...] = jnp.dot(xn, wb_ref[:, 2 * d:3 * d], preferred_element_type=F32).astype(BF16)
    gate_ref[...] = jnp.dot(xn, wb_ref[:, 3 * d:4 * d], preferred_element_type=F32)

    f = jnp.dot(xn, wf_ref[...], preferred_element_type=F32) + bf_ref[...]
    ls = jnp.minimum(f, 0.0) - jnp.log1p(jnp.exp(-jnp.abs(f)))

    @pl.when(i % tiles_per_seq == 0)
    def _():
        carry_ref[...] = jnp.zeros_like(carry_ref)

    row = lax.broadcasted_iota(I32, (tm, tm), 0)
    col = lax.broadcasted_iota(I32, (tm, tm), 1)
    tri = jnp.where(col <= row, 1.0, 0.0).astype(BF16)
    hi, mid, lo = _split3(ls)
    cum = (jnp.dot(tri, hi, preferred_element_type=F32) + jnp.dot(tri, mid, preferred_element_type=F32)
           + jnp.dot(tri, lo, preferred_element_type=F32)) + carry_ref[...]
    carry_ref[...] = cum[tm - 1:tm, :]

    c_hi, c_mid, c_lo = (c.astype(F32) for c in _split3(cum))
    lane = lax.broadcasted_iota(I32, (tm, LANES), 1)
    a_ref[...] = jnp.where(lane < 16, c_hi, jnp.where(lane < 32, c_mid, jnp.where(
        lane < 48, c_lo, jnp.where(lane < 96, 1.0, 0.0)))).astype(BF16)
    bk = jnp.where(lane < 48, 1.0, jnp.where(lane < 64, -c_hi, jnp.where(
        lane < 80, -c_mid, jnp.where(lane < 96, -c_lo, 0.0)))).astype(BF16)

    k = jnp.dot(xn, wb_ref[:, d:2 * d], preferred_element_type=F32).astype(BF16)
    for p in range(d // LANES):
        kb_ref[:, 2 * p * LANES:(2 * p + 1) * LANES] = k[:, p * LANES:(p + 1) * LANES]
        kb_ref[:, (2 * p + 1) * LANES:(2 * p + 2) * LANES] = bk


def _fox_inproj(src, g, w_all, layer, wf, bfr, *, seq):
    t, d = src.h1.shape if isinstance(src, _Pending) else src.shape
    tm = min(ROW_TILE, seq)
    scale = float((d // FOX_HEADS) ** -0.5)
    fused, sub, operands, in_specs, out_specs, out_shape, scratch = _residual_specs(src, tm, d)
    kern = functools.partial(_fox_inproj_kernel, layer=layer, tiles_per_seq=seq // tm, scale=scale,
                             fused=fused, sub=sub)
    row = lambda i: (i, 0)
    const = lambda i: (0, 0)
    grid_spec = pltpu.PrefetchScalarGridSpec(
        num_scalar_prefetch=0,
        grid=(t // tm,),
        in_specs=in_specs + [pl.BlockSpec((1, d), const), pl.BlockSpec(memory_space=pl.ANY),
                             pl.BlockSpec((d, LANES), const), pl.BlockSpec((1, LANES), const)],
        out_specs=out_specs + [pl.BlockSpec((tm, d), row), pl.BlockSpec((tm, 2 * d), row),
                               pl.BlockSpec((tm, d), row), pl.BlockSpec((tm, d), row),
                               pl.BlockSpec((tm, LANES), row)],
        scratch_shapes=[pltpu.VMEM((1, LANES), F32)] + _weight_scratch(d, 4 * d) + scratch,
    )
    outs = pl.pallas_call(
        kern,
        grid_spec=grid_spec,
        out_shape=out_shape + [jax.ShapeDtypeStruct((t, d), BF16), jax.ShapeDtypeStruct((t, 2 * d), BF16),
                               jax.ShapeDtypeStruct((t, d), BF16), jax.ShapeDtypeStruct((t, d), F32),
                               jax.ShapeDtypeStruct((t, LANES), BF16)],
        compiler_params=_cparams("arbitrary"),
        name="fox_inproj",
    )(*operands, g, w_all, wf, bfr)
    return (outs if fused else [src] + list(outs))


def _fox_attn_kernel(q_ref, a_ref, kb_ref, v_ref, gate_ref, o_ref, *, tq, dh, depth):
    seq = q_ref.shape[0]
    pair = pl.program_id(1)
    lane = lax.broadcasted_iota(I32, (tq, LANES), 1)
    krow = lax.broadcasted_iota(I32, (tq, tq), 0)
    qcol = lax.broadcasted_iota(I32, (tq, tq), 1)
    causal = krow <= qcol
    heads_per_block = LANES // dh
    qmask = [jnp.where((lane >= hh * dh) & (lane < (hh + 1) * dh), 1.0, 0.0).astype(BF16)
             for hh in range(heads_per_block)]
    amask = [jnp.where((lane < 96) & ((lane & 15) == pair * heads_per_block + hh), 1.0, 0.0).astype(BF16)
             for hh in range(heads_per_block)]
    vt = jnp.transpose(v_ref[...])
    chains = [(i, hh) for i in range(seq // tq) for hh in range(heads_per_block)]

    def scores(i, hh):
        off = i * tq
        qa = jnp.concatenate([q_ref[off:off + tq, :] * qmask[hh], a_ref[off:off + tq, :] * amask[hh]], axis=1)
        return lax.dot_general(kb_ref[0:off + tq, :], qa, _CONTRACT_LAST, preferred_element_type=F32)

    def finish(i, hh, st):
        off = i * tq
        sd = jnp.where(causal, st[off:off + tq, :], NEG_INF)
        st = jnp.concatenate([st[0:off, :], sd], axis=0) if off > 0 else sd
        m = jnp.max(st, axis=0, keepdims=True)
        e = jnp.exp(st - m)
        inv = 1.0 / jnp.sum(e, axis=0, keepdims=True)
        return jnp.dot(vt[hh * dh:(hh + 1) * dh, 0:off + tq], e.astype(BF16), preferred_element_type=F32) * inv

    outs = {}
    queue = [scores(*c) for c in chains[:depth]]
    for n, (i, hh) in enumerate(chains):
        cur = queue.pop(0)
        if n + depth < len(chains):
            queue.append(scores(*chains[n + depth]))
        outs[hh] = finish(i, hh, cur)
        if hh == heads_per_block - 1:
            off = i * tq
            o = jnp.transpose(jnp.concatenate([outs[h2] for h2 in range(heads_per_block)], axis=0))
            o_ref[off:off + tq, :] = (o * _sigmoid(gate_ref[off:off + tq, :])).astype(BF16)


def _fox_attn(q, a, kb, v, gate, *, batch, seq, tq=256, depth=3):
    t, d = q.shape
    dh = d // FOX_HEADS
    pairs = d // LANES
    kern = functools.partial(_fox_attn_kernel, tq=tq, dh=dh, depth=depth)
    blk = lambda b, p: (b, p)
    return pl.pallas_call(
        kern,
        grid=(batch, pairs),
        in_specs=[pl.BlockSpec((seq, LANES), blk), pl.BlockSpec((seq, LANES), lambda b, p: (b, 0)),
                  pl.BlockSpec((seq, 2 * LANES), blk), pl.BlockSpec((seq, LANES), blk),
                  pl.BlockSpec((seq, LANES), blk)],
        out_specs=pl.BlockSpec((seq, LANES), blk),
        out_shape=jax.ShapeDtypeStruct((t, d), BF16),
        compiler_params=_cparams("arbitrary", "arbitrary"),
        name="fox_attn",
    )(q, a, kb, v, gate)


def _ret_inproj_kernel(*refs, layer, kscale, fused, sub):
    if fused:
        (meta_cur, meta_nxt, h1_ref, route_ref, ys_ref, g_ref, w_hbm, cos_ref, sin_ref, hout_ref, q_ref, k_ref,
         v_ref, gate_ref, wb_ref, stage_ref, sem, buf_ref, gsem) = refs
        _issue_residual_gathers(meta_cur, meta_nxt, ys_ref, buf_ref, gsem, tm=h1_ref.shape[0], sub=sub)
    else:
        h_ref, g_ref, w_hbm, cos_ref, sin_ref, q_ref, k_ref, v_ref, gate_ref, wb_ref, stage_ref, sem = refs
    tm, d = q_ref.shape

    @pl.when(pl.program_id(0) == 0)
    def _():
        _stage_weight(w_hbm.at[layer], wb_ref, stage_ref, sem, 6 * d)

    if fused:
        x = _gathered_residual(h1_ref, route_ref, buf_ref, gsem)
        hout_ref[...] = x
    else:
        x = h_ref[...]
    xn = _rms(x, g_ref[...], EPS).astype(BF16)
    cos = cos_ref[...]
    sin = sin_ref[...]
    dk = d // RET_HEADS
    half = dk // 2

    def rot(t, h):
        t1 = t[:, h * dk:h * dk + half]
        t2 = t[:, h * dk + half:(h + 1) * dk]
        return t1 * cos - t2 * sin, t1 * sin + t2 * cos

    q = jnp.dot(xn, wb_ref[:, 0:d], preferred_element_type=F32)
    k = jnp.dot(xn, wb_ref[:, d:2 * d], preferred_element_type=F32)
    for h in range(RET_HEADS):
        q1, q2 = rot(q, h)
        q_ref[:, h * dk:h * dk + half] = q1.astype(BF16)
        q_ref[:, h * dk + half:(h + 1) * dk] = q2.astype(BF16)
        k1, k2 = rot(k, h)
        k_ref[:, h * dk:h * dk + half] = k1 * kscale
        k_ref[:, h * dk + half:(h + 1) * dk] = k2 * kscale
    v_ref[...] = jnp.dot(xn, wb_ref[:, 2 * d:4 * d], preferred_element_type=F32).astype(BF16)
    gate_ref[...] = jnp.dot(xn, wb_ref[:, 4 * d:6 * d], preferred_element_type=F32)


def _ret_inproj(src, g, w_all, layer, cos, sin, *, seq, tm=256):
    t, d = src.h1.shape if isinstance(src, _Pending) else src.shape
    tm = min(tm, seq)
    half = d // RET_HEADS // 2
    fused, sub, operands, in_specs, out_specs, out_shape, scratch = _residual_specs(src, tm, d)
    kern = functools.partial(_ret_inproj_kernel, layer=layer, kscale=float((d // RET_HEADS) ** -0.5),
                             fused=fused, sub=sub)
    row = lambda i: (i, 0)
    const = lambda i: (0, 0)
    pos = lambda i: (i % (seq // tm), 0)
    grid_spec = pltpu.PrefetchScalarGridSpec(
        num_scalar_prefetch=0,
        grid=(t // tm,),
        in_specs=in_specs + [pl.BlockSpec((1, d), const), pl.BlockSpec(memory_space=pl.ANY),
                             pl.BlockSpec((tm, half), pos), pl.BlockSpec((tm, half), pos)],
        out_specs=out_specs + [pl.BlockSpec((tm, d), row), pl.BlockSpec((tm, d), row),
                               pl.BlockSpec((tm, 2 * d), row), pl.BlockSpec((tm, 2 * d), row)],
        scratch_shapes=_weight_scratch(d, 6 * d) + scratch,
    )
    outs = pl.pallas_call(
        kern,
        grid_spec=grid_spec,
        out_shape=out_shape + [jax.ShapeDtypeStruct((t, d), BF16), jax.ShapeDtypeStruct((t, d), F32),
                               jax.ShapeDtypeStruct((t, 2 * d), BF16), jax.ShapeDtypeStruct((t, 2 * d), F32)],
        compiler_params=_cparams("arbitrary"),
        name="ret_inproj",
    )(*operands, g, w_all, cos, sin)
    return (outs if fused else [src] + list(outs))


def _retention_kernel(q_ref, k_ref, v_ref, gate_ref, gain_ref, intra_ref, qd_ref, kd_ref, cd_ref,
                      y_ref, r_ref, *, chunk):
    seq = q_ref.shape[0]
    r_ref[...] = jnp.zeros_like(r_ref)

    def body(c):
        off = c * chunk
        qn = q_ref[pl.ds(off, chunk), :]
        k32 = k_ref[pl.ds(off, chunk), :]
        vn = v_ref[pl.ds(off, chunk), :]
        s = lax.dot_general(qn, k32.astype(BF16), _CONTRACT_LAST, preferred_element_type=F32) * intra_ref[...]
        o = jnp.dot(s.astype(BF16), vn, preferred_element_type=F32)
        r = r_ref[...]
        o = o + jnp.dot(qn, r.astype(BF16), preferred_element_type=F32) * qd_ref[...]
        kdec = (k32 * kd_ref[...]).astype(BF16)
        r_ref[...] = r * cd_ref[...] + lax.dot_general(kdec, vn, _CONTRACT_FIRST, preferred_element_type=F32)
        mu = jnp.mean(o, axis=-1, keepdims=True)
        dlt = o - mu
        var = jnp.mean(dlt * dlt, axis=-1, keepdims=True)
        on = dlt * lax.rsqrt(var + EPS) * gain_ref[...]
        g = gate_ref[pl.ds(off, chunk), :]
        y_ref[pl.ds(off, chunk), :] = ((g * _sigmoid(g)) * on).astype(BF16)

    for c in range(seq // chunk):
        body(c)


def _retention(q, k, v, gate, gain, intra, qd, kd, cd, *, batch, seq):
    t, d = q.shape
    dk = d // RET_HEADS
    dv = v.shape[1] // RET_HEADS
    c = RET_CHUNK
    kern = functools.partial(_retention_kernel, chunk=c)
    blk = lambda b, h: (b, h)
    hd = lambda b, h: (h, 0, 0)
    return pl.pallas_call(
        kern,
        grid=(batch, RET_HEADS),
        in_specs=[pl.BlockSpec((seq, dk), blk), pl.BlockSpec((seq, dk), blk), pl.BlockSpec((seq, dv), blk),
                  pl.BlockSpec((seq, dv), blk), pl.BlockSpec((1, dv), lambda b, h: (0, h)),
                  pl.BlockSpec((None, c, c), hd), pl.BlockSpec((None, c, 1), hd),
                  pl.BlockSpec((None, c, 1), hd), pl.BlockSpec((None, 1, 1), hd)],
        out_specs=pl.BlockSpec((seq, dv), blk),
        out_shape=jax.ShapeDtypeStruct((t, v.shape[1]), BF16),
        scratch_shapes=[pltpu.VMEM((dk, dv), F32)],
        compiler_params=_cparams("arbitrary", "arbitrary"),
        name="retention",
    )(q, k, v, gate, gain, intra, qd, kd, cd)


def _outproj_router_kernel(a_ref, w_hbm, h_ref, g_ref, wrh_ref, wrl_ref, br_ref,
                           h1_ref, hn_ref, route_ref, meta_ref, cnt_ref, wb_ref, stage_ref, sem, *, layer):
    i = pl.program_id(0)
    tm = a_ref.shape[0]
    d = h_ref.shape[1]

    @pl.when(i == 0)
    def _():
        cnt_ref[...] = jnp.zeros_like(cnt_ref)
        _stage_weight(w_hbm.at[layer], wb_ref, stage_ref, sem, d)

    h1 = h_ref[...] + jnp.dot(a_ref[...], wb_ref[...], preferred_element_type=F32)
    h1_ref[...] = h1
    hn = _rms(h1, g_ref[...], EPS)
    _rows_to_slabs(hn_ref, hn)

    hh = hn.astype(BF16)
    hl = (hn - hh.astype(F32)).astype(BF16)
    lg = (jnp.dot(hh, wrh_ref[...], preferred_element_type=F32)
          + jnp.dot(hh, wrl_ref[...], preferred_element_type=F32)
          + jnp.dot(hl, wrh_ref[...], preferred_element_type=F32)) + br_ref[...]

    lane = lax.broadcasted_iota(I32, (tm, LANES), 1)
    lanef = lane.astype(F32)
    big = float(LANES)

    def softmax_masked(mask):
        mx = jnp.max(jnp.where(mask, lg, -jnp.inf), axis=1, keepdims=True)
        e = jnp.where(mask, jnp.exp(lg - mx), 0.0)
        return e / jnp.sum(e, axis=1, keepdims=True)

    def top1(p, mask):
        best = jnp.max(jnp.where(mask, p, -1.0), axis=1, keepdims=True)
        idx = jnp.min(jnp.where(mask & (p == best), lanef, big), axis=1, keepdims=True)
        return best, idx

    gmask = lane < N_GROUPS
    pg, gi = top1(softmax_masked(gmask), gmask)
    lo = ROUTE_LANE0 + gi * EXPERTS_PER_GROUP
    emask = (lanef >= lo) & (lanef < lo + EXPERTS_PER_GROUP)
    pe = softmax_masked(emask)
    p1, i1 = top1(pe, emask)
    p2, i2 = top1(pe, emask & (lanef != i1))
    den = p1 + p2
    g1 = pg * p1 / den
    g2 = pg * p2 / den

    oh1 = lanef == i1
    oh2 = lanef == i2
    both = jnp.where(oh1 | oh2, 1.0, 0.0)
    row = lax.broadcasted_iota(I32, (tm, tm), 0)
    col = lax.broadcasted_iota(I32, (tm, tm), 1)
    strict = jnp.where(col < row, 1.0, 0.0).astype(BF16)
    before = jnp.dot(strict, both.astype(BF16), preferred_element_type=F32) + cnt_ref[0:1, :]
    r1 = jnp.sum(jnp.where(oh1, before, 0.0), axis=1, keepdims=True)
    r2 = jnp.sum(jnp.where(oh2, before, 0.0), axis=1, keepdims=True)
    cnt_ref[0:1, :] = cnt_ref[0:1, :] + jnp.sum(both, axis=0, keepdims=True)

    e1 = i1 - ROUTE_LANE0
    e2 = i2 - ROUTE_LANE0
    route = jnp.where(lane == 0, e1, jnp.where(lane == 1, e2, jnp.where(
        lane == 2, r1, jnp.where(lane == 3, r2, jnp.where(lane == 4, g1, jnp.where(lane == 5, g2, 0.0))))))
    route_ref[...] = route
    meta_ref[...] = jnp.transpose(route)[0:SUBLANES, :].astype(I32)


def _outproj_router(a, w_all, layer, h, g, wrh, wrl, br, *, tm):
    t, kdim = a.shape
    d = h.shape[1]
    row = lambda i: (i, 0)
    const = lambda i: (0, 0)
    kern = functools.partial(_outproj_router_kernel, layer=layer)
    return pl.pallas_call(
        kern,
        grid=(t // tm,),
        in_specs=[pl.BlockSpec((tm, kdim), row), pl.BlockSpec(memory_space=pl.ANY), pl.BlockSpec((tm, d), row),
                  pl.BlockSpec((1, d), const), pl.BlockSpec((d, LANES), const), pl.BlockSpec((d, LANES), const),
                  pl.BlockSpec((1, LANES), const)],
        out_specs=[pl.BlockSpec((tm, d), row), pl.BlockSpec((tm * d // LANES, LANES), row),
                   pl.BlockSpec((tm, LANES), row), pl.BlockSpec((None, SUBLANES, tm), lambda i: (i, 0, 0)),
                   pl.BlockSpec((SUBLANES, LANES), const)],
        out_shape=[jax.ShapeDtypeStruct((t, d), F32), jax.ShapeDtypeStruct((t * d // LANES, LANES), F32),
                   jax.ShapeDtypeStruct((t, LANES), F32), jax.ShapeDtypeStruct((t // tm, SUBLANES, tm), I32),
                   jax.ShapeDtypeStruct((SUBLANES, LANES), F32)],
        scratch_shapes=_weight_scratch(kdim, d),
        compiler_params=_cparams("arbitrary"),
        name="outproj_router",
    )(a, w_all, h, g, wrh, wrl, br)


def _pad_copies(counts_ref, pstart_ref, zero_ref, xs_ref, sem, *, bm):
    nslab = SUBLANES
    out = []
    for e in range(N_EXPERTS):
        cnt = counts_ref[e]
        npad = (bm - (cnt & (bm - 1))) & (bm - 1)
        pos = pstart_ref[e] + cnt
        bit = 1
        while bit < bm:
            out.append(((npad & bit) != 0, pltpu.make_async_copy(
                _slab_rows(zero_ref, 0, bit, nslab), _slab_rows(xs_ref, pos, bit, nslab), sem)))
            pos = pos + (npad & bit)
            bit *= 2
    return out


def _dispatch_kernel(counts_ref, pstart_ref, nused_ref, meta_ref, hn_ref, xs_ref, zero_ref, sem, psem, *, bm):
    nslab = SUBLANES
    tm = hn_ref.shape[0] // nslab
    half = bm // 2

    @pl.when(pl.program_id(0) == 0)
    def _():
        zero_ref[...] = jnp.zeros_like(zero_ref)
        pads = _pad_copies(counts_ref, pstart_ref, zero_ref, xs_ref, psem, bm=bm)
        for pred, cp in pads:
            pl.when(pred)(cp.start)

        def tail_copy(j):
            return pltpu.make_async_copy(zero_ref, _slab_rows(xs_ref, j * half, half, nslab), psem)

        first = nused_ref[0] * 2
        last = xs_ref.shape[0] // (half * nslab)
        lax.fori_loop(first, last, lambda j, c: (tail_copy(j).start(), c)[1], 0)
        for pred, cp in pads:
            pl.when(pred)(cp.wait)
        lax.fori_loop(first, last, lambda j, c: (tail_copy(j).wait(), c)[1], 0)

    def row_copy(t, k):
        dst = meta_ref[TOP_K + k, t]
        return pltpu.make_async_copy(_slab_rows(hn_ref, t, 1, nslab), _slab_at(xs_ref, dst, nslab), sem)

    def issue(t, carry):
        for k in range(TOP_K):
            row_copy(t, k).start(priority=k)
        return carry

    lax.fori_loop(0, tm, issue, 0, unroll=8)

    def drain(t, carry):
        for k in range(TOP_K):
            row_copy(t, k).wait()
        return carry

    lax.fori_loop(0, tm, drain, 0, unroll=8)


def _dispatch(counts, pstart, n_used, meta, hn, n_rows, *, bm):
    nslab = SUBLANES
    t = hn.shape[0] // nslab
    tm = meta.shape[2]
    kern = functools.partial(_dispatch_kernel, bm=bm)
    grid_spec = pltpu.PrefetchScalarGridSpec(
        num_scalar_prefetch=3,
        grid=(t // tm,),
        in_specs=[pl.BlockSpec((None, SUBLANES, tm), lambda i, *_: (i, 0, 0), memory_space=pltpu.SMEM),
                  pl.BlockSpec((tm * nslab, LANES), lambda i, *_: (i, 0))],
        out_specs=pl.BlockSpec(memory_space=pl.ANY),
        scratch_shapes=[pltpu.VMEM((bm // 2 * nslab, LANES), F32), pltpu.SemaphoreType.DMA(()),
                        pltpu.SemaphoreType.DMA(())],
    )
    return pl.pallas_call(
        kern,
        grid_spec=grid_spec,
        out_shape=jax.ShapeDtypeStruct((n_rows * nslab, LANES), F32),
        compiler_params=_cparams("arbitrary"),
        name="moe_dispatch",
    )(counts, pstart, n_used, meta, hn)


def _experts_kernel(be_ref, nu_ref, x_ref, wgu_ref, wd_ref, y_ref, wgu_b, wd_b):
    i = pl.program_id(0)
    d = wgu_b.shape[0]
    bm = x_ref.shape[0] * LANES // d
    prev = be_ref[jnp.maximum(i - 1, 0)]

    @pl.when((i < nu_ref[0]) & ((i == 0) | (be_ref[i] != prev)))
    def _():
        wgu_b[...] = wgu_ref[...].astype(BF16)
        wd_b[...] = wd_ref[...].astype(BF16)

    @pl.when(i < nu_ref[0])
    def _():
        x = _slabs_to_rows(x_ref, bm, d // LANES).astype(BF16)
        hcat = jnp.dot(x, wgu_b[...], preferred_element_type=F32)
        ff = hcat.shape[1] // 2
        a = hcat[:, :ff]
        b = hcat[:, ff:]
        act = ((a * _sigmoid(a)) * b).astype(BF16)
        _rows_to_slabs(y_ref, jnp.dot(act, wd_b[...], preferred_element_type=F32))

    @pl.when(i >= nu_ref[0])
    def _():
        y_ref[...] = jnp.zeros_like(y_ref)


def _experts(block_e, n_used, xs, wgu_all, wd_all, layer, *, bm):
    d = wgu_all.shape[2]
    nslab = d // LANES
    n_rows = xs.shape[0] // nslab
    ff2 = wgu_all.shape[3]
    grid_spec = pltpu.PrefetchScalarGridSpec(
        num_scalar_prefetch=2,
        grid=(n_rows // bm,),
        in_specs=[pl.BlockSpec((bm * nslab, LANES), lambda i, be, nu: (i, 0)),
                  pl.BlockSpec((None, None, d, ff2), lambda i, be, nu: (layer, be[i], 0, 0)),
                  pl.BlockSpec((None, None, ff2 // 2, d), lambda i, be, nu: (layer, be[i], 0, 0))],
        out_specs=pl.BlockSpec((bm * nslab, LANES), lambda i, be, nu: (i, 0)),
        scratch_shapes=[pltpu.VMEM((d, ff2), BF16), pltpu.VMEM((ff2 // 2, d), BF16)],
    )
    return pl.pallas_call(
        _experts_kernel,
        grid_spec=grid_spec,
        out_shape=jax.ShapeDtypeStruct((n_rows * nslab, LANES), F32),
        compiler_params=_cparams("arbitrary"),
        name="moe_experts",
    )(block_e, n_used, xs, wgu_all, wd_all)


def _combine_kernel(meta_ref, h1_ref, route_ref, gfin_ref, ys_ref, o_ref, buf_ref, sem, *, final_norm):
    tm, d = h1_ref.shape
    nslab = d // LANES

    def row_copy(t, k):
        src = meta_ref[TOP_K + k, t]
        return pltpu.make_async_copy(_slab_at(ys_ref, src, nslab), _slab_rows(buf_ref.at[k], t, 1, nslab), sem)

    def issue(t, carry):
        for k in range(TOP_K):
            row_copy(t, k).start(priority=k)
        return carry

    lax.fori_loop(0, tm, issue, 0, unroll=8)

    def drain(t, carry):
        for k in range(TOP_K):
            row_copy(t, k).wait()
        return carry

    lax.fori_loop(0, tm, drain, 0, unroll=8)

    g0 = route_ref[:, 4:5]
    g1 = route_ref[:, 5:6]
    out = h1_ref[...] + (_slabs_to_rows(buf_ref.at[0], tm, nslab) * g0 + _slabs_to_rows(buf_ref.at[1], tm, nslab) * g1)
    if final_norm:
        out = _rms(out, gfin_ref[...], EPS)
    o_ref[...] = out


def _combine(meta, h1, route, gfin, ys, *, final_norm):
    t, d = h1.shape
    tm = meta.shape[2]
    kern = functools.partial(_combine_kernel, final_norm=final_norm)
    grid_spec = pltpu.PrefetchScalarGridSpec(
        num_scalar_prefetch=0,
        grid=(t // tm,),
        in_specs=[pl.BlockSpec((None, SUBLANES, tm), lambda i: (i, 0, 0), memory_space=pltpu.SMEM),
                  pl.BlockSpec((tm, d), lambda i: (i, 0)),
                  pl.BlockSpec((tm, LANES), lambda i: (i, 0)),
                  pl.BlockSpec((1, d), lambda i: (0, 0)),
                  pl.BlockSpec(memory_space=pl.ANY)],
        out_specs=pl.BlockSpec((tm, d), lambda i: (i, 0)),
        scratch_shapes=[pltpu.VMEM((TOP_K, tm * d // LANES, LANES), F32), pltpu.SemaphoreType.DMA(())],
    )
    return pl.pallas_call(
        kern,
        grid_spec=grid_spec,
        out_shape=jax.ShapeDtypeStruct((t, d), F32),
        compiler_params=_cparams("arbitrary"),
        name="moe_combine",
    )(meta, h1, route, gfin, ys)


def _moe(a, w_out_all, mixer_layer, h, layer, g_ffn, wr_g, br_g, wr_e, br_e, w_gu_all, w_down_all, gfin,
         *, final_norm):
    t, d = h.shape
    bm = MOE_BLOCK
    tm = min(ROW_TILE, t)
    wr = jnp.concatenate([wr_g, wr_e, jnp.zeros((d, LANES - N_GROUPS - N_EXPERTS), F32)], axis=1)
    wrh = wr.astype(BF16)
    wrl = (wr - wrh.astype(F32)).astype(BF16)
    br = jnp.concatenate([br_g, br_e, jnp.zeros((LANES - N_GROUPS - N_EXPERTS,), F32)])[None, :]
    h1, hn, route, meta, cnt = _outproj_router(a, w_out_all, mixer_layer, h, g_ffn[None, :], wrh, wrl, br, tm=tm)

    counts = cnt[0, ROUTE_LANE0:ROUTE_LANE0 + N_EXPERTS].astype(I32)
    nb = (counts + bm - 1) // bm
    nb_end = jnp.cumsum(nb)
    pstart = ((nb_end - nb) * bm).astype(I32)
    n_blocks = (t * TOP_K + N_EXPERTS * (bm - 1)) // bm
    j = jnp.arange(n_blocks, dtype=I32)
    block_e = jnp.minimum(jnp.sum((nb_end[None, :] <= j[:, None]).astype(I32), axis=1), N_EXPERTS - 1)
    n_used = nb_end[-1:].astype(I32)

    eid = meta[:, 0:TOP_K, :]
    base = jnp.sum(jnp.where(eid[..., None] == jnp.arange(N_EXPERTS, dtype=I32), pstart, 0), axis=-1)
    meta = jnp.concatenate([eid, (meta[:, TOP_K:2 * TOP_K, :] + base) * (d // LANES), meta[:, 2 * TOP_K:, :]], axis=1)

    xs = _dispatch(counts, pstart, n_used, meta, hn, n_blocks * bm, bm=bm)
    ys = _experts(block_e, n_used, xs, w_gu_all, w_down_all, layer, bm=bm)
    if final_norm:
        return _combine(meta, h1, route, gfin[None, :], ys, final_norm=True)
    return _Pending(meta, h1, route, ys)


def _fox_mixer(src, g, w_in_all, layer, b_f, *, batch, seq):
    d = w_in_all.shape[1]
    nh = FOX_HEADS
    wf = w_in_all[layer, :, 4 * d:4 * d + nh]
    wf = jnp.concatenate([wf] * 6 + [jnp.zeros((d, LANES - 6 * nh), F32)], axis=1).astype(BF16)
    bfr = jnp.concatenate([b_f] * 6 + [jnp.zeros((LANES - 6 * nh,), F32)])[None, :]
    h, q, kb, v, gate, a = _fox_inproj(src, g[None, :], w_in_all, layer, wf, bfr, seq=seq)
    return h, _fox_attn(q, a, kb, v, gate, batch=batch, seq=seq)


def _ret_mixer(src, g, w_in_all, layer, gn_gain, *, batch, seq):
    d = w_in_all.shape[1]
    nh = RET_HEADS
    dk = d // nh
    c = RET_CHUNK
    inv = 1.0 / (ROPE_BASE ** jnp.linspace(0.0, 1.0, dk // 2, dtype=F32))
    ang = jnp.arange(seq).astype(F32)[:, None] * inv[None, :]
    cos, sin = jnp.cos(ang), jnp.sin(ang)
    log_g = jnp.log(1.0 - 2.0 ** (-5.0 - jnp.arange(nh, dtype=F32)))
    idx = jnp.arange(c, dtype=F32)
    rel = idx[:, None] - idx[None, :]
    intra = jnp.where(rel >= 0, jnp.exp(log_g[:, None, None] * jnp.maximum(rel, 0.0)), 0.0)
    qd = jnp.exp(log_g[:, None] * (idx + 1.0))[:, :, None]
    kd = jnp.exp(log_g[:, None] * (c - 1.0 - idx))[:, :, None]
    cd = jnp.exp(log_g * c)[:, None, None]
    h, q, k, v, gate = _ret_inproj(src, g[None, :], w_in_all, layer, cos, sin, seq=seq)
    return h, _retention(q, k, v, gate, gn_gain[None, :], intra, qd, kd, cd, batch=batch, seq=seq)


def kernel(x, fox_w_in, fox_b_f, fox_w_out, ret_w_in, ret_gn_gain, ret_w_out, norm_mix, norm_ffn,
           router_group_w, router_group_b, router_expert_w, router_expert_b, expert_w_gu, expert_w_down,
           norm_final):
    batch, seq, d = x.shape
    depth = norm_mix.shape[0]
    src = x.reshape(batch * seq, d)
    for i in range(depth):
        j = i // 2
        if i % 2 == 0:
            h, a = _fox_mixer(src, norm_mix[i], fox_w_in, j, fox_b_f[j], batch=batch, seq=seq)
            w_out_all = fox_w_out
        else:
            h, a = _ret_mixer(src, norm_mix[i], ret_w_in, j, ret_gn_gain[j], batch=batch, seq=seq)
            w_out_all = ret_w_out
        src = _moe(a, w_out_all, j, h, i, norm_ffn[i], router_group_w[i], router_group_b[i], router_expert_w[i],
                   router_expert_b[i], expert_w_gu, expert_w_down, norm_final, final_norm=(i == depth - 1))
    return src.reshape(batch, seq, d)
```

```python
import functools
from typing import NamedTuple

import jax
import jax.numpy as jnp
from jax import lax
from jax.experimental import pallas as pl
from jax.experimental.pallas import tpu as pltpu

F32 = jnp.float32
BF16 = jnp.bfloat16
I32 = jnp.int32

FOX_HEADS = 16
RET_HEADS = 4
RET_CHUNK = 128
N_GROUPS = 4
EXPERTS_PER_GROUP = 8
N_EXPERTS = N_GROUPS * EXPERTS_PER_GROUP
TOP_K = 2
EPS = 1e-6
NEG_INF = -1e30
ROPE_BASE = 10000.0

LANES = 128
SUBLANES = 8
ROUTE_LANE0 = N_GROUPS
VMEM_LIMIT = 56 * 1024 * 1024
MOE_BLOCK = 512
ROW_TILE = 512
WEIGHT_CHUNK = 512

_CONTRACT_LAST = (((1,), (1,)), ((), ()))
_CONTRACT_FIRST = (((0,), (0,)), ((), ()))


def _cparams(*sem):
    return pltpu.CompilerParams(dimension_semantics=sem, vmem_limit_bytes=VMEM_LIMIT)


def _sigmoid(x):
    return 1.0 / (1.0 + jnp.exp(-x))


def _rms(x, g, eps):
    ms = jnp.mean(x * x, axis=-1, keepdims=True)
    return x * lax.rsqrt(ms + eps) * g


def _split3(x):
    hi = x.astype(BF16)
    r = x - hi.astype(F32)
    mid = r.astype(BF16)
    lo = (r - mid.astype(F32)).astype(BF16)
    return hi, mid, lo


def _stage_weight(w_hbm, wb_ref, stage_ref, sem, ncols):
    chunk = stage_ref.shape[2]
    n = ncols // chunk

    def cp(c):
        return pltpu.make_async_copy(w_hbm.at[:, pl.ds(c * chunk, chunk)], stage_ref.at[c % 2], sem.at[c % 2])

    cp(0).start()
    for c in range(n):
        if c + 1 < n:
            cp(c + 1).start()
        cp(c).wait()
        wb_ref[:, c * chunk:(c + 1) * chunk] = stage_ref[c % 2].astype(BF16)


def _rows_to_slabs(ref, x):
    n, d = x.shape
    nslab = d // LANES
    for c in range(nslab):
        ref[pl.ds(c, n, stride=nslab), :] = x[:, c * LANES:(c + 1) * LANES]


def _slabs_to_rows(ref, n, nslab):
    return jnp.concatenate([ref[pl.ds(c, n, stride=nslab), :] for c in range(nslab)], axis=1)


def _slab_rows(ref, row, nrows, nslab):
    return ref.at[pl.ds(pl.multiple_of(row * nslab, nslab), nrows * nslab), :]


def _slab_at(ref, first, nslab):
    return ref.at[pl.ds(pl.multiple_of(first, nslab), nslab), :]


class _Pending(NamedTuple):
    meta: jax.Array
    h1: jax.Array
    route: jax.Array
    ys: jax.Array


def _issue_residual_gathers(meta_cur, meta_nxt, ys_ref, buf_ref, gsem, *, tm, sub):
    i = pl.program_id(0)
    nslab = buf_ref.shape[2] // tm

    def issue(meta_ref, step):
        slot = step % 2
        off = (step % sub) * tm

        def body(t, carry):
            for k in range(TOP_K):
                src = meta_ref[0, TOP_K * (off + t) + k]
                pltpu.make_async_copy(_slab_at(ys_ref, src, nslab), _slab_rows(buf_ref.at[slot, k], t, 1, nslab),
                                      gsem.at[slot]).start(priority=k)
            return carry

        lax.fori_loop(0, tm, body, 0, unroll=8)

    @pl.when(i == 0)
    def _():
        issue(meta_cur, i)

    @pl.when(i + 1 < pl.num_programs(0))
    def _():
        issue(meta_nxt, i + 1)


def _gathered_residual(h1_ref, route_ref, buf_ref, gsem):
    tm, d = h1_ref.shape
    nslab = d // LANES
    slot = pl.program_id(0) % 2
    pltpu.make_async_copy(buf_ref.at[slot], buf_ref.at[slot], gsem.at[slot]).wait()
    g0 = route_ref[:, 4:5]
    g1 = route_ref[:, 5:6]
    return h1_ref[...] + (_slabs_to_rows(buf_ref.at[slot, 0], tm, nslab) * g0
                          + _slabs_to_rows(buf_ref.at[slot, 1], tm, nslab) * g1)


def _residual_specs(src, tm, d):
    if not isinstance(src, _Pending):
        return False, 1, [src], [pl.BlockSpec((tm, d), lambda i: (i, 0))], [], [], []
    tile = src.meta.shape[2] // TOP_K
    sub = tile // tm
    last = src.h1.shape[0] // tm - 1
    smem = lambda f: pl.BlockSpec((None, 1, TOP_K * tile), f, memory_space=pltpu.SMEM)
    in_specs = [smem(lambda i: (i // sub, 0, 0)), smem(lambda i: (jnp.minimum(i + 1, last) // sub, 0, 0)),
                pl.BlockSpec((tm, d), lambda i: (i, 0)), pl.BlockSpec((tm, LANES), lambda i: (i, 0)),
                pl.BlockSpec(memory_space=pl.ANY)]
    scratch = [pltpu.VMEM((2, TOP_K, tm * d // LANES, LANES), F32), pltpu.SemaphoreType.DMA((2,))]
    out_specs = [pl.BlockSpec((tm, d), lambda i: (i, 0))]
    out_shape = [jax.ShapeDtypeStruct(src.h1.shape, F32)]
    return True, sub, [src.meta, src.meta, src.h1, src.route, src.ys], in_specs, out_specs, out_shape, scratch


def _weight_scratch(kdim, ncols):
    return [pltpu.VMEM((kdim, ncols), BF16), pltpu.VMEM((2, kdim, WEIGHT_CHUNK), F32),
            pltpu.SemaphoreType.DMA((2,))]


def _fox_inproj_kernel(*refs, layer, tiles_per_seq, scale, fused, sub):
    i = pl.program_id(0)
    if fused:
        (meta_cur, meta_nxt, h1_ref, route_ref, ys_ref, g_ref, w_hbm, wf_ref, bf_ref, hout_ref, q_ref, kb_ref,
         v_ref, gate_ref, a_ref, carry_ref, wb_ref, stage_ref, sem, buf_ref, gsem) = refs
        _issue_residual_gathers(meta_cur, meta_nxt, ys_ref, buf_ref, gsem, tm=h1_ref.shape[0], sub=sub)
    else:
        (h_ref, g_ref, w_hbm, wf_ref, bf_ref, q_ref, kb_ref, v_ref, gate_ref, a_ref,
         carry_ref, wb_ref, stage_ref, sem) = refs
    tm, d = q_ref.shape

    @pl.when(i == 0)
    def _():
        _stage_weight(w_hbm.at[layer], wb_ref, stage_ref, sem, 4 * d)

    if fused:
        x = _gathered_residual(h1_ref, route_ref, buf_ref, gsem)
        hout_ref[...] = x
    else:
        x = h_ref[...]
    xn = _rms(x, g_ref[...], EPS).astype(BF16)

    q = jnp.dot(xn, wb_ref[:, 0:d], preferred_element_type=F32)
    q_ref[...] = (q * scale).astype(BF16)
    v_ref[...] = jnp.dot(xn, wb_ref[:, 2 * d:3 * d], preferred_element_type=F32).astype(BF16)
    gate_ref[...] = jnp.dot(xn, wb_ref[:, 3 * d:4 * d], preferred_element_type=F32)

    f = jnp.dot(xn, wf_ref[...], preferred_element_type=F32) + bf_ref[...]
    ls = jnp.minimum(f, 0.0) - jnp.log1p(jnp.exp(-jnp.abs(f)))

    @pl.when(i % tiles_per_seq == 0)
    def _():
        carry_ref[...] = jnp.zeros_like(carry_ref)

    row = lax.broadcasted_iota(I32, (tm, tm), 0)
    col = lax.broadcasted_iota(I32, (tm, tm), 1)
    tri = jnp.where(col <= row, 1.0, 0.0).astype(BF16)
    hi, mid, lo = _split3(ls)
    cum = (jnp.dot(tri, hi, preferred_element_type=F32) + jnp.dot(tri, mid, preferred_element_type=F32)
           + jnp.dot(tri, lo, preferred_element_type=F32)) + carry_ref[...]
    carry_ref[...] = cum[tm - 1:tm, :]

    c_hi, c_mid, c_lo = (c.astype(F32) for c in _split3(cum))
    lane = lax.broadcasted_iota(I32, (tm, LANES), 1)
    a_ref[...] = jnp.where(lane < 16, c_hi, jnp.where(lane < 32, c_mid, jnp.where(
        lane < 48, c_lo, jnp.where(lane < 96, 1.0, 0.0)))).astype(BF16)
    bk = jnp.where(lane < 48, 1.0, jnp.where(lane < 64, -c_hi, jnp.where(
        lane < 80, -c_mid, jnp.where(lane < 96, -c_lo, 0.0)))).astype(BF16)

    k = jnp.dot(xn, wb_ref[:, d:2 * d], preferred_element_type=F32).astype(BF16)
    for p in range(d // LANES):
        kb_ref[:, 2 * p * LANES:(2 * p + 1) * LANES] = k[:, p * LANES:(p + 1) * LANES]
        kb_ref[:, (2 * p + 1) * LANES:(2 * p + 2) * LANES] = bk


def _fox_inproj(src, g, w_all, layer, wf, bfr, *, seq):
    t, d = src.h1.shape if isinstance(src, _Pending) else src.shape
    tm = min(ROW_TILE, seq)
    scale = float((d // FOX_HEADS) ** -0.5)
    fused, sub, operands, in_specs, out_specs, out_shape, scratch = _residual_specs(src, tm, d)
    kern = functools.partial(_fox_inproj_kernel, layer=layer, tiles_per_seq=seq // tm, scale=scale,
                             fused=fused, sub=sub)
    row = lambda i: (i, 0)
    const = lambda i: (0, 0)
    grid_spec = pltpu.PrefetchScalarGridSpec(
        num_scalar_prefetch=0,
        grid=(t // tm,),
        in_specs=in_specs + [pl.BlockSpec((1, d), const), pl.BlockSpec(memory_space=pl.ANY),
                             pl.BlockSpec((d, LANES), const), pl.BlockSpec((1, LANES), const)],
        out_specs=out_specs + [pl.BlockSpec((tm, d), row), pl.BlockSpec((tm, 2 * d), row),
                               pl.BlockSpec((tm, d), row), pl.BlockSpec((tm, d), row),
                               pl.BlockSpec((tm, LANES), row)],
        scratch_shapes=[pltpu.VMEM((1, LANES), F32)] + _weight_scratch(d, 4 * d) + scratch,
    )
    outs = pl.pallas_call(
        kern,
        grid_spec=grid_spec,
        out_shape=out_shape + [jax.ShapeDtypeStruct((t, d), BF16), jax.ShapeDtypeStruct((t, 2 * d), BF16),
                               jax.ShapeDtypeStruct((t, d), BF16), jax.ShapeDtypeStruct((t, d), F32),
                               jax.ShapeDtypeStruct((t, LANES), BF16)],
        compiler_params=_cparams("arbitrary"),
        name="fox_inproj",
    )(*operands, g, w_all, wf, bfr)
    return (outs if fused else [src] + list(outs))


def _fox_attn_kernel(q_ref, a_ref, kb_ref, v_ref, gate_ref, o_ref, *, tq, dh, depth):
    seq = q_ref.shape[0]
    pair = pl.program_id(1)
    lane = lax.broadcasted_iota(I32, (tq, LANES), 1)
    krow = lax.broadcasted_iota(I32, (tq, tq), 0)
    qcol = lax.broadcasted_iota(I32, (tq, tq), 1)
    causal = krow <= qcol
    heads_per_block = LANES // dh
    qmask = [jnp.where((lane >= hh * dh) & (lane < (hh + 1) * dh), 1.0, 0.0).astype(BF16)
             for hh in range(heads_per_block)]
    amask = [jnp.where((lane < 96) & ((lane & 15) == pair * heads_per_block + hh), 1.0, 0.0).astype(BF16)
             for hh in range(heads_per_block)]
    vt = jnp.transpose(v_ref[...])
    chains = [(i, hh) for i in range(seq // tq) for hh in range(heads_per_block)]

    def scores(i, hh):
        off = i * tq
        qa = jnp.concatenate([q_ref[off:off + tq, :] * qmask[hh], a_ref[off:off + tq, :] * amask[hh]], axis=1)
        return lax.dot_general(kb_ref[0:off + tq, :], qa, _CONTRACT_LAST, preferred_element_type=F32)

    def finish(i, hh, st):
        off = i * tq
        sd = jnp.where(causal, st[off:off + tq, :], NEG_INF)
        st = jnp.concatenate([st[0:off, :], sd], axis=0) if off > 0 else sd
        m = jnp.max(st, axis=0, keepdims=True)
        e = jnp.exp(st - m)
        inv = 1.0 / jnp.sum(e, axis=0, keepdims=True)
        return jnp.dot(vt[hh * dh:(hh + 1) * dh, 0:off + tq], e.astype(BF16), preferred_element_type=F32) * inv

    outs = {}
    queue = [scores(*c) for c in chains[:depth]]
    for n, (i, hh) in enumerate(chains):
        cur = queue.pop(0)
        if n + depth < len(chains):
            queue.append(scores(*chains[n + depth]))
        outs[hh] = finish(i, hh, cur)
        if hh == heads_per_block - 1:
            off = i * tq
            o = jnp.transpose(jnp.concatenate([outs[h2] for h2 in range(heads_per_block)], axis=0))
            o_ref[off:off + tq, :] = (o * _sigmoid(gate_ref[off:off + tq, :])).astype(BF16)


def _fox_attn(q, a, kb, v, gate, *, batch, seq, tq=256, depth=3):
    t, d = q.shape
    dh = d // FOX_HEADS
    pairs = d // LANES
    kern = functools.partial(_fox_attn_kernel, tq=tq, dh=dh, depth=depth)
    blk = lambda b, p: (b, p)
    return pl.pallas_call(
        kern,
        grid=(batch, pairs),
        in_specs=[pl.BlockSpec((seq, LANES), blk), pl.BlockSpec((seq, LANES), lambda b, p: (b, 0)),
                  pl.BlockSpec((seq, 2 * LANES), blk), pl.BlockSpec((seq, LANES), blk),
                  pl.BlockSpec((seq, LANES), blk)],
        out_specs=pl.BlockSpec((seq, LANES), blk),
        out_shape=jax.ShapeDtypeStruct((t, d), BF16),
        compiler_params=_cparams("arbitrary", "arbitrary"),
        name="fox_attn",
    )(q, a, kb, v, gate)


def _ret_inproj_kernel(*refs, layer, kscale, fused, sub):
    if fused:
        (meta_cur, meta_nxt, h1_ref, route_ref, ys_ref, g_ref, w_hbm, cos_ref, sin_ref, hout_ref, q_ref, k_ref,
         v_ref, gate_ref, wb_ref, stage_ref, sem, buf_ref, gsem) = refs
        _issue_residual_gathers(meta_cur, meta_nxt, ys_ref, buf_ref, gsem, tm=h1_ref.shape[0], sub=sub)
    else:
        h_ref, g_ref, w_hbm, cos_ref, sin_ref, q_ref, k_ref, v_ref, gate_ref, wb_ref, stage_ref, sem = refs
    tm, d = q_ref.shape

    @pl.when(pl.program_id(0) == 0)
    def _():
        _stage_weight(w_hbm.at[layer], wb_ref, stage_ref, sem, 6 * d)

    if fused:
        x = _gathered_residual(h1_ref, route_ref, buf_ref, gsem)
        hout_ref[...] = x
    else:
        x = h_ref[...]
    xn = _rms(x, g_ref[...], EPS).astype(BF16)
    cos = cos_ref[...]
    sin = sin_ref[...]
    dk = d // RET_HEADS
    half = dk // 2

    def rot(t, h):
        t1 = t[:, h * dk:h * dk + half]
        t2 = t[:, h * dk + half:(h + 1) * dk]
        return t1 * cos - t2 * sin, t1 * sin + t2 * cos

    q = jnp.dot(xn, wb_ref[:, 0:d], preferred_element_type=F32)
    k = jnp.dot(xn, wb_ref[:, d:2 * d], preferred_element_type=F32)
    for h in range(RET_HEADS):
        q1, q2 = rot(q, h)
        q_ref[:, h * dk:h * dk + half] = q1.astype(BF16)
        q_ref[:, h * dk + half:(h + 1) * dk] = q2.astype(BF16)
        k1, k2 = rot(k, h)
        k_ref[:, h * dk:h * dk + half] = k1 * kscale
        k_ref[:, h * dk + half:(h + 1) * dk] = k2 * kscale
    v_ref[...] = jnp.dot(xn, wb_ref[:, 2 * d:4 * d], preferred_element_type=F32).astype(BF16)
    gate_ref[...] = jnp.dot(xn, wb_ref[:, 4 * d:6 * d], preferred_element_type=F32)


def _ret_inproj(src, g, w_all, layer, cos, sin, *, seq, tm=256):
    t, d = src.h1.shape if isinstance(src, _Pending) else src.shape
    tm = min(tm, seq)
    half = d // RET_HEADS // 2
    fused, sub, operands, in_specs, out_specs, out_shape, scratch = _residual_specs(src, tm, d)
    kern = functools.partial(_ret_inproj_kernel, layer=layer, kscale=float((d // RET_HEADS) ** -0.5),
                             fused=fused, sub=sub)
    row = lambda i: (i, 0)
    const = lambda i: (0, 0)
    pos = lambda i: (i % (seq // tm), 0)
    grid_spec = pltpu.PrefetchScalarGridSpec(
        num_scalar_prefetch=0,
        grid=(t // tm,),
        in_specs=in_specs + [pl.BlockSpec((1, d), const), pl.BlockSpec(memory_space=pl.ANY),
                             pl.BlockSpec((tm, half), pos), pl.BlockSpec((tm, half), pos)],
        out_specs=out_specs + [pl.BlockSpec((tm, d), row), pl.BlockSpec((tm, d), row),
                               pl.BlockSpec((tm, 2 * d), row), pl.BlockSpec((tm, 2 * d), row)],
        scratch_shapes=_weight_scratch(d, 6 * d) + scratch,
    )
    outs = pl.pallas_call(
        kern,
        grid_spec=grid_spec,
        out_shape=out_shape + [jax.ShapeDtypeStruct((t, d), BF16), jax.ShapeDtypeStruct((t, d), F32),
                               jax.ShapeDtypeStruct((t, 2 * d), BF16), jax.ShapeDtypeStruct((t, 2 * d), F32)],
        compiler_params=_cparams("arbitrary"),
        name="ret_inproj",
    )(*operands, g, w_all, cos, sin)
    return (outs if fused else [src] + list(outs))


def _retention_kernel(q_ref, k_ref, v_ref, gate_ref, gain_ref, intra_ref, qd_ref, kd_ref, cd_ref,
                      y_ref, r_ref, *, chunk):
    seq = q_ref.shape[0]
    r_ref[...] = jnp.zeros_like(r_ref)

    def body(c):
        off = c * chunk
        qn = q_ref[pl.ds(off, chunk), :]
        k32 = k_ref[pl.ds(off, chunk), :]
        vn = v_ref[pl.ds(off, chunk), :]
        s = lax.dot_general(qn, k32.astype(BF16), _CONTRACT_LAST, preferred_element_type=F32) * intra_ref[...]
        o = jnp.dot(s.astype(BF16), vn, preferred_element_type=F32)
        r = r_ref[...]
        o = o + jnp.dot(qn, r.astype(BF16), preferred_element_type=F32) * qd_ref[...]
        kdec = (k32 * kd_ref[...]).astype(BF16)
        r_ref[...] = r * cd_ref[...] + lax.dot_general(kdec, vn, _CONTRACT_FIRST, preferred_element_type=F32)
        mu = jnp.mean(o, axis=-1, keepdims=True)
        dlt = o - mu
        var = jnp.mean(dlt * dlt, axis=-1, keepdims=True)
        on = dlt * lax.rsqrt(var + EPS) * gain_ref[...]
        g = gate_ref[pl.ds(off, chunk), :]
        y_ref[pl.ds(off, chunk), :] = ((g * _sigmoid(g)) * on).astype(BF16)

    for c in range(seq // chunk):
        body(c)


def _retention(q, k, v, gate, gain, intra, qd, kd, cd, *, batch, seq):
    t, d = q.shape
    dk = d // RET_HEADS
    dv = v.shape[1] // RET_HEADS
    c = RET_CHUNK
    kern = functools.partial(_retention_kernel, chunk=c)
    blk = lambda b, h: (b, h)
    hd = lambda b, h: (h, 0, 0)
    return pl.pallas_call(
        kern,
        grid=(batch, RET_HEADS),
        in_specs=[pl.BlockSpec((seq, dk), blk), pl.BlockSpec((seq, dk), blk), pl.BlockSpec((seq, dv), blk),
                  pl.BlockSpec((seq, dv), blk), pl.BlockSpec((1, dv), lambda b, h: (0, h)),
                  pl.BlockSpec((None, c, c), hd), pl.BlockSpec((None, c, 1), hd),
                  pl.BlockSpec((None, c, 1), hd), pl.BlockSpec((None, 1, 1), hd)],
        out_specs=pl.BlockSpec((seq, dv), blk),
        out_shape=jax.ShapeDtypeStruct((t, v.shape[1]), BF16),
        scratch_shapes=[pltpu.VMEM((dk, dv), F32)],
        compiler_params=_cparams("arbitrary", "arbitrary"),
        name="retention",
    )(q, k, v, gate, gain, intra, qd, kd, cd)


def _outproj_router_kernel(a_ref, w_hbm, h_ref, g_ref, wrh_ref, wrl_ref, br_ref,
                           h1_ref, hn_ref, route_ref, meta_ref, cnt_ref, wb_ref, stage_ref, sem, *, layer):
    i = pl.program_id(0)
    tm = a_ref.shape[0]
    d = h_ref.shape[1]

    @pl.when(i == 0)
    def _():
        cnt_ref[...] = jnp.zeros_like(cnt_ref)
        _stage_weight(w_hbm.at[layer], wb_ref, stage_ref, sem, d)

    h1 = h_ref[...] + jnp.dot(a_ref[...], wb_ref[...], preferred_element_type=F32)
    h1_ref[...] = h1
    hn = _rms(h1, g_ref[...], EPS)
    _rows_to_slabs(hn_ref, hn)

    hh = hn.astype(BF16)
    hl = (hn - hh.astype(F32)).astype(BF16)
    lg = (jnp.dot(hh, wrh_ref[...], preferred_element_type=F32)
          + jnp.dot(hh, wrl_ref[...], preferred_element_type=F32)
          + jnp.dot(hl, wrh_ref[...], preferred_element_type=F32)) + br_ref[...]

    lane = lax.broadcasted_iota(I32, (tm, LANES), 1)
    lanef = lane.astype(F32)
    big = float(LANES)

    def softmax_masked(mask):
        mx = jnp.max(jnp.where(mask, lg, -jnp.inf), axis=1, keepdims=True)
        e = jnp.where(mask, jnp.exp(lg - mx), 0.0)
        return e / jnp.sum(e, axis=1, keepdims=True)

    def top1(p, mask):
        best = jnp.max(jnp.where(mask, p, -1.0), axis=1, keepdims=True)
        idx = jnp.min(jnp.where(mask & (p == best), lanef, big), axis=1, keepdims=True)
        return best, idx

    gmask = lane < N_GROUPS
    pg, gi = top1(softmax_masked(gmask), gmask)
    lo = ROUTE_LANE0 + gi * EXPERTS_PER_GROUP
    emask = (lanef >= lo) & (lanef < lo + EXPERTS_PER_GROUP)
    pe = softmax_masked(emask)
    p1, i1 = top1(pe, emask)
    p2, i2 = top1(pe, emask & (lanef != i1))
    den = p1 + p2
    g1 = pg * p1 / den
    g2 = pg * p2 / den

    oh1 = lanef == i1
    oh2 = lanef == i2
    both = jnp.where(oh1 | oh2, 1.0, 0.0)
    row = lax.broadcasted_iota(I32, (tm, tm), 0)
    col = lax.broadcasted_iota(I32, (tm, tm), 1)
    strict = jnp.where(col < row, 1.0, 0.0).astype(BF16)
    before = jnp.dot(strict, both.astype(BF16), preferred_element_type=F32) + cnt_ref[0:1, :]
    r1 = jnp.sum(jnp.where(oh1, before, 0.0), axis=1, keepdims=True)
    r2 = jnp.sum(jnp.where(oh2, before, 0.0), axis=1, keepdims=True)
    cnt_ref[0:1, :] = cnt_ref[0:1, :] + jnp.sum(both, axis=0, keepdims=True)

    e1 = i1 - ROUTE_LANE0
    e2 = i2 - ROUTE_LANE0
    route = jnp.where(lane == 0, e1, jnp.where(lane == 1, e2, jnp.where(
        lane == 2, r1, jnp.where(lane == 3, r2, jnp.where(lane == 4, g1, jnp.where(lane == 5, g2, 0.0))))))
    route_ref[...] = route
    meta_ref[...] = jnp.transpose(route)[0:SUBLANES, :].astype(I32)


def _outproj_router(a, w_all, layer, h, g, wrh, wrl, br, *, tm):
    t, kdim = a.shape
    d = h.shape[1]
    row = lambda i: (i, 0)
    const = lambda i: (0, 0)
    kern = functools.partial(_outproj_router_kernel, layer=layer)
    return pl.pallas_call(
        kern,
        grid=(t // tm,),
        in_specs=[pl.BlockSpec((tm, kdim), row), pl.BlockSpec(memory_space=pl.ANY), pl.BlockSpec((tm, d), row),
                  pl.BlockSpec((1, d), const), pl.BlockSpec((d, LANES), const), pl.BlockSpec((d, LANES), const),
                  pl.BlockSpec((1, LANES), const)],
        out_specs=[pl.BlockSpec((tm, d), row), pl.BlockSpec((tm * d // LANES, LANES), row),
                   pl.BlockSpec((tm, LANES), row), pl.BlockSpec((None, SUBLANES, tm), lambda i: (i, 0, 0)),
                   pl.BlockSpec((SUBLANES, LANES), const)],
        out_shape=[jax.ShapeDtypeStruct((t, d), F32), jax.ShapeDtypeStruct((t * d // LANES, LANES), F32),
                   jax.ShapeDtypeStruct((t, LANES), F32), jax.ShapeDtypeStruct((t // tm, SUBLANES, tm), I32),
                   jax.ShapeDtypeStruct((SUBLANES, LANES), F32)],
        scratch_shapes=_weight_scratch(kdim, d),
        compiler_params=_cparams("arbitrary"),
        name="outproj_router",
    )(a, w_all, h, g, wrh, wrl, br)


def _pad_copies(counts_ref, pstart_ref, zero_ref, xs_ref, sem, *, bm):
    nslab = SUBLANES
    out = []
    for e in range(N_EXPERTS):
        cnt = counts_ref[e]
        npad = (bm - (cnt & (bm - 1))) & (bm - 1)
        pos = pstart_ref[e] + cnt
        bit = 1
        while bit < bm:
            out.append(((npad & bit) != 0, pltpu.make_async_copy(
                _slab_rows(zero_ref, 0, bit, nslab), _slab_rows(xs_ref, pos, bit, nslab), sem)))
            pos = pos + (npad & bit)
            bit *= 2
    return out


def _dispatch_kernel(counts_ref, pstart_ref, nused_ref, meta_ref, hn_ref, xs_ref, zero_ref, sem, psem, *, bm):
    nslab = SUBLANES
    tm = hn_ref.shape[0] // nslab
    half = bm // 2

    @pl.when(pl.program_id(0) == 0)
    def _():
        zero_ref[...] = jnp.zeros_like(zero_ref)
        pads = _pad_copies(counts_ref, pstart_ref, zero_ref, xs_ref, psem, bm=bm)
        for pred, cp in pads:
            pl.when(pred)(cp.start)

        def tail_copy(j):
            return pltpu.make_async_copy(zero_ref, _slab_rows(xs_ref, j * half, half, nslab), psem)

        first = nused_ref[0] * 2
        last = xs_ref.shape[0] // (half * nslab)
        lax.fori_loop(first, last, lambda j, c: (tail_copy(j).start(), c)[1], 0)
        for pred, cp in pads:
            pl.when(pred)(cp.wait)
        lax.fori_loop(first, last, lambda j, c: (tail_copy(j).wait(), c)[1], 0)

    def row_copy(t, k):
        dst = meta_ref[0, TOP_K * t + k]
        return pltpu.make_async_copy(_slab_rows(hn_ref, t, 1, nslab), _slab_at(xs_ref, dst, nslab), sem)

    def issue(t, carry):
        for k in range(TOP_K):
            row_copy(t, k).start(priority=k)
        return carry

    lax.fori_loop(0, tm, issue, 0, unroll=8)

    def drain(t, carry):
        for k in range(TOP_K):
            row_copy(t, k).wait()
        return carry

    lax.fori_loop(0, tm, drain, 0, unroll=8)


def _dispatch(counts, pstart, n_used, meta, hn, n_rows, *, bm):
    nslab = SUBLANES
    t = hn.shape[0] // nslab
    tm = meta.shape[2] // TOP_K
    kern = functools.partial(_dispatch_kernel, bm=bm)
    grid_spec = pltpu.PrefetchScalarGridSpec(
        num_scalar_prefetch=3,
        grid=(t // tm,),
        in_specs=[pl.BlockSpec((None, 1, TOP_K * tm), lambda i, *_: (i, 0, 0), memory_space=pltpu.SMEM),
                  pl.BlockSpec((tm * nslab, LANES), lambda i, *_: (i, 0))],
        out_specs=pl.BlockSpec(memory_space=pl.ANY),
        scratch_shapes=[pltpu.VMEM((bm // 2 * nslab, LANES), F32), pltpu.SemaphoreType.DMA(()),
                        pltpu.SemaphoreType.DMA(())],
    )
    return pl.pallas_call(
        kern,
        grid_spec=grid_spec,
        out_shape=jax.ShapeDtypeStruct((n_rows * nslab, LANES), F32),
        compiler_params=_cparams("arbitrary"),
        name="moe_dispatch",
    )(counts, pstart, n_used, meta, hn)


def _experts_kernel(be_ref, nu_ref, x_ref, wgu_ref, wd_ref, y_ref, wgu_b, wd_b):
    i = pl.program_id(0)
    d = wgu_b.shape[0]
    bm = x_ref.shape[0] * LANES // d
    prev = be_ref[jnp.maximum(i - 1, 0)]

    @pl.when((i < nu_ref[0]) & ((i == 0) | (be_ref[i] != prev)))
    def _():
        wgu_b[...] = wgu_ref[...].astype(BF16)
        wd_b[...] = wd_ref[...].astype(BF16)

    @pl.when(i < nu_ref[0])
    def _():
        x = _slabs_to_rows(x_ref, bm, d // LANES).astype(BF16)
        hcat = jnp.dot(x, wgu_b[...], preferred_element_type=F32)
        ff = hcat.shape[1] // 2
        a = hcat[:, :ff]
        b = hcat[:, ff:]
        act = ((a * _sigmoid(a)) * b).astype(BF16)
        _rows_to_slabs(y_ref, jnp.dot(act, wd_b[...], preferred_element_type=F32))

    @pl.when(i >= nu_ref[0])
    def _():
        y_ref[...] = jnp.zeros_like(y_ref)


def _experts(block_e, n_used, xs, wgu_all, wd_all, layer, *, bm):
    d = wgu_all.shape[2]
    nslab = d // LANES
    n_rows = xs.shape[0] // nslab
    ff2 = wgu_all.shape[3]
    grid_spec = pltpu.PrefetchScalarGridSpec(
        num_scalar_prefetch=2,
        grid=(n_rows // bm,),
        in_specs=[pl.BlockSpec((bm * nslab, LANES), lambda i, be, nu: (i, 0)),
                  pl.BlockSpec((None, None, d, ff2), lambda i, be, nu: (layer, be[i], 0, 0)),
                  pl.BlockSpec((None, None, ff2 // 2, d), lambda i, be, nu: (layer, be[i], 0, 0))],
        out_specs=pl.BlockSpec((bm * nslab, LANES), lambda i, be, nu: (i, 0)),
        scratch_shapes=[pltpu.VMEM((d, ff2), BF16), pltpu.VMEM((ff2 // 2, d), BF16)],
    )
    return pl.pallas_call(
        _experts_kernel,
        grid_spec=grid_spec,
        out_shape=jax.ShapeDtypeStruct((n_rows * nslab, LANES), F32),
        compiler_params=_cparams("arbitrary"),
        name="moe_experts",
    )(block_e, n_used, xs, wgu_all, wd_all)


def _combine_kernel(meta_ref, h1_ref, route_ref, gfin_ref, ys_ref, o_ref, buf_ref, sem, *, final_norm):
    tm, d = h1_ref.shape
    nslab = d // LANES

    def row_copy(t, k):
        src = meta_ref[0, TOP_K * t + k]
        return pltpu.make_async_copy(_slab_at(ys_ref, src, nslab), _slab_rows(buf_ref.at[k], t, 1, nslab), sem)

    def issue(t, carry):
        for k in range(TOP_K):
            row_copy(t, k).start(priority=k)
        return carry

    lax.fori_loop(0, tm, issue, 0, unroll=8)

    def drain(t, carry):
        for k in range(TOP_K):
            row_copy(t, k).wait()
        return carry

    lax.fori_loop(0, tm, drain, 0, unroll=8)

    g0 = route_ref[:, 4:5]
    g1 = route_ref[:, 5:6]
    out = h1_ref[...] + (_slabs_to_rows(buf_ref.at[0], tm, nslab) * g0 + _slabs_to_rows(buf_ref.at[1], tm, nslab) * g1)
    if final_norm:
        out = _rms(out, gfin_ref[...], EPS)
    o_ref[...] = out


def _combine(meta, h1, route, gfin, ys, *, final_norm):
    t, d = h1.shape
    tm = meta.shape[2] // TOP_K
    kern = functools.partial(_combine_kernel, final_norm=final_norm)
    grid_spec = pltpu.PrefetchScalarGridSpec(
        num_scalar_prefetch=0,
        grid=(t // tm,),
        in_specs=[pl.BlockSpec((None, 1, TOP_K * tm), lambda i: (i, 0, 0), memory_space=pltpu.SMEM),
                  pl.BlockSpec((tm, d), lambda i: (i, 0)),
                  pl.BlockSpec((tm, LANES), lambda i: (i, 0)),
                  pl.BlockSpec((1, d), lambda i: (0, 0)),
                  pl.BlockSpec(memory_space=pl.ANY)],
        out_specs=pl.BlockSpec((tm, d), lambda i: (i, 0)),
        scratch_shapes=[pltpu.VMEM((TOP_K, tm * d // LANES, LANES), F32), pltpu.SemaphoreType.DMA(())],
    )
    return pl.pallas_call(
        kern,
        grid_spec=grid_spec,
        out_shape=jax.ShapeDtypeStruct((t, d), F32),
        compiler_params=_cparams("arbitrary"),
        name="moe_combine",
    )(meta, h1, route, gfin, ys)


def _moe(a, w_out_all, mixer_layer, h, layer, g_ffn, wr_g, br_g, wr_e, br_e, w_gu_all, w_down_all, gfin,
         *, final_norm):
    t, d = h.shape
    bm = MOE_BLOCK
    tm = min(ROW_TILE, t)
    wr = jnp.concatenate([wr_g, wr_e, jnp.zeros((d, LANES - N_GROUPS - N_EXPERTS), F32)], axis=1)
    wrh = wr.astype(BF16)
    wrl = (wr - wrh.astype(F32)).astype(BF16)
    br = jnp.concatenate([br_g, br_e, jnp.zeros((LANES - N_GROUPS - N_EXPERTS,), F32)])[None, :]
    h1, hn, route, meta, cnt = _outproj_router(a, w_out_all, mixer_layer, h, g_ffn[None, :], wrh, wrl, br, tm=tm)

    counts = cnt[0, ROUTE_LANE0:ROUTE_LANE0 + N_EXPERTS].astype(I32)
    nb = (counts + bm - 1) // bm
    nb_end = jnp.cumsum(nb)
    pstart = ((nb_end - nb) * bm).astype(I32)
    n_blocks = (t * TOP_K + N_EXPERTS * (bm - 1)) // bm
    j = jnp.arange(n_blocks, dtype=I32)
    block_e = jnp.minimum(jnp.sum((nb_end[None, :] <= j[:, None]).astype(I32), axis=1), N_EXPERTS - 1)
    n_used = nb_end[-1:].astype(I32)

    eid = meta[:, 0:TOP_K, :]
    base = jnp.sum(jnp.where(eid[..., None] == jnp.arange(N_EXPERTS, dtype=I32), pstart, 0), axis=-1)
    rows = (meta[:, TOP_K:2 * TOP_K, :] + base) * (d // LANES)
    rows = rows.transpose(0, 2, 1).reshape(rows.shape[0], 1, TOP_K * rows.shape[2])

    xs = _dispatch(counts, pstart, n_used, rows, hn, n_blocks * bm, bm=bm)
    ys = _experts(block_e, n_used, xs, w_gu_all, w_down_all, layer, bm=bm)
    if final_norm:
        return _combine(rows, h1, route, gfin[None, :], ys, final_norm=True)
    return _Pending(rows, h1, route, ys)


def _fox_mixer(src, g, w_in_all, layer, b_f, *, batch, seq):
    d = w_in_all.shape[1]
    nh = FOX_HEADS
    wf = w_in_all[layer, :, 4 * d:4 * d + nh]
    wf = jnp.concatenate([wf] * 6 + [jnp.zeros((d, LANES - 6 * nh), F32)], axis=1).astype(BF16)
    bfr = jnp.concatenate([b_f] * 6 + [jnp.zeros((LANES - 6 * nh,), F32)])[None, :]
    h, q, kb, v, gate, a = _fox_inproj(src, g[None, :], w_in_all, layer, wf, bfr, seq=seq)
    return h, _fox_attn(q, a, kb, v, gate, batch=batch, seq=seq)


def _ret_mixer(src, g, w_in_all, layer, gn_gain, *, batch, seq):
    d = w_in_all.shape[1]
    nh = RET_HEADS
    dk = d // nh
    c = RET_CHUNK
    inv = 1.0 / (ROPE_BASE ** jnp.linspace(0.0, 1.0, dk // 2, dtype=F32))
    ang = jnp.arange(seq).astype(F32)[:, None] * inv[None, :]
    cos, sin = jnp.cos(ang), jnp.sin(ang)
    log_g = jnp.log(1.0 - 2.0 ** (-5.0 - jnp.arange(nh, dtype=F32)))
    idx = jnp.arange(c, dtype=F32)
    rel = idx[:, None] - idx[None, :]
    intra = jnp.where(rel >= 0, jnp.exp(log_g[:, None, None] * jnp.maximum(rel, 0.0)), 0.0)
    qd = jnp.exp(log_g[:, None] * (idx + 1.0))[:, :, None]
    kd = jnp.exp(log_g[:, None] * (c - 1.0 - idx))[:, :, None]
    cd = jnp.exp(log_g * c)[:, None, None]
    h, q, k, v, gate = _ret_inproj(src, g[None, :], w_in_all, layer, cos, sin, seq=seq)
    return h, _retention(q, k, v, gate, gn_gain[None, :], intra, qd, kd, cd, batch=batch, seq=seq)


def kernel(x, fox_w_in, fox_b_f, fox_w_out, ret_w_in, ret_gn_gain, ret_w_out, norm_mix, norm_ffn,
           router_group_w, router_group_b, router_expert_w, router_expert_b, expert_w_gu, expert_w_down,
           norm_final):
    batch, seq, d = x.shape
    depth = norm_mix.shape[0]
    src = x.reshape(batch * seq, d)
    for i in range(depth):
        j = i // 2
        if i % 2 == 0:
            h, a = _fox_mixer(src, norm_mix[i], fox_w_in, j, fox_b_f[j], batch=batch, seq=seq)
            w_out_all = fox_w_out
        else:
            h, a = _ret_mixer(src, norm_mix[i], ret_w_in, j, ret_gn_gain[j], batch=batch, seq=seq)
            w_out_all = ret_w_out
        src = _moe(a, w_out_all, j, h, i, norm_ffn[i], router_group_w[i], router_group_b[i], router_expert_w[i],
                   router_expert_b[i], expert_w_gu, expert_w_down, norm_final, final_norm=(i == depth - 1))
    return src.reshape(batch, seq, d)
```

```python
import functools
from typing import NamedTuple

import jax
import jax.numpy as jnp
from jax import lax
from jax.experimental import pallas as pl
from jax.experimental.pallas import tpu as pltpu

F32 = jnp.float32
BF16 = jnp.bfloat16
I32 = jnp.int32

FOX_HEADS = 16
RET_HEADS = 4
RET_CHUNK = 128
N_GROUPS = 4
EXPERTS_PER_GROUP = 8
N_EXPERTS = N_GROUPS * EXPERTS_PER_GROUP
TOP_K = 2
EPS = 1e-6
NEG_INF = -1e30
ROPE_BASE = 10000.0

LANES = 128
SUBLANES = 8
ROUTE_LANE0 = N_GROUPS
VMEM_LIMIT = 56 * 1024 * 1024
MOE_BLOCK = 512
ROW_TILE = 512
WEIGHT_CHUNK = 512

_CONTRACT_LAST = (((1,), (1,)), ((), ()))
_CONTRACT_FIRST = (((0,), (0,)), ((), ()))


def _cparams(*sem):
    return pltpu.CompilerParams(dimension_semantics=sem, vmem_limit_bytes=VMEM_LIMIT)


def _sigmoid(x):
    return 1.0 / (1.0 + jnp.exp(-x))


def _rms(x, g, eps):
    ms = jnp.mean(x * x, axis=-1, keepdims=True)
    return x * lax.rsqrt(ms + eps) * g


def _split3(x):
    hi = x.astype(BF16)
    r = x - hi.astype(F32)
    mid = r.astype(BF16)
    lo = (r - mid.astype(F32)).astype(BF16)
    return hi, mid, lo


def _stage_weight(w_hbm, wb_ref, stage_ref, sem, ncols):
    chunk = stage_ref.shape[2]
    n = ncols // chunk

    def cp(c):
        return pltpu.make_async_copy(w_hbm.at[:, pl.ds(c * chunk, chunk)], stage_ref.at[c % 2], sem.at[c % 2])

    cp(0).start()
    for c in range(n):
        if c + 1 < n:
            cp(c + 1).start()
        cp(c).wait()
        wb_ref[:, c * chunk:(c + 1) * chunk] = stage_ref[c % 2].astype(BF16)


def _rows_to_slabs(ref, x):
    n, d = x.shape
    nslab = d // LANES
    for c in range(nslab):
        ref[pl.ds(c, n, stride=nslab), :] = x[:, c * LANES:(c + 1) * LANES]


def _slabs_to_rows(ref, n, nslab):
    return jnp.concatenate([ref[pl.ds(c, n, stride=nslab), :] for c in range(nslab)], axis=1)


def _slab_rows(ref, row, nrows, nslab):
    return ref.at[pl.ds(pl.multiple_of(row * nslab, nslab), nrows * nslab), :]


def _slab_at(ref, first, nslab):
    return ref.at[pl.ds(pl.multiple_of(first, nslab), nslab), :]


class _Pending(NamedTuple):
    meta: jax.Array
    h1: jax.Array
    route: jax.Array
    ys: jax.Array


def _issue_residual_gathers(meta_cur, meta_nxt, ys_ref, buf_ref, gsem, *, tm, sub):
    i = pl.program_id(0)
    nslab = buf_ref.shape[2] // tm

    def issue(meta_ref, step):
        slot = step % 2
        off = (step % sub) * tm

        def body(t, carry):
            for k in range(TOP_K):
                src = meta_ref[0, TOP_K * (off + t) + k]
                pltpu.make_async_copy(_slab_at(ys_ref, src, nslab), _slab_rows(buf_ref.at[slot, k], t, 1, nslab),
                                      gsem.at[slot]).start(priority=k)
            return carry

        lax.fori_loop(0, tm, body, 0, unroll=8)

    @pl.when(i == 0)
    def _():
        issue(meta_cur, i)

    @pl.when(i + 1 < pl.num_programs(0))
    def _():
        issue(meta_nxt, i + 1)


def _gathered_residual(h1_ref, route_ref, buf_ref, gsem):
    tm, d = h1_ref.shape
    nslab = d // LANES
    slot = pl.program_id(0) % 2
    pltpu.make_async_copy(buf_ref.at[slot], buf_ref.at[slot], gsem.at[slot]).wait()
    g0 = route_ref[:, 4:5]
    g1 = route_ref[:, 5:6]
    return h1_ref[...] + (_slabs_to_rows(buf_ref.at[slot, 0], tm, nslab) * g0
                          + _slabs_to_rows(buf_ref.at[slot, 1], tm, nslab) * g1)


def _residual_specs(src, tm, d):
    if not isinstance(src, _Pending):
        return False, 1, [src], [pl.BlockSpec((tm, d), lambda i: (i, 0))], [], [], []
    tile = src.meta.shape[2] // TOP_K
    sub = tile // tm
    last = src.h1.shape[0] // tm - 1
    smem = lambda f: pl.BlockSpec((None, 1, TOP_K * tile), f, memory_space=pltpu.SMEM)
    in_specs = [smem(lambda i: (i // sub, 0, 0)), smem(lambda i: (jnp.minimum(i + 1, last) // sub, 0, 0)),
                pl.BlockSpec((tm, d), lambda i: (i, 0)), pl.BlockSpec((tm, LANES), lambda i: (i, 0)),
                pl.BlockSpec(memory_space=pl.ANY)]
    scratch = [pltpu.VMEM((2, TOP_K, tm * d // LANES, LANES), F32), pltpu.SemaphoreType.DMA((2,))]
    out_specs = [pl.BlockSpec((tm, d), lambda i: (i, 0))]
    out_shape = [jax.ShapeDtypeStruct(src.h1.shape, F32)]
    return True, sub, [src.meta, src.meta, src.h1, src.route, src.ys], in_specs, out_specs, out_shape, scratch


def _weight_scratch(kdim, ncols):
    return [pltpu.VMEM((kdim, ncols), BF16), pltpu.VMEM((2, kdim, WEIGHT_CHUNK), F32),
            pltpu.SemaphoreType.DMA((2,))]


def _fox_inproj_kernel(*refs, layer, tiles_per_seq, scale, fused, sub):
    i = pl.program_id(0)
    if fused:
        (meta_cur, meta_nxt, h1_ref, route_ref, ys_ref, g_ref, w_hbm, wf_ref, bf_ref, hout_ref, q_ref, kb_ref,
         v_ref, gate_ref, a_ref, carry_ref, wb_ref, stage_ref, sem, buf_ref, gsem) = refs
        _issue_residual_gathers(meta_cur, meta_nxt, ys_ref, buf_ref, gsem, tm=h1_ref.shape[0], sub=sub)
    else:
        (h_ref, g_ref, w_hbm, wf_ref, bf_ref, q_ref, kb_ref, v_ref, gate_ref, a_ref,
         carry_ref, wb_ref, stage_ref, sem) = refs
    tm, d = q_ref.shape

    @pl.when(i == 0)
    def _():
        _stage_weight(w_hbm.at[layer], wb_ref, stage_ref, sem, 4 * d)

    if fused:
        x = _gathered_residual(h1_ref, route_ref, buf_ref, gsem)
        hout_ref[...] = x
    else:
        x = h_ref[...]
    xn = _rms(x, g_ref[...], EPS).astype(BF16)

    q = jnp.dot(xn, wb_ref[:, 0:d], preferred_element_type=F32)
    q_ref[...] = (q * scale).astype(BF16)
    v_ref[...] = jnp.dot(xn, wb_ref[:, 2 * d:3 * d], preferred_element_type=F32).astype(BF16)
    gate_ref[...] = jnp.dot(xn, wb_ref[:, 3 * d:4 * d], preferred_element_type=F32)

    f = jnp.dot(xn, wf_ref[...], preferred_element_type=F32) + bf_ref[...]
    ls = jnp.minimum(f, 0.0) - jnp.log1p(jnp.exp(-jnp.abs(f)))

    @pl.when(i % tiles_per_seq == 0)
    def _():
        carry_ref[...] = jnp.zeros_like(carry_ref)

    row = lax.broadcasted_iota(I32, (tm, tm), 0)
    col = lax.broadcasted_iota(I32, (tm, tm), 1)
    tri = jnp.where(col <= row, 1.0, 0.0).astype(BF16)
    hi, mid, lo = _split3(ls)
    cum = (jnp.dot(tri, hi, preferred_element_type=F32) + jnp.dot(tri, mid, preferred_element_type=F32)
           + jnp.dot(tri, lo, preferred_element_type=F32)) + carry_ref[...]
    carry_ref[...] = cum[tm - 1:tm, :]

    c_hi, c_mid, c_lo = (c.astype(F32) for c in _split3(cum))
    lane = lax.broadcasted_iota(I32, (tm, LANES), 1)
    a_ref[...] = jnp.where(lane < 16, c_hi, jnp.where(lane < 32, c_mid, jnp.where(
        lane < 48, c_lo, jnp.where(lane < 96, 1.0, 0.0)))).astype(BF16)
    bk = jnp.where(lane < 48, 1.0, jnp.where(lane < 64, -c_hi, jnp.where(
        lane < 80, -c_mid, jnp.where(lane < 96, -c_lo, 0.0)))).astype(BF16)

    k = jnp.dot(xn, wb_ref[:, d:2 * d], preferred_element_type=F32).astype(BF16)
    for p in range(d // LANES):
        kb_ref[:, 2 * p * LANES:(2 * p + 1) * LANES] = k[:, p * LANES:(p + 1) * LANES]
        kb_ref[:, (2 * p + 1) * LANES:(2 * p + 2) * LANES] = bk


def _fox_inproj(src, g, w_all, layer, wf, bfr, *, seq):
    t, d = src.h1.shape if isinstance(src, _Pending) else src.shape
    tm = min(ROW_TILE, seq)
    scale = float((d // FOX_HEADS) ** -0.5)
    fused, sub, operands, in_specs, out_specs, out_shape, scratch = _residual_specs(src, tm, d)
    kern = functools.partial(_fox_inproj_kernel, layer=layer, tiles_per_seq=seq // tm, scale=scale,
                             fused=fused, sub=sub)
    row = lambda i: (i, 0)
    const = lambda i: (0, 0)
    grid_spec = pltpu.PrefetchScalarGridSpec(
        num_scalar_prefetch=0,
        grid=(t // tm,),
        in_specs=in_specs + [pl.BlockSpec((1, d), const), pl.BlockSpec(memory_space=pl.ANY),
                             pl.BlockSpec((d, LANES), const), pl.BlockSpec((1, LANES), const)],
        out_specs=out_specs + [pl.BlockSpec((tm, d), row), pl.BlockSpec((tm, 2 * d), row),
                               pl.BlockSpec((tm, d), row), pl.BlockSpec((tm, d), row),
                               pl.BlockSpec((tm, LANES), row)],
        scratch_shapes=[pltpu.VMEM((1, LANES), F32)] + _weight_scratch(d, 4 * d) + scratch,
    )
    outs = pl.pallas_call(
        kern,
        grid_spec=grid_spec,
        out_shape=out_shape + [jax.ShapeDtypeStruct((t, d), BF16), jax.ShapeDtypeStruct((t, 2 * d), BF16),
                               jax.ShapeDtypeStruct((t, d), BF16), jax.ShapeDtypeStruct((t, d), F32),
                               jax.ShapeDtypeStruct((t, LANES), BF16)],
        compiler_params=_cparams("arbitrary"),
        name="fox_inproj",
    )(*operands, g, w_all, wf, bfr)
    return (outs if fused else [src] + list(outs))


def _fox_attn_kernel(q_ref, a_ref, kb_ref, v_ref, gate_ref, o_ref, *, tq, dh, depth):
    seq = q_ref.shape[0]
    pair = pl.program_id(1)
    lane = lax.broadcasted_iota(I32, (tq, LANES), 1)
    krow = lax.broadcasted_iota(I32, (tq, tq), 0)
    qcol = lax.broadcasted_iota(I32, (tq, tq), 1)
    causal = krow <= qcol
    heads_per_block = LANES // dh
    qmask = [jnp.where((lane >= hh * dh) & (lane < (hh + 1) * dh), 1.0, 0.0).astype(BF16)
             for hh in range(heads_per_block)]
    amask = [jnp.where((lane < 96) & ((lane & 15) == pair * heads_per_block + hh), 1.0, 0.0).astype(BF16)
             for hh in range(heads_per_block)]
    vt = jnp.transpose(v_ref[...])
    chains = [(i, hh) for i in range(seq // tq) for hh in range(heads_per_block)]

    def scores(i, hh):
        off = i * tq
        qa = jnp.concatenate([q_ref[off:off + tq, :] * qmask[hh], a_ref[off:off + tq, :] * amask[hh]], axis=1)
        return lax.dot_general(kb_ref[0:off + tq, :], qa, _CONTRACT_LAST, preferred_element_type=F32)

    def finish(i, hh, st):
        off = i * tq
        sd = jnp.where(causal, st[off:off + tq, :], NEG_INF)
        st = jnp.concatenate([st[0:off, :], sd], axis=0) if off > 0 else sd
        m = jnp.max(st, axis=0, keepdims=True)
        e = jnp.exp(st - m)
        inv = 1.0 / jnp.sum(e, axis=0, keepdims=True)
        return jnp.dot(vt[hh * dh:(hh + 1) * dh, 0:off + tq], e.astype(BF16), preferred_element_type=F32) * inv

    outs = {}
    queue = [scores(*c) for c in chains[:depth]]
    for n, (i, hh) in enumerate(chains):
        cur = queue.pop(0)
        if n + depth < len(chains):
            queue.append(scores(*chains[n + depth]))
        outs[hh] = finish(i, hh, cur)
        if hh == heads_per_block - 1:
            off = i * tq
            o = jnp.transpose(jnp.concatenate([outs[h2] for h2 in range(heads_per_block)], axis=0))
            o_ref[off:off + tq, :] = (o * _sigmoid(gate_ref[off:off + tq, :])).astype(BF16)


def _fox_attn(q, a, kb, v, gate, *, batch, seq, tq=256, depth=3):
    t, d = q.shape
    dh = d // FOX_HEADS
    pairs = d // LANES
    kern = functools.partial(_fox_attn_kernel, tq=tq, dh=dh, depth=depth)
    blk = lambda b, p: (b, p)
    return pl.pallas_call(
        kern,
        grid=(batch, pairs),
        in_specs=[pl.BlockSpec((seq, LANES), blk), pl.BlockSpec((seq, LANES), lambda b, p: (b, 0)),
                  pl.BlockSpec((seq, 2 * LANES), blk), pl.BlockSpec((seq, LANES), blk),
                  pl.BlockSpec((seq, LANES), blk)],
        out_specs=pl.BlockSpec((seq, LANES), blk),
        out_shape=jax.ShapeDtypeStruct((t, d), BF16),
        compiler_params=_cparams("arbitrary", "arbitrary"),
        name="fox_attn",
    )(q, a, kb, v, gate)


def _ret_inproj_kernel(*refs, layer, kscale, fused, sub):
    if fused:
        (meta_cur, meta_nxt, h1_ref, route_ref, ys_ref, g_ref, w_hbm, cos_ref, sin_ref, hout_ref, q_ref, k_ref,
         v_ref, gate_ref, wb_ref, stage_ref, sem, buf_ref, gsem) = refs
        _issue_residual_gathers(meta_cur, meta_nxt, ys_ref, buf_ref, gsem, tm=h1_ref.shape[0], sub=sub)
    else:
        h_ref, g_ref, w_hbm, cos_ref, sin_ref, q_ref, k_ref, v_ref, gate_ref, wb_ref, stage_ref, sem = refs
    tm, d = q_ref.shape

    @pl.when(pl.program_id(0) == 0)
    def _():
        _stage_weight(w_hbm.at[layer], wb_ref, stage_ref, sem, 6 * d)

    if fused:
        x = _gathered_residual(h1_ref, route_ref, buf_ref, gsem)
        hout_ref[...] = x
    else:
        x = h_ref[...]
    xn = _rms(x, g_ref[...], EPS).astype(BF16)
    cos = cos_ref[...]
    sin = sin_ref[...]
    dk = d // RET_HEADS
    half = dk // 2

    def rot(t, h):
        t1 = t[:, h * dk:h * dk + half]
        t2 = t[:, h * dk + half:(h + 1) * dk]
        return t1 * cos - t2 * sin, t1 * sin + t2 * cos

    q = jnp.dot(xn, wb_ref[:, 0:d], preferred_element_type=F32)
    k = jnp.dot(xn, wb_ref[:, d:2 * d], preferred_element_type=F32)
    for h in range(RET_HEADS):
        q1, q2 = rot(q, h)
        q_ref[:, h * dk:h * dk + half] = q1.astype(BF16)
        q_ref[:, h * dk + half:(h + 1) * dk] = q2.astype(BF16)
        k1, k2 = rot(k, h)
        k_ref[:, h * dk:h * dk + half] = k1 * kscale
        k_ref[:, h * dk + half:(h + 1) * dk] = k2 * kscale
    v_ref[...] = jnp.dot(xn, wb_ref[:, 2 * d:4 * d], preferred_element_type=F32).astype(BF16)
    gate_ref[...] = jnp.dot(xn, wb_ref[:, 4 * d:6 * d], preferred_element_type=F32)


def _ret_inproj(src, g, w_all, layer, cos, sin, *, seq, tm=256):
    t, d = src.h1.shape if isinstance(src, _Pending) else src.shape
    tm = min(tm, seq)
    half = d // RET_HEADS // 2
    fused, sub, operands, in_specs, out_specs, out_shape, scratch = _residual_specs(src, tm, d)
    kern = functools.partial(_ret_inproj_kernel, layer=layer, kscale=float((d // RET_HEADS) ** -0.5),
                             fused=fused, sub=sub)
    row = lambda i: (i, 0)
    const = lambda i: (0, 0)
    pos = lambda i: (i % (seq // tm), 0)
    grid_spec = pltpu.PrefetchScalarGridSpec(
        num_scalar_prefetch=0,
        grid=(t // tm,),
        in_specs=in_specs + [pl.BlockSpec((1, d), const), pl.BlockSpec(memory_space=pl.ANY),
                             pl.BlockSpec((tm, half), pos), pl.BlockSpec((tm, half), pos)],
        out_specs=out_specs + [pl.BlockSpec((tm, d), row), pl.BlockSpec((tm, d), row),
                               pl.BlockSpec((tm, 2 * d), row), pl.BlockSpec((tm, 2 * d), row)],
        scratch_shapes=_weight_scratch(d, 6 * d) + scratch,
    )
    outs = pl.pallas_call(
        kern,
        grid_spec=grid_spec,
        out_shape=out_shape + [jax.ShapeDtypeStruct((t, d), BF16), jax.ShapeDtypeStruct((t, d), F32),
                               jax.ShapeDtypeStruct((t, 2 * d), BF16), jax.ShapeDtypeStruct((t, 2 * d), F32)],
        compiler_params=_cparams("arbitrary"),
        name="ret_inproj",
    )(*operands, g, w_all, cos, sin)
    return (outs if fused else [src] + list(outs))


def _retention_kernel(q_ref, k_ref, v_ref, gate_ref, gain_ref, intra_ref, qd_ref, kd_ref, cd_ref,
                      y_ref, r_ref, *, chunk):
    seq = q_ref.shape[0]
    r_ref[...] = jnp.zeros_like(r_ref)

    def body(c):
        off = c * chunk
        qn = q_ref[pl.ds(off, chunk), :]
        k32 = k_ref[pl.ds(off, chunk), :]
        vn = v_ref[pl.ds(off, chunk), :]
        s = lax.dot_general(qn, k32.astype(BF16), _CONTRACT_LAST, preferred_element_type=F32) * intra_ref[...]
        o = jnp.dot(s.astype(BF16), vn, preferred_element_type=F32)
        r = r_ref[...]
        o = o + jnp.dot(qn, r.astype(BF16), preferred_element_type=F32) * qd_ref[...]
        kdec = (k32 * kd_ref[...]).astype(BF16)
        r_ref[...] = r * cd_ref[...] + lax.dot_general(kdec, vn, _CONTRACT_FIRST, preferred_element_type=F32)
        mu = jnp.mean(o, axis=-1, keepdims=True)
        dlt = o - mu
        var = jnp.mean(dlt * dlt, axis=-1, keepdims=True)
        on = dlt * lax.rsqrt(var + EPS) * gain_ref[...]
        g = gate_ref[pl.ds(off, chunk), :]
        y_ref[pl.ds(off, chunk), :] = ((g * _sigmoid(g)) * on).astype(BF16)

    for c in range(seq // chunk):
        body(c)


def _retention(q, k, v, gate, gain, intra, qd, kd, cd, *, batch, seq):
    t, d = q.shape
    dk = d // RET_HEADS
    dv = v.shape[1] // RET_HEADS
    c = RET_CHUNK
    kern = functools.partial(_retention_kernel, chunk=c)
    blk = lambda b, h: (b, h)
    hd = lambda b, h: (h, 0, 0)
    return pl.pallas_call(
        kern,
        grid=(batch, RET_HEADS),
        in_specs=[pl.BlockSpec((seq, dk), blk), pl.BlockSpec((seq, dk), blk), pl.BlockSpec((seq, dv), blk),
                  pl.BlockSpec((seq, dv), blk), pl.BlockSpec((1, dv), lambda b, h: (0, h)),
                  pl.BlockSpec((None, c, c), hd), pl.BlockSpec((None, c, 1), hd),
                  pl.BlockSpec((None, c, 1), hd), pl.BlockSpec((None, 1, 1), hd)],
        out_specs=pl.BlockSpec((seq, dv), blk),
        out_shape=jax.ShapeDtypeStruct((t, v.shape[1]), BF16),
        scratch_shapes=[pltpu.VMEM((dk, dv), F32)],
        compiler_params=_cparams("arbitrary", "arbitrary"),
        name="retention",
    )(q, k, v, gate, gain, intra, qd, kd, cd)


def _outproj_router_kernel(a_ref, w_hbm, h_ref, g_ref, wrh_ref, wrl_ref, br_ref,
                           h1_ref, hn_ref, route_ref, meta_ref, cnt_ref, wb_ref, stage_ref, sem, *, layer):
    i = pl.program_id(0)
    tm = a_ref.shape[0]
    d = h_ref.shape[1]

    @pl.when(i == 0)
    def _():
        cnt_ref[...] = jnp.zeros_like(cnt_ref)
        _stage_weight(w_hbm.at[layer], wb_ref, stage_ref, sem, d)

    h1 = h_ref[...] + jnp.dot(a_ref[...], wb_ref[...], preferred_element_type=F32)
    h1_ref[...] = h1
    hn = _rms(h1, g_ref[...], EPS)
    _rows_to_slabs(hn_ref, hn)

    hh = hn.astype(BF16)
    hl = (hn - hh.astype(F32)).astype(BF16)
    lg = (jnp.dot(hh, wrh_ref[...], preferred_element_type=F32)
          + jnp.dot(hh, wrl_ref[...], preferred_element_type=F32)
          + jnp.dot(hl, wrh_ref[...], preferred_element_type=F32)) + br_ref[...]

    lane = lax.broadcasted_iota(I32, (tm, LANES), 1)
    lanef = lane.astype(F32)
    big = float(LANES)

    def softmax_masked(mask):
        mx = jnp.max(jnp.where(mask, lg, -jnp.inf), axis=1, keepdims=True)
        e = jnp.where(mask, jnp.exp(lg - mx), 0.0)
        return e / jnp.sum(e, axis=1, keepdims=True)

    def top1(p, mask):
        best = jnp.max(jnp.where(mask, p, -1.0), axis=1, keepdims=True)
        idx = jnp.min(jnp.where(mask & (p == best), lanef, big), axis=1, keepdims=True)
        return best, idx

    gmask = lane < N_GROUPS
    pg, gi = top1(softmax_masked(gmask), gmask)
    lo = ROUTE_LANE0 + gi * EXPERTS_PER_GROUP
    emask = (lanef >= lo) & (lanef < lo + EXPERTS_PER_GROUP)
    pe = softmax_masked(emask)
    p1, i1 = top1(pe, emask)
    p2, i2 = top1(pe, emask & (lanef != i1))
    den = p1 + p2
    g1 = pg * p1 / den
    g2 = pg * p2 / den

    oh1 = lanef == i1
    oh2 = lanef == i2
    both = jnp.where(oh1 | oh2, 1.0, 0.0)
    row = lax.broadcasted_iota(I32, (tm, tm), 0)
    col = lax.broadcasted_iota(I32, (tm, tm), 1)
    strict = jnp.where(col < row, 1.0, 0.0).astype(BF16)
    before = jnp.dot(strict, both.astype(BF16), preferred_element_type=F32) + cnt_ref[0:1, :]
    r1 = jnp.sum(jnp.where(oh1, before, 0.0), axis=1, keepdims=True)
    r2 = jnp.sum(jnp.where(oh2, before, 0.0), axis=1, keepdims=True)
    cnt_ref[0:1, :] = cnt_ref[0:1, :] + jnp.sum(both, axis=0, keepdims=True)

    e1 = i1 - ROUTE_LANE0
    e2 = i2 - ROUTE_LANE0
    route = jnp.where(lane == 0, e1, jnp.where(lane == 1, e2, jnp.where(
        lane == 2, r1, jnp.where(lane == 3, r2, jnp.where(lane == 4, g1, jnp.where(lane == 5, g2, 0.0))))))
    route_ref[...] = route
    meta_ref[...] = jnp.transpose(route)[0:SUBLANES, :].astype(I32)


def _outproj_router(a, w_all, layer, h, g, wrh, wrl, br, *, tm):
    t, kdim = a.shape
    d = h.shape[1]
    row = lambda i: (i, 0)
    const = lambda i: (0, 0)
    kern = functools.partial(_outproj_router_kernel, layer=layer)
    return pl.pallas_call(
        kern,
        grid=(t // tm,),
        in_specs=[pl.BlockSpec((tm, kdim), row), pl.BlockSpec(memory_space=pl.ANY), pl.BlockSpec((tm, d), row),
                  pl.BlockSpec((1, d), const), pl.BlockSpec((d, LANES), const), pl.BlockSpec((d, LANES), const),
                  pl.BlockSpec((1, LANES), const)],
        out_specs=[pl.BlockSpec((tm, d), row), pl.BlockSpec((tm * d // LANES, LANES), row),
                   pl.BlockSpec((tm, LANES), row), pl.BlockSpec((None, SUBLANES, tm), lambda i: (i, 0, 0)),
                   pl.BlockSpec((SUBLANES, LANES), const)],
        out_shape=[jax.ShapeDtypeStruct((t, d), F32), jax.ShapeDtypeStruct((t * d // LANES, LANES), F32),
                   jax.ShapeDtypeStruct((t, LANES), F32), jax.ShapeDtypeStruct((t // tm, SUBLANES, tm), I32),
                   jax.ShapeDtypeStruct((SUBLANES, LANES), F32)],
        scratch_shapes=_weight_scratch(kdim, d),
        compiler_params=_cparams("arbitrary"),
        name="outproj_router",
    )(a, w_all, h, g, wrh, wrl, br)


def _pad_copies(counts_ref, pstart_ref, zero_ref, xs_ref, sem, *, bm):
    nslab = SUBLANES
    out = []
    for e in range(N_EXPERTS):
        cnt = counts_ref[e]
        npad = (bm - (cnt & (bm - 1))) & (bm - 1)
        pos = pstart_ref[e] + cnt
        bit = 1
        while bit < bm:
            out.append(((npad & bit) != 0, pltpu.make_async_copy(
                _slab_rows(zero_ref, 0, bit, nslab), _slab_rows(xs_ref, pos, bit, nslab), sem)))
            pos = pos + (npad & bit)
            bit *= 2
    return out


def _dispatch_kernel(counts_ref, pstart_ref, nused_ref, meta_ref, hn_ref, xs_ref, zero_ref, sem, psem, *, bm):
    nslab = SUBLANES
    tiles = meta_ref.shape[0]
    tm = hn_ref.shape[0] // (nslab * tiles)
    half = bm // 2

    @pl.when(pl.program_id(0) == 0)
    def _():
        zero_ref[...] = jnp.zeros_like(zero_ref)
        pads = _pad_copies(counts_ref, pstart_ref, zero_ref, xs_ref, psem, bm=bm)
        for pred, cp in pads:
            pl.when(pred)(cp.start)

        def tail_copy(j):
            return pltpu.make_async_copy(zero_ref, _slab_rows(xs_ref, j * half, half, nslab), psem)

        first = nused_ref[0] * 2
        last = xs_ref.shape[0] // (half * nslab)
        lax.fori_loop(first, last, lambda j, c: (tail_copy(j).start(), c)[1], 0)
        for pred, cp in pads:
            pl.when(pred)(cp.wait)
        lax.fori_loop(first, last, lambda j, c: (tail_copy(j).wait(), c)[1], 0)

    def row_copy(j, t, k):
        dst = meta_ref[j, 0, TOP_K * t + k]
        return pltpu.make_async_copy(_slab_rows(hn_ref, j * tm + t, 1, nslab), _slab_at(xs_ref, dst, nslab), sem)

    def issue(j):
        def body(t, carry):
            for k in range(TOP_K):
                row_copy(j, t, k).start(priority=k)
            return carry
        return body

    def drain(j):
        def body(t, carry):
            for k in range(TOP_K):
                row_copy(j, t, k).wait()
            return carry
        return body

    for j in range(tiles):
        lax.fori_loop(0, tm, issue(j), 0, unroll=8)
    for j in range(tiles):
        lax.fori_loop(0, tm, drain(j), 0, unroll=8)


def _dispatch(counts, pstart, n_used, meta, hn, n_rows, *, bm):
    nslab = SUBLANES
    t = hn.shape[0] // nslab
    tm = meta.shape[2] // TOP_K
    tiles = 2 if meta.shape[0] % 2 == 0 else 1
    kern = functools.partial(_dispatch_kernel, bm=bm)
    grid_spec = pltpu.PrefetchScalarGridSpec(
        num_scalar_prefetch=3,
        grid=(t // (tm * tiles),),
        in_specs=[pl.BlockSpec((tiles, 1, TOP_K * tm), lambda i, *_: (i, 0, 0), memory_space=pltpu.SMEM),
                  pl.BlockSpec((tiles * tm * nslab, LANES), lambda i, *_: (i, 0))],
        out_specs=pl.BlockSpec(memory_space=pl.ANY),
        scratch_shapes=[pltpu.VMEM((bm // 2 * nslab, LANES), F32), pltpu.SemaphoreType.DMA(()),
                        pltpu.SemaphoreType.DMA(())],
    )
    return pl.pallas_call(
        kern,
        grid_spec=grid_spec,
        out_shape=jax.ShapeDtypeStruct((n_rows * nslab, LANES), F32),
        compiler_params=_cparams("arbitrary"),
        name="moe_dispatch",
    )(counts, pstart, n_used, meta, hn)


def _experts_kernel(be_ref, nu_ref, x_ref, wgu_ref, wd_ref, y_ref, wgu_b, wd_b):
    i = pl.program_id(0)
    d = wgu_b.shape[0]
    bm = x_ref.shape[0] * LANES // d
    prev = be_ref[jnp.maximum(i - 1, 0)]

    @pl.when((i < nu_ref[0]) & ((i == 0) | (be_ref[i] != prev)))
    def _():
        wgu_b[...] = wgu_ref[...].astype(BF16)
        wd_b[...] = wd_ref[...].astype(BF16)

    @pl.when(i < nu_ref[0])
    def _():
        x = _slabs_to_rows(x_ref, bm, d // LANES).astype(BF16)
        hcat = jnp.dot(x, wgu_b[...], preferred_element_type=F32)
        ff = hcat.shape[1] // 2
        a = hcat[:, :ff]
        b = hcat[:, ff:]
        act = ((a * _sigmoid(a)) * b).astype(BF16)
        _rows_to_slabs(y_ref, jnp.dot(act, wd_b[...], preferred_element_type=F32))

    @pl.when(i >= nu_ref[0])
    def _():
        y_ref[...] = jnp.zeros_like(y_ref)


def _experts(block_e, n_used, xs, wgu_all, wd_all, layer, *, bm):
    d = wgu_all.shape[2]
    nslab = d // LANES
    n_rows = xs.shape[0] // nslab
    ff2 = wgu_all.shape[3]
    grid_spec = pltpu.PrefetchScalarGridSpec(
        num_scalar_prefetch=2,
        grid=(n_rows // bm,),
        in_specs=[pl.BlockSpec((bm * nslab, LANES), lambda i, be, nu: (jnp.minimum(i, nu[0] - 1), 0)),
                  pl.BlockSpec((None, None, d, ff2), lambda i, be, nu: (layer, be[i], 0, 0)),
                  pl.BlockSpec((None, None, ff2 // 2, d), lambda i, be, nu: (layer, be[i], 0, 0))],
        out_specs=pl.BlockSpec((bm * nslab, LANES), lambda i, be, nu: (i, 0)),
        scratch_shapes=[pltpu.VMEM((d, ff2), BF16), pltpu.VMEM((ff2 // 2, d), BF16)],
    )
    return pl.pallas_call(
        _experts_kernel,
        grid_spec=grid_spec,
        out_shape=jax.ShapeDtypeStruct((n_rows * nslab, LANES), F32),
        compiler_params=_cparams("arbitrary"),
        name="moe_experts",
    )(block_e, n_used, xs, wgu_all, wd_all)


def _final_norm_kernel(meta_cur, meta_nxt, h1_ref, route_ref, ys_ref, gfin_ref, o_ref, buf_ref, gsem):
    _issue_residual_gathers(meta_cur, meta_nxt, ys_ref, buf_ref, gsem, tm=h1_ref.shape[0], sub=1)
    o_ref[...] = _rms(_gathered_residual(h1_ref, route_ref, buf_ref, gsem), gfin_ref[...], EPS)


def _final_norm(src, gfin):
    t, d = src.h1.shape
    tm = src.meta.shape[2] // TOP_K
    _, _, operands, in_specs, out_specs, out_shape, scratch = _residual_specs(src, tm, d)
    grid_spec = pltpu.PrefetchScalarGridSpec(
        num_scalar_prefetch=0,
        grid=(t // tm,),
        in_specs=in_specs + [pl.BlockSpec((1, d), lambda i: (0, 0))],
        out_specs=out_specs[0],
        scratch_shapes=scratch,
    )
    return pl.pallas_call(
        _final_norm_kernel,
        grid_spec=grid_spec,
        out_shape=out_shape[0],
        compiler_params=_cparams("arbitrary"),
        name="moe_combine_final_norm",
    )(*operands, gfin)


def _moe(a, w_out_all, mixer_layer, h, layer, g_ffn, wr_g, br_g, wr_e, br_e, w_gu_all, w_down_all):
    t, d = h.shape
    bm = MOE_BLOCK
    tm = min(ROW_TILE, t)
    wr = jnp.concatenate([wr_g, wr_e, jnp.zeros((d, LANES - N_GROUPS - N_EXPERTS), F32)], axis=1)
    wrh = wr.astype(BF16)
    wrl = (wr - wrh.astype(F32)).astype(BF16)
    br = jnp.concatenate([br_g, br_e, jnp.zeros((LANES - N_GROUPS - N_EXPERTS,), F32)])[None, :]
    h1, hn, route, meta, cnt = _outproj_router(a, w_out_all, mixer_layer, h, g_ffn[None, :], wrh, wrl, br, tm=tm)

    counts = cnt[0, ROUTE_LANE0:ROUTE_LANE0 + N_EXPERTS].astype(I32)
    nb = (counts + bm - 1) // bm
    nb_end = jnp.cumsum(nb)
    pstart = ((nb_end - nb) * bm).astype(I32)
    n_blocks = (t * TOP_K + N_EXPERTS * (bm - 1)) // bm
    j = jnp.arange(n_blocks, dtype=I32)
    block_e = jnp.minimum(jnp.sum((nb_end[None, :] <= j[:, None]).astype(I32), axis=1), N_EXPERTS - 1)
    n_used = nb_end[-1:].astype(I32)

    eid = meta[:, 0:TOP_K, :]
    base = jnp.sum(jnp.where(eid[..., None] == jnp.arange(N_EXPERTS, dtype=I32), pstart, 0), axis=-1)
    rows = (meta[:, TOP_K:2 * TOP_K, :] + base) * (d // LANES)
    rows = rows.transpose(0, 2, 1).reshape(rows.shape[0], 1, TOP_K * rows.shape[2])

    xs = _dispatch(counts, pstart, n_used, rows, hn, n_blocks * bm, bm=bm)
    ys = _experts(block_e, n_used, xs, w_gu_all, w_down_all, layer, bm=bm)
    return _Pending(rows, h1, route, ys)


def _fox_mixer(src, g, w_in_all, layer, b_f, *, batch, seq):
    d = w_in_all.shape[1]
    nh = FOX_HEADS
    wf = w_in_all[layer, :, 4 * d:4 * d + nh]
    wf = jnp.concatenate([wf] * 6 + [jnp.zeros((d, LANES - 6 * nh), F32)], axis=1).astype(BF16)
    bfr = jnp.concatenate([b_f] * 6 + [jnp.zeros((LANES - 6 * nh,), F32)])[None, :]
    h, q, kb, v, gate, a = _fox_inproj(src, g[None, :], w_in_all, layer, wf, bfr, seq=seq)
    return h, _fox_attn(q, a, kb, v, gate, batch=batch, seq=seq)


def _ret_mixer(src, g, w_in_all, layer, gn_gain, *, batch, seq):
    d = w_in_all.shape[1]
    nh = RET_HEADS
    dk = d // nh
    c = RET_CHUNK
    inv = 1.0 / (ROPE_BASE ** jnp.linspace(0.0, 1.0, dk // 2, dtype=F32))
    ang = jnp.arange(seq).astype(F32)[:, None] * inv[None, :]
    cos, sin = jnp.cos(ang), jnp.sin(ang)
    log_g = jnp.log(1.0 - 2.0 ** (-5.0 - jnp.arange(nh, dtype=F32)))
    idx = jnp.arange(c, dtype=F32)
    rel = idx[:, None] - idx[None, :]
    intra = jnp.where(rel >= 0, jnp.exp(log_g[:, None, None] * jnp.maximum(rel, 0.0)), 0.0)
    qd = jnp.exp(log_g[:, None] * (idx + 1.0))[:, :, None]
    kd = jnp.exp(log_g[:, None] * (c - 1.0 - idx))[:, :, None]
    cd = jnp.exp(log_g * c)[:, None, None]
    h, q, k, v, gate = _ret_inproj(src, g[None, :], w_in_all, layer, cos, sin, seq=seq)
    return h, _retention(q, k, v, gate, gn_gain[None, :], intra, qd, kd, cd, batch=batch, seq=seq)


def kernel(x, fox_w_in, fox_b_f, fox_w_out, ret_w_in, ret_gn_gain, ret_w_out, norm_mix, norm_ffn,
           router_group_w, router_group_b, router_expert_w, router_expert_b, expert_w_gu, expert_w_down,
           norm_final):
    batch, seq, d = x.shape
    depth = norm_mix.shape[0]
    src = x.reshape(batch * seq, d)
    for i in range(depth):
        j = i // 2
        if i % 2 == 0:
            h, a = _fox_mixer(src, norm_mix[i], fox_w_in, j, fox_b_f[j], batch=batch, seq=seq)
            w_out_all = fox_w_out
        else:
            h, a = _ret_mixer(src, norm_mix[i], ret_w_in, j, ret_gn_gain[j], batch=batch, seq=seq)
            w_out_all = ret_w_out
        src = _moe(a, w_out_all, j, h, i, norm_ffn[i], router_group_w[i], router_group_b[i], router_expert_w[i],
                   router_expert_b[i], expert_w_gu, expert_w_down)
    return _final_norm(src, norm_final[None, :]).reshape(batch, seq, d)
```

```python
import functools
from typing import NamedTuple

import jax
import jax.numpy as jnp
from jax import lax
from jax.experimental import pallas as pl
from jax.experimental.pallas import tpu as pltpu

F32 = jnp.float32
BF16 = jnp.bfloat16
I32 = jnp.int32

FOX_HEADS = 16
RET_HEADS = 4
RET_CHUNK = 128
N_GROUPS = 4
EXPERTS_PER_GROUP = 8
N_EXPERTS = N_GROUPS * EXPERTS_PER_GROUP
TOP_K = 2
EPS = 1e-6
NEG_INF = -1e30
ROPE_BASE = 10000.0

LANES = 128
SUBLANES = 8
ROUTE_LANE0 = N_GROUPS
VMEM_LIMIT = 56 * 1024 * 1024
MOE_BLOCK = 512
ROW_TILE = 512
WEIGHT_CHUNK = 512

_CONTRACT_LAST = (((1,), (1,)), ((), ()))
_CONTRACT_FIRST = (((0,), (0,)), ((), ()))


def _cparams(*sem):
    return pltpu.CompilerParams(dimension_semantics=sem, vmem_limit_bytes=VMEM_LIMIT)


def _sigmoid(x):
    return 1.0 / (1.0 + jnp.exp(-x))


def _rms(x, g, eps):
    ms = jnp.mean(x * x, axis=-1, keepdims=True)
    return x * lax.rsqrt(ms + eps) * g


def _split3(x):
    hi = x.astype(BF16)
    r = x - hi.astype(F32)
    mid = r.astype(BF16)
    lo = (r - mid.astype(F32)).astype(BF16)
    return hi, mid, lo


def _stage_weight(w_hbm, wb_ref, stage_ref, sem, ncols):
    chunk = stage_ref.shape[2]
    n = ncols // chunk

    def cp(c):
        return pltpu.make_async_copy(w_hbm.at[:, pl.ds(c * chunk, chunk)], stage_ref.at[c % 2], sem.at[c % 2])

    cp(0).start()
    for c in range(n):
        if c + 1 < n:
            cp(c + 1).start()
        cp(c).wait()
        wb_ref[:, c * chunk:(c + 1) * chunk] = stage_ref[c % 2].astype(BF16)


def _stage_weight_t(wt_hbm, wb_ref, stage_ref, sem, ncols):
    chunk = stage_ref.shape[1]
    n = ncols // chunk

    def cp(c):
        return pltpu.make_async_copy(wt_hbm.at[pl.ds(c * chunk, chunk), :], stage_ref.at[c % 2], sem.at[c % 2])

    cp(0).start()
    for c in range(n):
        if c + 1 < n:
            cp(c + 1).start()
        cp(c).wait()
        wb_ref[:, c * chunk:(c + 1) * chunk] = jnp.transpose(stage_ref[c % 2]).astype(BF16)


def _rows_to_slabs(ref, x):
    n, d = x.shape
    nslab = d // LANES
    for c in range(nslab):
        ref[pl.ds(c, n, stride=nslab), :] = x[:, c * LANES:(c + 1) * LANES]


def _slabs_to_rows(ref, n, nslab):
    return jnp.concatenate([ref[pl.ds(c, n, stride=nslab), :] for c in range(nslab)], axis=1)


def _slab_rows(ref, row, nrows, nslab):
    return ref.at[pl.ds(pl.multiple_of(row * nslab, nslab), nrows * nslab), :]


def _slab_at(ref, first, nslab):
    return ref.at[pl.ds(pl.multiple_of(first, nslab), nslab), :]


class _Pending(NamedTuple):
    meta: jax.Array
    h1: jax.Array
    route: jax.Array
    ys: jax.Array


def _issue_residual_gathers(meta_cur, meta_nxt, ys_ref, buf_ref, gsem, *, tm, sub):
    i = pl.program_id(0)
    nslab = buf_ref.shape[2] // tm

    def issue(meta_ref, step):
        slot = step % 2
        off = (step % sub) * tm

        def body(t, carry):
            for k in range(TOP_K):
                src = meta_ref[0, TOP_K * (off + t) + k]
                pltpu.make_async_copy(_slab_at(ys_ref, src, nslab), _slab_rows(buf_ref.at[slot, k], t, 1, nslab),
                                      gsem.at[slot]).start(priority=k)
            return carry

        lax.fori_loop(0, tm, body, 0, unroll=8)

    @pl.when(i == 0)
    def _():
        issue(meta_cur, i)

    @pl.when(i + 1 < pl.num_programs(0))
    def _():
        issue(meta_nxt, i + 1)


def _gathered_residual(h1_ref, route_ref, buf_ref, gsem):
    tm, d = h1_ref.shape
    nslab = d // LANES
    slot = pl.program_id(0) % 2
    pltpu.make_async_copy(buf_ref.at[slot], buf_ref.at[slot], gsem.at[slot]).wait()
    g0 = route_ref[:, 4:5]
    g1 = route_ref[:, 5:6]
    return h1_ref[...] + (_slabs_to_rows(buf_ref.at[slot, 0], tm, nslab) * g0
                          + _slabs_to_rows(buf_ref.at[slot, 1], tm, nslab) * g1)


def _residual_specs(src, tm, d):
    if not isinstance(src, _Pending):
        return False, 1, [src], [pl.BlockSpec((tm, d), lambda i: (i, 0))], [], [], []
    tile = src.meta.shape[2] // TOP_K
    sub = tile // tm
    last = src.h1.shape[0] // tm - 1
    smem = lambda f: pl.BlockSpec((None, 1, TOP_K * tile), f, memory_space=pltpu.SMEM)
    in_specs = [smem(lambda i: (i // sub, 0, 0)), smem(lambda i: (jnp.minimum(i + 1, last) // sub, 0, 0)),
                pl.BlockSpec((tm, d), lambda i: (i, 0)), pl.BlockSpec((tm, LANES), lambda i: (i, 0)),
                pl.BlockSpec(memory_space=pl.ANY)]
    scratch = [pltpu.VMEM((2, TOP_K, tm * d // LANES, LANES), F32), pltpu.SemaphoreType.DMA((2,))]
    out_specs = [pl.BlockSpec((tm, d), lambda i: (i, 0))]
    out_shape = [jax.ShapeDtypeStruct(src.h1.shape, F32)]
    return True, sub, [src.meta, src.meta, src.h1, src.route, src.ys], in_specs, out_specs, out_shape, scratch


def _weight_scratch(kdim, ncols, transposed=False):
    stage = (2, WEIGHT_CHUNK, kdim) if transposed else (2, kdim, WEIGHT_CHUNK)
    return [pltpu.VMEM((kdim, ncols), BF16), pltpu.VMEM(stage, F32), pltpu.SemaphoreType.DMA((2,))]


def _fox_inproj_kernel(*refs, layer, tiles_per_seq, scale, fused, sub):
    i = pl.program_id(0)
    if fused:
        (meta_cur, meta_nxt, h1_ref, route_ref, ys_ref, g_ref, w_hbm, wf_ref, bf_ref, hout_ref, q_ref, kb_ref,
         v_ref, gate_ref, a_ref, carry_ref, wb_ref, stage_ref, sem, buf_ref, gsem) = refs
        _issue_residual_gathers(meta_cur, meta_nxt, ys_ref, buf_ref, gsem, tm=h1_ref.shape[0], sub=sub)
    else:
        (h_ref, g_ref, w_hbm, wf_ref, bf_ref, q_ref, kb_ref, v_ref, gate_ref, a_ref,
         carry_ref, wb_ref, stage_ref, sem) = refs
    tm, d = q_ref.shape

    @pl.when(i == 0)
    def _():
        _stage_weight_t(w_hbm.at[layer], wb_ref, stage_ref, sem, 4 * d)

    if fused:
        x = _gathered_residual(h1_ref, route_ref, buf_ref, gsem)
        hout_ref[...] = x
    else:
        x = h_ref[...]
    xn = _rms(x, g_ref[...], EPS).astype(BF16)

    q = jnp.dot(xn, wb_ref[:, 0:d], preferred_element_type=F32)
    q_ref[...] = (q * scale).astype(BF16)
    v_ref[...] = jnp.dot(xn, wb_ref[:, 2 * d:3 * d], preferred_element_type=F32).astype(BF16)
    gate_ref[...] = jnp.dot(xn, wb_ref[:, 3 * d:4 * d], preferred_element_type=F32)

    f = jnp.dot(xn, wf_ref[...], preferred_element_type=F32) + bf_ref[...]
    ls = jnp.minimum(f, 0.0) - jnp.log1p(jnp.exp(-jnp.abs(f)))

    @pl.when(i % tiles_per_seq == 0)
    def _():
        carry_ref[...] = jnp.zeros_like(carry_ref)

    row = lax.broadcasted_iota(I32, (tm, tm), 0)
    col = lax.broadcasted_iota(I32, (tm, tm), 1)
    tri = jnp.where(col <= row, 1.0, 0.0).astype(BF16)
    hi, mid, lo = _split3(ls)
    cum = (jnp.dot(tri, hi, preferred_element_type=F32) + jnp.dot(tri, mid, preferred_element_type=F32)
           + jnp.dot(tri, lo, preferred_element_type=F32)) + carry_ref[...]
    carry_ref[...] = cum[tm - 1:tm, :]

    c_hi, c_mid, c_lo = (c.astype(F32) for c in _split3(cum))
    lane = lax.broadcasted_iota(I32, (tm, LANES), 1)
    a_ref[...] = jnp.where(lane < 16, c_hi, jnp.where(lane < 32, c_mid, jnp.where(
        lane < 48, c_lo, jnp.where(lane < 96, 1.0, 0.0)))).astype(BF16)
    bk = jnp.where(lane < 48, 1.0, jnp.where(lane < 64, -c_hi, jnp.where(
        lane < 80, -c_mid, jnp.where(lane < 96, -c_lo, 0.0)))).astype(BF16)

    k = jnp.dot(xn, wb_ref[:, d:2 * d], preferred_element_type=F32).astype(BF16)
    for p in range(d // LANES):
        kb_ref[:, 2 * p * LANES:(2 * p + 1) * LANES] = k[:, p * LANES:(p + 1) * LANES]
        kb_ref[:, (2 * p + 1) * LANES:(2 * p + 2) * LANES] = bk


def _fox_inproj(src, g, w_all, layer, wf, bfr, *, seq):
    t, d = src.h1.shape if isinstance(src, _Pending) else src.shape
    tm = min(ROW_TILE, seq)
    scale = float((d // FOX_HEADS) ** -0.5)
    fused, sub, operands, in_specs, out_specs, out_shape, scratch = _residual_specs(src, tm, d)
    kern = functools.partial(_fox_inproj_kernel, layer=layer, tiles_per_seq=seq // tm, scale=scale,
                             fused=fused, sub=sub)
    row = lambda i: (i, 0)
    const = lambda i: (0, 0)
    grid_spec = pltpu.PrefetchScalarGridSpec(
        num_scalar_prefetch=0,
        grid=(t // tm,),
        in_specs=in_specs + [pl.BlockSpec((1, d), const), pl.BlockSpec(memory_space=pl.ANY),
                             pl.BlockSpec((d, LANES), const), pl.BlockSpec((1, LANES), const)],
        out_specs=out_specs + [pl.BlockSpec((tm, d), row), pl.BlockSpec((tm, 2 * d), row),
                               pl.BlockSpec((tm, d), row), pl.BlockSpec((tm, d), row),
                               pl.BlockSpec((tm, LANES), row)],
        scratch_shapes=[pltpu.VMEM((1, LANES), F32)] + _weight_scratch(d, 4 * d, transposed=True) + scratch,
    )
    outs = pl.pallas_call(
        kern,
        grid_spec=grid_spec,
        out_shape=out_shape + [jax.ShapeDtypeStruct((t, d), BF16), jax.ShapeDtypeStruct((t, 2 * d), BF16),
                               jax.ShapeDtypeStruct((t, d), BF16), jax.ShapeDtypeStruct((t, d), F32),
                               jax.ShapeDtypeStruct((t, LANES), BF16)],
        compiler_params=_cparams("arbitrary"),
        name="fox_inproj",
    )(*operands, g, w_all, wf, bfr)
    return (outs if fused else [src] + list(outs))


def _fox_attn_kernel(q_ref, a_ref, kb_ref, v_ref, gate_ref, o_ref, *, tq, dh, depth):
    seq = q_ref.shape[0]
    pair = pl.program_id(1)
    lane = lax.broadcasted_iota(I32, (tq, LANES), 1)
    krow = lax.broadcasted_iota(I32, (tq, tq), 0)
    qcol = lax.broadcasted_iota(I32, (tq, tq), 1)
    causal = krow <= qcol
    heads_per_block = LANES // dh
    qmask = [jnp.where((lane >= hh * dh) & (lane < (hh + 1) * dh), 1.0, 0.0).astype(BF16)
             for hh in range(heads_per_block)]
    amask = [jnp.where((lane < 96) & ((lane & 15) == pair * heads_per_block + hh), 1.0, 0.0).astype(BF16)
             for hh in range(heads_per_block)]
    vt = jnp.transpose(v_ref[...])
    chains = [(i, hh) for i in range(seq // tq) for hh in range(heads_per_block)]

    def scores(i, hh):
        off = i * tq
        qa = jnp.concatenate([q_ref[off:off + tq, :] * qmask[hh], a_ref[off:off + tq, :] * amask[hh]], axis=1)
        return lax.dot_general(kb_ref[0:off + tq, :], qa, _CONTRACT_LAST, preferred_element_type=F32)

    def finish(i, hh, st):
        off = i * tq
        sd = jnp.where(causal, st[off:off + tq, :], NEG_INF)
        st = jnp.concatenate([st[0:off, :], sd], axis=0) if off > 0 else sd
        m = jnp.max(st, axis=0, keepdims=True)
        e = jnp.exp(st - m)
        inv = 1.0 / jnp.sum(e, axis=0, keepdims=True)
        return jnp.dot(vt[hh * dh:(hh + 1) * dh, 0:off + tq], e.astype(BF16), preferred_element_type=F32) * inv

    outs = {}
    queue = [scores(*c) for c in chains[:depth]]
    for n, (i, hh) in enumerate(chains):
        cur = queue.pop(0)
        if n + depth < len(chains):
            queue.append(scores(*chains[n + depth]))
        outs[hh] = finish(i, hh, cur)
        if hh == heads_per_block - 1:
            off = i * tq
            o = jnp.transpose(jnp.concatenate([outs[h2] for h2 in range(heads_per_block)], axis=0))
            o_ref[off:off + tq, :] = (o * _sigmoid(gate_ref[off:off + tq, :])).astype(BF16)


def _fox_attn(q, a, kb, v, gate, *, batch, seq, tq=256, depth=3):
    t, d = q.shape
    dh = d // FOX_HEADS
    pairs = d // LANES
    kern = functools.partial(_fox_attn_kernel, tq=tq, dh=dh, depth=depth)
    blk = lambda b, p: (b, p)
    return pl.pallas_call(
        kern,
        grid=(batch, pairs),
        in_specs=[pl.BlockSpec((seq, LANES), blk), pl.BlockSpec((seq, LANES), lambda b, p: (b, 0)),
                  pl.BlockSpec((seq, 2 * LANES), blk), pl.BlockSpec((seq, LANES), blk),
                  pl.BlockSpec((seq, LANES), blk)],
        out_specs=pl.BlockSpec((seq, LANES), blk),
        out_shape=jax.ShapeDtypeStruct((t, d), BF16),
        compiler_params=_cparams("arbitrary", "arbitrary"),
        name="fox_attn",
    )(q, a, kb, v, gate)


def _ret_inproj_kernel(*refs, layer, kscale, fused, sub):
    if fused:
        (meta_cur, meta_nxt, h1_ref, route_ref, ys_ref, g_ref, w_hbm, cos_ref, sin_ref, hout_ref, q_ref, k_ref,
         v_ref, gate_ref, wb_ref, stage_ref, sem, buf_ref, gsem) = refs
        _issue_residual_gathers(meta_cur, meta_nxt, ys_ref, buf_ref, gsem, tm=h1_ref.shape[0], sub=sub)
    else:
        h_ref, g_ref, w_hbm, cos_ref, sin_ref, q_ref, k_ref, v_ref, gate_ref, wb_ref, stage_ref, sem = refs
    tm, d = q_ref.shape

    @pl.when(pl.program_id(0) == 0)
    def _():
        _stage_weight(w_hbm.at[layer], wb_ref, stage_ref, sem, 6 * d)

    if fused:
        x = _gathered_residual(h1_ref, route_ref, buf_ref, gsem)
        hout_ref[...] = x
    else:
        x = h_ref[...]
    xn = _rms(x, g_ref[...], EPS).astype(BF16)
    cos = cos_ref[...]
    sin = sin_ref[...]
    dk = d // RET_HEADS
    half = dk // 2

    def rot(t, h):
        t1 = t[:, h * dk:h * dk + half]
        t2 = t[:, h * dk + half:(h + 1) * dk]
        return t1 * cos - t2 * sin, t1 * sin + t2 * cos

    q = jnp.dot(xn, wb_ref[:, 0:d], preferred_element_type=F32)
    k = jnp.dot(xn, wb_ref[:, d:2 * d], preferred_element_type=F32)
    for h in range(RET_HEADS):
        q1, q2 = rot(q, h)
        q_ref[:, h * dk:h * dk + half] = q1.astype(BF16)
        q_ref[:, h * dk + half:(h + 1) * dk] = q2.astype(BF16)
        k1, k2 = rot(k, h)
        k_ref[:, h * dk:h * dk + half] = k1 * kscale
        k_ref[:, h * dk + half:(h + 1) * dk] = k2 * kscale
    v_ref[...] = jnp.dot(xn, wb_ref[:, 2 * d:4 * d], preferred_element_type=F32).astype(BF16)
    gate_ref[...] = jnp.dot(xn, wb_ref[:, 4 * d:6 * d], preferred_element_type=F32)


def _ret_inproj(src, g, w_all, layer, cos, sin, *, seq, tm=256):
    t, d = src.h1.shape if isinstance(src, _Pending) else src.shape
    tm = min(tm, seq)
    half = d // RET_HEADS // 2
    fused, sub, operands, in_specs, out_specs, out_shape, scratch = _residual_specs(src, tm, d)
    kern = functools.partial(_ret_inproj_kernel, layer=layer, kscale=float((d // RET_HEADS) ** -0.5),
                             fused=fused, sub=sub)
    row = lambda i: (i, 0)
    const = lambda i: (0, 0)
    pos = lambda i: (i % (seq // tm), 0)
    grid_spec = pltpu.PrefetchScalarGridSpec(
        num_scalar_prefetch=0,
        grid=(t // tm,),
        in_specs=in_specs + [pl.BlockSpec((1, d), const), pl.BlockSpec(memory_space=pl.ANY),
                             pl.BlockSpec((tm, half), pos), pl.BlockSpec((tm, half), pos)],
        out_specs=out_specs + [pl.BlockSpec((tm, d), row), pl.BlockSpec((tm, d), row),
                               pl.BlockSpec((tm, 2 * d), row), pl.BlockSpec((tm, 2 * d), row)],
        scratch_shapes=_weight_scratch(d, 6 * d) + scratch,
    )
    outs = pl.pallas_call(
        kern,
        grid_spec=grid_spec,
        out_shape=out_shape + [jax.ShapeDtypeStruct((t, d), BF16), jax.ShapeDtypeStruct((t, d), F32),
                               jax.ShapeDtypeStruct((t, 2 * d), BF16), jax.ShapeDtypeStruct((t, 2 * d), F32)],
        compiler_params=_cparams("arbitrary"),
        name="ret_inproj",
    )(*operands, g, w_all, cos, sin)
    return (outs if fused else [src] + list(outs))


def _retention_kernel(q_ref, k_ref, v_ref, gate_ref, gain_ref, intra_ref, qd_ref, kd_ref, cd_ref,
                      y_ref, r_ref, *, chunk):
    seq = q_ref.shape[0]
    r_ref[...] = jnp.zeros_like(r_ref)

    def body(c):
        off = c * chunk
        qn = q_ref[pl.ds(off, chunk), :]
        k32 = k_ref[pl.ds(off, chunk), :]
        vn = v_ref[pl.ds(off, chunk), :]
        s = lax.dot_general(qn, k32.astype(BF16), _CONTRACT_LAST, preferred_element_type=F32) * intra_ref[...]
        o = jnp.dot(s.astype(BF16), vn, preferred_element_type=F32)
        r = r_ref[...]
        o = o + jnp.dot(qn, r.astype(BF16), preferred_element_type=F32) * qd_ref[...]
        kdec = (k32 * kd_ref[...]).astype(BF16)
        r_ref[...] = r * cd_ref[...] + lax.dot_general(kdec, vn, _CONTRACT_FIRST, preferred_element_type=F32)
        mu = jnp.mean(o, axis=-1, keepdims=True)
        dlt = o - mu
        var = jnp.mean(dlt * dlt, axis=-1, keepdims=True)
        on = dlt * lax.rsqrt(var + EPS) * gain_ref[...]
        g = gate_ref[pl.ds(off, chunk), :]
        y_ref[pl.ds(off, chunk), :] = ((g * _sigmoid(g)) * on).astype(BF16)

    for c in range(seq // chunk):
        body(c)


def _retention(q, k, v, gate, gain, intra, qd, kd, cd, *, batch, seq):
    t, d = q.shape
    dk = d // RET_HEADS
    dv = v.shape[1] // RET_HEADS
    c = RET_CHUNK
    kern = functools.partial(_retention_kernel, chunk=c)
    blk = lambda b, h: (b, h)
    hd = lambda b, h: (h, 0, 0)
    return pl.pallas_call(
        kern,
        grid=(batch, RET_HEADS),
        in_specs=[pl.BlockSpec((seq, dk), blk), pl.BlockSpec((seq, dk), blk), pl.BlockSpec((seq, dv), blk),
                  pl.BlockSpec((seq, dv), blk), pl.BlockSpec((1, dv), lambda b, h: (0, h)),
                  pl.BlockSpec((None, c, c), hd), pl.BlockSpec((None, c, 1), hd),
                  pl.BlockSpec((None, c, 1), hd), pl.BlockSpec((None, 1, 1), hd)],
        out_specs=pl.BlockSpec((seq, dv), blk),
        out_shape=jax.ShapeDtypeStruct((t, v.shape[1]), BF16),
        scratch_shapes=[pltpu.VMEM((dk, dv), F32)],
        compiler_params=_cparams("arbitrary", "arbitrary"),
        name="retention",
    )(q, k, v, gate, gain, intra, qd, kd, cd)


def _outproj_router_kernel(a_ref, w_hbm, h_ref, g_ref, wrh_ref, wrl_ref, br_ref,
                           h1_ref, hn_ref, route_ref, meta_ref, cnt_ref, wb_ref, stage_ref, sem, *, layer):
    i = pl.program_id(0)
    tm = a_ref.shape[0]
    d = h_ref.shape[1]

    @pl.when(i == 0)
    def _():
        cnt_ref[...] = jnp.zeros_like(cnt_ref)
        _stage_weight(w_hbm.at[layer], wb_ref, stage_ref, sem, d)

    h1 = h_ref[...] + jnp.dot(a_ref[...], wb_ref[...], preferred_element_type=F32)
    h1_ref[...] = h1
    hn = _rms(h1, g_ref[...], EPS)
    _rows_to_slabs(hn_ref, hn)

    hh = hn.astype(BF16)
    hl = (hn - hh.astype(F32)).astype(BF16)
    lg = (jnp.dot(hh, wrh_ref[...], preferred_element_type=F32)
          + jnp.dot(hh, wrl_ref[...], preferred_element_type=F32)
          + jnp.dot(hl, wrh_ref[...], preferred_element_type=F32)) + br_ref[...]

    lane = lax.broadcasted_iota(I32, (tm, LANES), 1)
    lanef = lane.astype(F32)
    big = float(LANES)

    def softmax_masked(mask):
        mx = jnp.max(jnp.where(mask, lg, -jnp.inf), axis=1, keepdims=True)
        e = jnp.where(mask, jnp.exp(lg - mx), 0.0)
        return e / jnp.sum(e, axis=1, keepdims=True)

    def top1(p, mask):
        best = jnp.max(jnp.where(mask, p, -1.0), axis=1, keepdims=True)
        idx = jnp.min(jnp.where(mask & (p == best), lanef, big), axis=1, keepdims=True)
        return best, idx

    gmask = lane < N_GROUPS
    pg, gi = top1(softmax_masked(gmask), gmask)
    lo = ROUTE_LANE0 + gi * EXPERTS_PER_GROUP
    emask = (lanef >= lo) & (lanef < lo + EXPERTS_PER_GROUP)
    pe = softmax_masked(emask)
    p1, i1 = top1(pe, emask)
    p2, i2 = top1(pe, emask & (lanef != i1))
    den = p1 + p2
    g1 = pg * p1 / den
    g2 = pg * p2 / den

    oh1 = lanef == i1
    oh2 = lanef == i2
    both = jnp.where(oh1 | oh2, 1.0, 0.0)
    row = lax.broadcasted_iota(I32, (tm, tm), 0)
    col = lax.broadcasted_iota(I32, (tm, tm), 1)
    strict = jnp.where(col < row, 1.0, 0.0).astype(BF16)
    before = jnp.dot(strict, both.astype(BF16), preferred_element_type=F32) + cnt_ref[0:1, :]
    r1 = jnp.sum(jnp.where(oh1, before, 0.0), axis=1, keepdims=True)
    r2 = jnp.sum(jnp.where(oh2, before, 0.0), axis=1, keepdims=True)
    cnt_ref[0:1, :] = cnt_ref[0:1, :] + jnp.sum(both, axis=0, keepdims=True)

    e1 = i1 - ROUTE_LANE0
    e2 = i2 - ROUTE_LANE0
    route = jnp.where(lane == 0, e1, jnp.where(lane == 1, e2, jnp.where(
        lane == 2, r1, jnp.where(lane == 3, r2, jnp.where(lane == 4, g1, jnp.where(lane == 5, g2, 0.0))))))
    route_ref[...] = route
    meta_ref[...] = jnp.transpose(route)[0:SUBLANES, :].astype(I32)


def _outproj_router(a, w_all, layer, h, g, wrh, wrl, br, *, tm):
    t, kdim = a.shape
    d = h.shape[1]
    row = lambda i: (i, 0)
    const = lambda i: (0, 0)
    kern = functools.partial(_outproj_router_kernel, layer=layer)
    return pl.pallas_call(
        kern,
        grid=(t // tm,),
        in_specs=[pl.BlockSpec((tm, kdim), row), pl.BlockSpec(memory_space=pl.ANY), pl.BlockSpec((tm, d), row),
                  pl.BlockSpec((1, d), const), pl.BlockSpec((d, LANES), const), pl.BlockSpec((d, LANES), const),
                  pl.BlockSpec((1, LANES), const)],
        out_specs=[pl.BlockSpec((tm, d), row), pl.BlockSpec((tm * d // LANES, LANES), row),
                   pl.BlockSpec((tm, LANES), row), pl.BlockSpec((None, SUBLANES, tm), lambda i: (i, 0, 0)),
                   pl.BlockSpec((SUBLANES, LANES), const)],
        out_shape=[jax.ShapeDtypeStruct((t, d), F32), jax.ShapeDtypeStruct((t * d // LANES, LANES), F32),
                   jax.ShapeDtypeStruct((t, LANES), F32), jax.ShapeDtypeStruct((t // tm, SUBLANES, tm), I32),
                   jax.ShapeDtypeStruct((SUBLANES, LANES), F32)],
        scratch_shapes=_weight_scratch(kdim, d),
        compiler_params=_cparams("arbitrary"),
        name="outproj_router",
    )(a, w_all, h, g, wrh, wrl, br)


def _pad_copies(counts_ref, pstart_ref, zero_ref, xs_ref, sem, *, bm):
    nslab = SUBLANES
    out = []
    for e in range(N_EXPERTS):
        cnt = counts_ref[e]
        npad = (bm - (cnt & (bm - 1))) & (bm - 1)
        pos = pstart_ref[e] + cnt
        bit = 1
        while bit < bm:
            out.append(((npad & bit) != 0, pltpu.make_async_copy(
                _slab_rows(zero_ref, 0, bit, nslab), _slab_rows(xs_ref, pos, bit, nslab), sem)))
            pos = pos + (npad & bit)
            bit *= 2
    return out


def _dispatch_kernel(counts_ref, pstart_ref, nused_ref, meta_ref, hn_ref, xs_ref, zero_ref, sem, psem, *, bm):
    nslab = SUBLANES
    tiles = meta_ref.shape[0]
    tm = hn_ref.shape[0] // (nslab * tiles)
    half = bm // 2

    @pl.when(pl.program_id(0) == 0)
    def _():
        zero_ref[...] = jnp.zeros_like(zero_ref)
        pads = _pad_copies(counts_ref, pstart_ref, zero_ref, xs_ref, psem, bm=bm)
        for pred, cp in pads:
            pl.when(pred)(cp.start)

        def tail_copy(j):
            return pltpu.make_async_copy(zero_ref, _slab_rows(xs_ref, j * half, half, nslab), psem)

        first = nused_ref[0] * 2
        last = xs_ref.shape[0] // (half * nslab)
        lax.fori_loop(first, last, lambda j, c: (tail_copy(j).start(), c)[1], 0)
        for pred, cp in pads:
            pl.when(pred)(cp.wait)
        lax.fori_loop(first, last, lambda j, c: (tail_copy(j).wait(), c)[1], 0)

    def row_copy(j, t, k):
        dst = meta_ref[j, 0, TOP_K * t + k]
        return pltpu.make_async_copy(_slab_rows(hn_ref, j * tm + t, 1, nslab), _slab_at(xs_ref, dst, nslab), sem)

    def issue(j):
        def body(t, carry):
            for k in range(TOP_K):
                row_copy(j, t, k).start(priority=k)
            return carry
        return body

    def drain(j):
        def body(t, carry):
            for k in range(TOP_K):
                row_copy(j, t, k).wait()
            return carry
        return body

    for j in range(tiles):
        lax.fori_loop(0, tm, issue(j), 0, unroll=8)
    for j in range(tiles):
        lax.fori_loop(0, tm, drain(j), 0, unroll=8)


def _dispatch(counts, pstart, n_used, meta, hn, n_rows, *, bm):
    nslab = SUBLANES
    t = hn.shape[0] // nslab
    tm = meta.shape[2] // TOP_K
    tiles = 2 if meta.shape[0] % 2 == 0 else 1
    kern = functools.partial(_dispatch_kernel, bm=bm)
    grid_spec = pltpu.PrefetchScalarGridSpec(
        num_scalar_prefetch=3,
        grid=(t // (tm * tiles),),
        in_specs=[pl.BlockSpec((tiles, 1, TOP_K * tm), lambda i, *_: (i, 0, 0), memory_space=pltpu.SMEM),
                  pl.BlockSpec((tiles * tm * nslab, LANES), lambda i, *_: (i, 0))],
        out_specs=pl.BlockSpec(memory_space=pl.ANY),
        scratch_shapes=[pltpu.VMEM((bm // 2 * nslab, LANES), F32), pltpu.SemaphoreType.DMA(()),
                        pltpu.SemaphoreType.DMA(())],
    )
    return pl.pallas_call(
        kern,
        grid_spec=grid_spec,
        out_shape=jax.ShapeDtypeStruct((n_rows * nslab, LANES), F32),
        compiler_params=_cparams("arbitrary"),
        name="moe_dispatch",
    )(counts, pstart, n_used, meta, hn)


def _experts_kernel(be_ref, nu_ref, x_ref, wgu_ref, wd_ref, y_ref, wgu_b, wd_b):
    i = pl.program_id(0)
    d = wgu_b.shape[0]
    bm = x_ref.shape[0] * LANES // d
    prev = be_ref[jnp.maximum(i - 1, 0)]

    @pl.when((i < nu_ref[0]) & ((i == 0) | (be_ref[i] != prev)))
    def _():
        wgu_b[...] = wgu_ref[...].astype(BF16)
        wd_b[...] = wd_ref[...].astype(BF16)

    @pl.when(i < nu_ref[0])
    def _():
        x = _slabs_to_rows(x_ref, bm, d // LANES).astype(BF16)
        hcat = jnp.dot(x, wgu_b[...], preferred_element_type=F32)
        ff = hcat.shape[1] // 2
        a = hcat[:, :ff]
        b = hcat[:, ff:]
        act = ((a * _sigmoid(a)) * b).astype(BF16)
        _rows_to_slabs(y_ref, jnp.dot(act, wd_b[...], preferred_element_type=F32))

    @pl.when(i >= nu_ref[0])
    def _():
        y_ref[...] = jnp.zeros_like(y_ref)


def _experts(block_e, n_used, xs, wgu_all, wd_all, layer, *, bm):
    d = wgu_all.shape[2]
    nslab = d // LANES
    n_rows = xs.shape[0] // nslab
    ff2 = wgu_all.shape[3]
    grid_spec = pltpu.PrefetchScalarGridSpec(
        num_scalar_prefetch=2,
        grid=(n_rows // bm,),
        in_specs=[pl.BlockSpec((bm * nslab, LANES), lambda i, be, nu: (jnp.minimum(i, nu[0] - 1), 0)),
                  pl.BlockSpec((None, None, d, ff2), lambda i, be, nu: (layer, be[i], 0, 0)),
                  pl.BlockSpec((None, None, ff2 // 2, d), lambda i, be, nu: (layer, be[i], 0, 0))],
        out_specs=pl.BlockSpec((bm * nslab, LANES), lambda i, be, nu: (i, 0)),
        scratch_shapes=[pltpu.VMEM((d, ff2), BF16), pltpu.VMEM((ff2 // 2, d), BF16)],
    )
    return pl.pallas_call(
        _experts_kernel,
        grid_spec=grid_spec,
        out_shape=jax.ShapeDtypeStruct((n_rows * nslab, LANES), F32),
        compiler_params=_cparams("arbitrary"),
        name="moe_experts",
    )(block_e, n_used, xs, wgu_all, wd_all)


def _final_norm_kernel(meta_cur, meta_nxt, h1_ref, route_ref, ys_ref, gfin_ref, o_ref, buf_ref, gsem):
    _issue_residual_gathers(meta_cur, meta_nxt, ys_ref, buf_ref, gsem, tm=h1_ref.shape[0], sub=1)
    o_ref[...] = _rms(_gathered_residual(h1_ref, route_ref, buf_ref, gsem), gfin_ref[...], EPS)


def _final_norm(src, gfin):
    t, d = src.h1.shape
    tm = src.meta.shape[2] // TOP_K
    _, _, operands, in_specs, out_specs, out_shape, scratch = _residual_specs(src, tm, d)
    grid_spec = pltpu.PrefetchScalarGridSpec(
        num_scalar_prefetch=0,
        grid=(t // tm,),
        in_specs=in_specs + [pl.BlockSpec((1, d), lambda i: (0, 0))],
        out_specs=out_specs[0],
        scratch_shapes=scratch,
    )
    return pl.pallas_call(
        _final_norm_kernel,
        grid_spec=grid_spec,
        out_shape=out_shape[0],
        compiler_params=_cparams("arbitrary"),
        name="moe_combine_final_norm",
    )(*operands, gfin)


def _moe(a, w_out_all, mixer_layer, h, layer, g_ffn, wr_g, br_g, wr_e, br_e, w_gu_all, w_down_all):
    t, d = h.shape
    bm = MOE_BLOCK
    tm = min(ROW_TILE, t)
    wr = jnp.concatenate([wr_g, wr_e, jnp.zeros((d, LANES - N_GROUPS - N_EXPERTS), F32)], axis=1)
    wrh = wr.astype(BF16)
    wrl = (wr - wrh.astype(F32)).astype(BF16)
    br = jnp.concatenate([br_g, br_e, jnp.zeros((LANES - N_GROUPS - N_EXPERTS,), F32)])[None, :]
    h1, hn, route, meta, cnt = _outproj_router(a, w_out_all, mixer_layer, h, g_ffn[None, :], wrh, wrl, br, tm=tm)

    counts = cnt[0, ROUTE_LANE0:ROUTE_LANE0 + N_EXPERTS].astype(I32)
    nb = (counts + bm - 1) // bm
    nb_end = jnp.cumsum(nb)
    pstart = ((nb_end - nb) * bm).astype(I32)
    n_blocks = (t * TOP_K + N_EXPERTS * (bm - 1)) // bm
    j = jnp.arange(n_blocks, dtype=I32)
    block_e = jnp.minimum(jnp.sum((nb_end[None, :] <= j[:, None]).astype(I32), axis=1), N_EXPERTS - 1)
    n_used = nb_end[-1:].astype(I32)

    eid = meta[:, 0:TOP_K, :]
    base = jnp.sum(jnp.where(eid[..., None] == jnp.arange(N_EXPERTS, dtype=I32), pstart, 0), axis=-1)
    rows = (meta[:, TOP_K:2 * TOP_K, :] + base) * (d // LANES)
    rows = rows.transpose(0, 2, 1).reshape(rows.shape[0], 1, TOP_K * rows.shape[2])

    xs = _dispatch(counts, pstart, n_used, rows, hn, n_blocks * bm, bm=bm)
    ys = _experts(block_e, n_used, xs, w_gu_all, w_down_all, layer, bm=bm)
    return _Pending(rows, h1, route, ys)


def _fox_mixer(src, g, w_in_all, layer, b_f, *, batch, seq):
    d = w_in_all.shape[1]
    nh = FOX_HEADS
    wf = w_in_all[layer, :, 4 * d:4 * d + nh]
    wf = jnp.concatenate([wf] * 6 + [jnp.zeros((d, LANES - 6 * nh), F32)], axis=1).astype(BF16)
    bfr = jnp.concatenate([b_f] * 6 + [jnp.zeros((LANES - 6 * nh,), F32)])[None, :]
    h, q, kb, v, gate, a = _fox_inproj(src, g[None, :], jnp.swapaxes(w_in_all, 1, 2), layer, wf, bfr, seq=seq)
    return h, _fox_attn(q, a, kb, v, gate, batch=batch, seq=seq)


def _ret_mixer(src, g, w_in_all, layer, gn_gain, *, batch, seq):
    d = w_in_all.shape[1]
    nh = RET_HEADS
    dk = d // nh
    c = RET_CHUNK
    inv = 1.0 / (ROPE_BASE ** jnp.linspace(0.0, 1.0, dk // 2, dtype=F32))
    ang = jnp.arange(seq).astype(F32)[:, None] * inv[None, :]
    cos, sin = jnp.cos(ang), jnp.sin(ang)
    log_g = jnp.log(1.0 - 2.0 ** (-5.0 - jnp.arange(nh, dtype=F32)))
    idx = jnp.arange(c, dtype=F32)
    rel = idx[:, None] - idx[None, :]
    intra = jnp.where(rel >= 0, jnp.exp(log_g[:, None, None] * jnp.maximum(rel, 0.0)), 0.0)
    qd = jnp.exp(log_g[:, None] * (idx + 1.0))[:, :, None]
    kd = jnp.exp(log_g[:, None] * (c - 1.0 - idx))[:, :, None]
    cd = jnp.exp(log_g * c)[:, None, None]
    h, q, k, v, gate = _ret_inproj(src, g[None, :], w_in_all, layer, cos, sin, seq=seq)
    return h, _retention(q, k, v, gate, gn_gain[None, :], intra, qd, kd, cd, batch=batch, seq=seq)


def kernel(x, fox_w_in, fox_b_f, fox_w_out, ret_w_in, ret_gn_gain, ret_w_out, norm_mix, norm_ffn,
           router_group_w, router_group_b, router_expert_w, router_expert_b, expert_w_gu, expert_w_down,
           norm_final):
    batch, seq, d = x.shape
    depth = norm_mix.shape[0]
    src = x.reshape(batch * seq, d)
    for i in range(depth):
        j = i // 2
        if i % 2 == 0:
            h, a = _fox_mixer(src, norm_mix[i], fox_w_in, j, fox_b_f[j], batch=batch, seq=seq)
            w_out_all = fox_w_out
        else:
            h, a = _ret_mixer(src, norm_mix[i], ret_w_in, j, ret_gn_gain[j], batch=batch, seq=seq)
            w_out_all = ret_w_out
        src = _moe(a, w_out_all, j, h, i, norm_ffn[i], router_group_w[i], router_group_b[i], router_expert_w[i],
                   router_expert_b[i], expert_w_gu, expert_w_down)
    return _final_norm(src, norm_final[None, :]).reshape(batch, seq, d)
```

```python
import functools
from typing import NamedTuple

import jax
import jax.numpy as jnp
from jax import lax
from jax.experimental import pallas as pl
from jax.experimental.pallas import tpu as pltpu

F32 = jnp.float32
BF16 = jnp.bfloat16
I32 = jnp.int32

FOX_HEADS = 16
RET_HEADS = 4
RET_CHUNK = 128
N_GROUPS = 4
EXPERTS_PER_GROUP = 8
N_EXPERTS = N_GROUPS * EXPERTS_PER_GROUP
TOP_K = 2
EPS = 1e-6
NEG_INF = -1e30
ROPE_BASE = 10000.0

LANES = 128
SUBLANES = 8
ROUTE_LANE0 = N_GROUPS
VMEM_LIMIT = 56 * 1024 * 1024
MOE_BLOCK = 512
ROW_TILE = 512
WEIGHT_CHUNK = 512

_CONTRACT_LAST = (((1,), (1,)), ((), ()))
_CONTRACT_FIRST = (((0,), (0,)), ((), ()))


def _cparams(*sem):
    return pltpu.CompilerParams(dimension_semantics=sem, vmem_limit_bytes=VMEM_LIMIT)


def _sigmoid(x):
    return 1.0 / (1.0 + jnp.exp(-x))


def _rms(x, g, eps):
    ms = jnp.mean(x * x, axis=-1, keepdims=True)
    return x * lax.rsqrt(ms + eps) * g


def _split3(x):
    hi = x.astype(BF16)
    r = x - hi.astype(F32)
    mid = r.astype(BF16)
    lo = (r - mid.astype(F32)).astype(BF16)
    return hi, mid, lo


def _stage_weight(w_hbm, wb_ref, stage_ref, sem, ncols):
    chunk = stage_ref.shape[2]
    n = ncols // chunk

    def cp(c):
        return pltpu.make_async_copy(w_hbm.at[:, pl.ds(c * chunk, chunk)], stage_ref.at[c % 2], sem.at[c % 2])

    cp(0).start()
    for c in range(n):
        if c + 1 < n:
            cp(c + 1).start()
        cp(c).wait()
        wb_ref[:, c * chunk:(c + 1) * chunk] = stage_ref[c % 2].astype(BF16)


def _stage_weight_t(wt_hbm, wb_ref, stage_ref, sem, ncols):
    chunk = stage_ref.shape[1]
    n = ncols // chunk

    def cp(c):
        return pltpu.make_async_copy(wt_hbm.at[pl.ds(c * chunk, chunk), :], stage_ref.at[c % 2], sem.at[c % 2])

    cp(0).start()
    for c in range(n):
        if c + 1 < n:
            cp(c + 1).start()
        cp(c).wait()
        wb_ref[:, c * chunk:(c + 1) * chunk] = jnp.transpose(stage_ref[c % 2]).astype(BF16)


def _rows_to_slabs(ref, x):
    n, d = x.shape
    nslab = d // LANES
    for c in range(nslab):
        ref[pl.ds(c, n, stride=nslab), :] = x[:, c * LANES:(c + 1) * LANES]


def _slabs_to_rows(ref, n, nslab):
    return jnp.concatenate([ref[pl.ds(c, n, stride=nslab), :] for c in range(nslab)], axis=1)


def _slab_rows(ref, row, nrows, nslab):
    return ref.at[pl.ds(pl.multiple_of(row * nslab, nslab), nrows * nslab), :]


def _slab_at(ref, first, nslab):
    return ref.at[pl.ds(pl.multiple_of(first, nslab), nslab), :]


class _Pending(NamedTuple):
    meta: jax.Array
    h1: jax.Array
    route: jax.Array
    ys: jax.Array


def _issue_residual_gathers(meta_cur, meta_nxt, ys_ref, buf_ref, gsem, *, tm, sub):
    i = pl.program_id(0)
    nslab = buf_ref.shape[2] // tm

    def issue(meta_ref, step):
        slot = step % 2
        off = (step % sub) * tm

        def body(t, carry):
            for k in range(TOP_K):
                src = meta_ref[0, TOP_K * (off + t) + k]
                pltpu.make_async_copy(_slab_at(ys_ref, src, nslab), _slab_rows(buf_ref.at[slot, k], t, 1, nslab),
                                      gsem.at[slot]).start(priority=k)
            return carry

        lax.fori_loop(0, tm, body, 0, unroll=8)

    @pl.when(i == 0)
    def _():
        issue(meta_cur, i)

    @pl.when(i + 1 < pl.num_programs(0))
    def _():
        issue(meta_nxt, i + 1)


def _gathered_residual(h1_ref, route_ref, buf_ref, gsem):
    tm, d = h1_ref.shape
    nslab = d // LANES
    slot = pl.program_id(0) % 2
    pltpu.make_async_copy(buf_ref.at[slot], buf_ref.at[slot], gsem.at[slot]).wait()
    g0 = route_ref[:, 4:5]
    g1 = route_ref[:, 5:6]
    return h1_ref[...] + (_slabs_to_rows(buf_ref.at[slot, 0], tm, nslab) * g0
                          + _slabs_to_rows(buf_ref.at[slot, 1], tm, nslab) * g1)


def _residual_specs(src, tm, d):
    if not isinstance(src, _Pending):
        return False, 1, [src], [pl.BlockSpec((tm, d), lambda i: (i, 0))], [], [], []
    tile = src.meta.shape[2] // TOP_K
    sub = tile // tm
    last = src.h1.shape[0] // tm - 1
    smem = lambda f: pl.BlockSpec((None, 1, TOP_K * tile), f, memory_space=pltpu.SMEM)
    in_specs = [smem(lambda i: (i // sub, 0, 0)), smem(lambda i: (jnp.minimum(i + 1, last) // sub, 0, 0)),
                pl.BlockSpec((tm, d), lambda i: (i, 0)), pl.BlockSpec((tm, LANES), lambda i: (i, 0)),
                pl.BlockSpec(memory_space=pl.ANY)]
    scratch = [pltpu.VMEM((2, TOP_K, tm * d // LANES, LANES), F32), pltpu.SemaphoreType.DMA((2,))]
    out_specs = [pl.BlockSpec((tm, d), lambda i: (i, 0))]
    out_shape = [jax.ShapeDtypeStruct(src.h1.shape, F32)]
    return True, sub, [src.meta, src.meta, src.h1, src.route, src.ys], in_specs, out_specs, out_shape, scratch


def _weight_scratch(kdim, ncols, transposed=False):
    stage = (2, WEIGHT_CHUNK, kdim) if transposed else (2, kdim, WEIGHT_CHUNK)
    return [pltpu.VMEM((kdim, ncols), BF16), pltpu.VMEM(stage, F32), pltpu.SemaphoreType.DMA((2,))]


def _fox_inproj_kernel(*refs, layer, tiles_per_seq, scale, fused, sub):
    i = pl.program_id(0)
    if fused:
        (meta_cur, meta_nxt, h1_ref, route_ref, ys_ref, g_ref, w_hbm, wf_ref, bf_ref, hout_ref, q_ref, kb_ref,
         v_ref, gate_ref, a_ref, carry_ref, wb_ref, stage_ref, sem, buf_ref, gsem) = refs
        _issue_residual_gathers(meta_cur, meta_nxt, ys_ref, buf_ref, gsem, tm=h1_ref.shape[0], sub=sub)
    else:
        (h_ref, g_ref, w_hbm, wf_ref, bf_ref, q_ref, kb_ref, v_ref, gate_ref, a_ref,
         carry_ref, wb_ref, stage_ref, sem) = refs
    tm, d = q_ref.shape

    @pl.when(i == 0)
    def _():
        _stage_weight_t(w_hbm.at[layer], wb_ref, stage_ref, sem, 4 * d)

    if fused:
        x = _gathered_residual(h1_ref, route_ref, buf_ref, gsem)
        hout_ref[...] = x
    else:
        x = h_ref[...]
    xn = _rms(x, g_ref[...], EPS).astype(BF16)

    q = jnp.dot(xn, wb_ref[:, 0:d], preferred_element_type=F32)
    q_ref[...] = (q * scale).astype(BF16)
    v_ref[...] = jnp.dot(xn, wb_ref[:, 2 * d:3 * d], preferred_element_type=F32).astype(BF16)
    gate_ref[...] = jnp.dot(xn, wb_ref[:, 3 * d:4 * d], preferred_element_type=F32)

    f = jnp.dot(xn, wf_ref[...], preferred_element_type=F32) + bf_ref[...]
    ls = jnp.minimum(f, 0.0) - jnp.log1p(jnp.exp(-jnp.abs(f)))

    @pl.when(i % tiles_per_seq == 0)
    def _():
        carry_ref[...] = jnp.zeros_like(carry_ref)

    row = lax.broadcasted_iota(I32, (tm, tm), 0)
    col = lax.broadcasted_iota(I32, (tm, tm), 1)
    tri = jnp.where(col <= row, 1.0, 0.0).astype(BF16)
    hi, mid, lo = _split3(ls)
    cum = (jnp.dot(tri, hi, preferred_element_type=F32) + jnp.dot(tri, mid, preferred_element_type=F32)
           + jnp.dot(tri, lo, preferred_element_type=F32)) + carry_ref[...]
    carry_ref[...] = cum[tm - 1:tm, :]

    c_hi, c_mid, c_lo = (c.astype(F32) for c in _split3(cum))
    lane = lax.broadcasted_iota(I32, (tm, LANES), 1)
    a_ref[...] = jnp.where(lane < 16, c_hi, jnp.where(lane < 32, c_mid, jnp.where(
        lane < 48, c_lo, jnp.where(lane < 96, 1.0, 0.0)))).astype(BF16)
    bk = jnp.where(lane < 48, 1.0, jnp.where(lane < 64, -c_hi, jnp.where(
        lane < 80, -c_mid, jnp.where(lane < 96, -c_lo, 0.0)))).astype(BF16)

    k = jnp.dot(xn, wb_ref[:, d:2 * d], preferred_element_type=F32).astype(BF16)
    for p in range(d // LANES):
        kb_ref[:, 2 * p * LANES:(2 * p + 1) * LANES] = k[:, p * LANES:(p + 1) * LANES]
        kb_ref[:, (2 * p + 1) * LANES:(2 * p + 2) * LANES] = bk


def _fox_inproj(src, g, w_all, layer, wf, bfr, *, seq):
    t, d = src.h1.shape if isinstance(src, _Pending) else src.shape
    tm = min(ROW_TILE, seq)
    scale = float((d // FOX_HEADS) ** -0.5)
    fused, sub, operands, in_specs, out_specs, out_shape, scratch = _residual_specs(src, tm, d)
    kern = functools.partial(_fox_inproj_kernel, layer=layer, tiles_per_seq=seq // tm, scale=scale,
                             fused=fused, sub=sub)
    row = lambda i: (i, 0)
    const = lambda i: (0, 0)
    grid_spec = pltpu.PrefetchScalarGridSpec(
        num_scalar_prefetch=0,
        grid=(t // tm,),
        in_specs=in_specs + [pl.BlockSpec((1, d), const), pl.BlockSpec(memory_space=pl.ANY),
                             pl.BlockSpec((d, LANES), const), pl.BlockSpec((1, LANES), const)],
        out_specs=out_specs + [pl.BlockSpec((tm, d), row), pl.BlockSpec((tm, 2 * d), row),
                               pl.BlockSpec((tm, d), row), pl.BlockSpec((tm, d), row),
                               pl.BlockSpec((tm, LANES), row)],
        scratch_shapes=[pltpu.VMEM((1, LANES), F32)] + _weight_scratch(d, 4 * d, transposed=True) + scratch,
    )
    outs = pl.pallas_call(
        kern,
        grid_spec=grid_spec,
        out_shape=out_shape + [jax.ShapeDtypeStruct((t, d), BF16), jax.ShapeDtypeStruct((t, 2 * d), BF16),
                               jax.ShapeDtypeStruct((t, d), BF16), jax.ShapeDtypeStruct((t, d), F32),
                               jax.ShapeDtypeStruct((t, LANES), BF16)],
        compiler_params=_cparams("arbitrary"),
        name="fox_inproj",
    )(*operands, g, w_all, wf, bfr)
    return (outs if fused else [src] + list(outs))


def _fox_attn_kernel(q_ref, a_ref, kb_ref, v_ref, gate_ref, o_ref, *, tq, dh, depth):
    seq = q_ref.shape[0]
    pair = pl.program_id(1)
    lane = lax.broadcasted_iota(I32, (tq, LANES), 1)
    krow = lax.broadcasted_iota(I32, (tq, tq), 0)
    qcol = lax.broadcasted_iota(I32, (tq, tq), 1)
    causal = krow <= qcol
    heads_per_block = LANES // dh
    qmask = [jnp.where((lane >= hh * dh) & (lane < (hh + 1) * dh), 1.0, 0.0).astype(BF16)
             for hh in range(heads_per_block)]
    amask = [jnp.where((lane < 96) & ((lane & 15) == pair * heads_per_block + hh), 1.0, 0.0).astype(BF16)
             for hh in range(heads_per_block)]
    vt = jnp.transpose(v_ref[...])
    chains = [(i, hh) for i in range(seq // tq) for hh in range(heads_per_block)]

    def scores(i, hh):
        off = i * tq
        qa = jnp.concatenate([q_ref[off:off + tq, :] * qmask[hh], a_ref[off:off + tq, :] * amask[hh]], axis=1)
        return lax.dot_general(kb_ref[0:off + tq, :], qa, _CONTRACT_LAST, preferred_element_type=F32)

    def finish(i, hh, st):
        off = i * tq
        sd = jnp.where(causal, st[off:off + tq, :], NEG_INF)
        st = jnp.concatenate([st[0:off, :], sd], axis=0) if off > 0 else sd
        m = jnp.max(st, axis=0, keepdims=True)
        e = jnp.exp(st - m)
        inv = 1.0 / jnp.sum(e, axis=0, keepdims=True)
        return jnp.dot(vt[hh * dh:(hh + 1) * dh, 0:off + tq], e.astype(BF16), preferred_element_type=F32) * inv

    outs = {}
    queue = [scores(*c) for c in chains[:depth]]
    for n, (i, hh) in enumerate(chains):
        cur = queue.pop(0)
        if n + depth < len(chains):
            queue.append(scores(*chains[n + depth]))
        outs[hh] = finish(i, hh, cur)
        if hh == heads_per_block - 1:
            off = i * tq
            o = jnp.transpose(jnp.concatenate([outs[h2] for h2 in range(heads_per_block)], axis=0))
            o_ref[off:off + tq, :] = (o * _sigmoid(gate_ref[off:off + tq, :])).astype(BF16)


def _fox_attn(q, a, kb, v, gate, *, batch, seq, tq=256, depth=3):
    t, d = q.shape
    dh = d // FOX_HEADS
    pairs = d // LANES
    kern = functools.partial(_fox_attn_kernel, tq=tq, dh=dh, depth=depth)
    blk = lambda b, p: (b, p)
    return pl.pallas_call(
        kern,
        grid=(batch, pairs),
        in_specs=[pl.BlockSpec((seq, LANES), blk), pl.BlockSpec((seq, LANES), lambda b, p: (b, 0)),
                  pl.BlockSpec((seq, 2 * LANES), blk), pl.BlockSpec((seq, LANES), blk),
                  pl.BlockSpec((seq, LANES), blk)],
        out_specs=pl.BlockSpec((seq, LANES), blk),
        out_shape=jax.ShapeDtypeStruct((t, d), BF16),
        compiler_params=_cparams("arbitrary", "arbitrary"),
        name="fox_attn",
    )(q, a, kb, v, gate)


def _ret_inproj_kernel(*refs, layer, kscale, fused, sub):
    if fused:
        (meta_cur, meta_nxt, h1_ref, route_ref, ys_ref, g_ref, w_hbm, cos_ref, sin_ref, hout_ref, q_ref, k_ref,
         v_ref, gate_ref, wb_ref, stage_ref, sem, buf_ref, gsem) = refs
        _issue_residual_gathers(meta_cur, meta_nxt, ys_ref, buf_ref, gsem, tm=h1_ref.shape[0], sub=sub)
    else:
        h_ref, g_ref, w_hbm, cos_ref, sin_ref, q_ref, k_ref, v_ref, gate_ref, wb_ref, stage_ref, sem = refs
    tm, d = q_ref.shape

    @pl.when(pl.program_id(0) == 0)
    def _():
        _stage_weight(w_hbm.at[layer], wb_ref, stage_ref, sem, 6 * d)

    if fused:
        x = _gathered_residual(h1_ref, route_ref, buf_ref, gsem)
        hout_ref[...] = x
    else:
        x = h_ref[...]
    xn = _rms(x, g_ref[...], EPS).astype(BF16)
    cos = cos_ref[...]
    sin = sin_ref[...]
    dk = d // RET_HEADS
    half = dk // 2

    def rot(t, h):
        t1 = t[:, h * dk:h * dk + half]
        t2 = t[:, h * dk + half:(h + 1) * dk]
        return t1 * cos - t2 * sin, t1 * sin + t2 * cos

    q = jnp.dot(xn, wb_ref[:, 0:d], preferred_element_type=F32)
    k = jnp.dot(xn, wb_ref[:, d:2 * d], preferred_element_type=F32)
    for h in range(RET_HEADS):
        q1, q2 = rot(q, h)
        q_ref[:, h * dk:h * dk + half] = q1.astype(BF16)
        q_ref[:, h * dk + half:(h + 1) * dk] = q2.astype(BF16)
        k1, k2 = rot(k, h)
        k_ref[:, h * dk:h * dk + half] = k1 * kscale
        k_ref[:, h * dk + half:(h + 1) * dk] = k2 * kscale
    v_ref[...] = jnp.dot(xn, wb_ref[:, 2 * d:4 * d], preferred_element_type=F32).astype(BF16)
    gate_ref[...] = jnp.dot(xn, wb_ref[:, 4 * d:6 * d], preferred_element_type=F32)


def _ret_inproj(src, g, w_all, layer, cos, sin, *, seq, tm=ROW_TILE):
    t, d = src.h1.shape if isinstance(src, _Pending) else src.shape
    tm = min(tm, seq)
    half = d // RET_HEADS // 2
    fused, sub, operands, in_specs, out_specs, out_shape, scratch = _residual_specs(src, tm, d)
    kern = functools.partial(_ret_inproj_kernel, layer=layer, kscale=float((d // RET_HEADS) ** -0.5),
                             fused=fused, sub=sub)
    row = lambda i: (i, 0)
    const = lambda i: (0, 0)
    pos = lambda i: (i % (seq // tm), 0)
    grid_spec = pltpu.PrefetchScalarGridSpec(
        num_scalar_prefetch=0,
        grid=(t // tm,),
        in_specs=in_specs + [pl.BlockSpec((1, d), const), pl.BlockSpec(memory_space=pl.ANY),
                             pl.BlockSpec((tm, half), pos), pl.BlockSpec((tm, half), pos)],
        out_specs=out_specs + [pl.BlockSpec((tm, d), row), pl.BlockSpec((tm, d), row),
                               pl.BlockSpec((tm, 2 * d), row), pl.BlockSpec((tm, 2 * d), row)],
        scratch_shapes=_weight_scratch(d, 6 * d) + scratch,
    )
    outs = pl.pallas_call(
        kern,
        grid_spec=grid_spec,
        out_shape=out_shape + [jax.ShapeDtypeStruct((t, d), BF16), jax.ShapeDtypeStruct((t, d), F32),
                               jax.ShapeDtypeStruct((t, 2 * d), BF16), jax.ShapeDtypeStruct((t, 2 * d), F32)],
        compiler_params=_cparams("arbitrary"),
        name="ret_inproj",
    )(*operands, g, w_all, cos, sin)
    return (outs if fused else [src] + list(outs))


def _retention_kernel(q_ref, k_ref, v_ref, gate_ref, gain_ref, intra_ref, qd_ref, kd_ref, cd_ref,
                      y_ref, r_ref, *, chunk):
    seq = q_ref.shape[0]
    r_ref[...] = jnp.zeros_like(r_ref)

    def body(c):
        off = c * chunk
        qn = q_ref[pl.ds(off, chunk), :]
        k32 = k_ref[pl.ds(off, chunk), :]
        vn = v_ref[pl.ds(off, chunk), :]
        s = lax.dot_general(qn, k32.astype(BF16), _CONTRACT_LAST, preferred_element_type=F32) * intra_ref[...]
        o = jnp.dot(s.astype(BF16), vn, preferred_element_type=F32)
        r = r_ref[...]
        o = o + jnp.dot(qn, r.astype(BF16), preferred_element_type=F32) * qd_ref[...]
        kdec = (k32 * kd_ref[...]).astype(BF16)
        r_ref[...] = r * cd_ref[...] + lax.dot_general(kdec, vn, _CONTRACT_FIRST, preferred_element_type=F32)
        mu = jnp.mean(o, axis=-1, keepdims=True)
        dlt = o - mu
        var = jnp.mean(dlt * dlt, axis=-1, keepdims=True)
        on = dlt * lax.rsqrt(var + EPS) * gain_ref[...]
        g = gate_ref[pl.ds(off, chunk), :]
        y_ref[pl.ds(off, chunk), :] = ((g * _sigmoid(g)) * on).astype(BF16)

    for c in range(seq // chunk):
        body(c)


def _retention(q, k, v, gate, gain, intra, qd, kd, cd, *, batch, seq):
    t, d = q.shape
    dk = d // RET_HEADS
    dv = v.shape[1] // RET_HEADS
    c = RET_CHUNK
    kern = functools.partial(_retention_kernel, chunk=c)
    blk = lambda b, h: (b, h)
    hd = lambda b, h: (h, 0, 0)
    return pl.pallas_call(
        kern,
        grid=(batch, RET_HEADS),
        in_specs=[pl.BlockSpec((seq, dk), blk), pl.BlockSpec((seq, dk), blk), pl.BlockSpec((seq, dv), blk),
                  pl.BlockSpec((seq, dv), blk), pl.BlockSpec((1, dv), lambda b, h: (0, h)),
                  pl.BlockSpec((None, c, c), hd), pl.BlockSpec((None, c, 1), hd),
                  pl.BlockSpec((None, c, 1), hd), pl.BlockSpec((None, 1, 1), hd)],
        out_specs=pl.BlockSpec((seq, dv), blk),
        out_shape=jax.ShapeDtypeStruct((t, v.shape[1]), BF16),
        scratch_shapes=[pltpu.VMEM((dk, dv), F32)],
        compiler_params=_cparams("arbitrary", "arbitrary"),
        name="retention",
    )(q, k, v, gate, gain, intra, qd, kd, cd)


def _outproj_router_kernel(a_ref, w_hbm, h_ref, g_ref, wrh_ref, wrl_ref, br_ref,
                           h1_ref, hn_ref, route_ref, meta_ref, cnt_ref, wb_ref, stage_ref, sem, *, layer):
    i = pl.program_id(0)
    tm = a_ref.shape[0]
    d = h_ref.shape[1]

    @pl.when(i == 0)
    def _():
        cnt_ref[...] = jnp.zeros_like(cnt_ref)
        _stage_weight(w_hbm.at[layer], wb_ref, stage_ref, sem, d)

    h1 = h_ref[...] + jnp.dot(a_ref[...], wb_ref[...], preferred_element_type=F32)
    h1_ref[...] = h1
    hn = _rms(h1, g_ref[...], EPS)
    _rows_to_slabs(hn_ref, hn)

    hh = hn.astype(BF16)
    hl = (hn - hh.astype(F32)).astype(BF16)
    lg = (jnp.dot(hh, wrh_ref[...], preferred_element_type=F32)
          + jnp.dot(hh, wrl_ref[...], preferred_element_type=F32)
          + jnp.dot(hl, wrh_ref[...], preferred_element_type=F32)) + br_ref[...]

    lane = lax.broadcasted_iota(I32, (tm, LANES), 1)
    lanef = lane.astype(F32)
    big = float(LANES)

    def softmax_masked(mask):
        mx = jnp.max(jnp.where(mask, lg, -jnp.inf), axis=1, keepdims=True)
        e = jnp.where(mask, jnp.exp(lg - mx), 0.0)
        return e / jnp.sum(e, axis=1, keepdims=True)

    def top1(p, mask):
        best = jnp.max(jnp.where(mask, p, -1.0), axis=1, keepdims=True)
        idx = jnp.min(jnp.where(mask & (p == best), lanef, big), axis=1, keepdims=True)
        return best, idx

    gmask = lane < N_GROUPS
    pg, gi = top1(softmax_masked(gmask), gmask)
    lo = ROUTE_LANE0 + gi * EXPERTS_PER_GROUP
    emask = (lanef >= lo) & (lanef < lo + EXPERTS_PER_GROUP)
    pe = softmax_masked(emask)
    p1, i1 = top1(pe, emask)
    p2, i2 = top1(pe, emask & (lanef != i1))
    den = p1 + p2
    g1 = pg * p1 / den
    g2 = pg * p2 / den

    oh1 = lanef == i1
    oh2 = lanef == i2
    both = jnp.where(oh1 | oh2, 1.0, 0.0)
    row = lax.broadcasted_iota(I32, (tm, tm), 0)
    col = lax.broadcasted_iota(I32, (tm, tm), 1)
    strict = jnp.where(col < row, 1.0, 0.0).astype(BF16)
    before = jnp.dot(strict, both.astype(BF16), preferred_element_type=F32) + cnt_ref[0:1, :]
    r1 = jnp.sum(jnp.where(oh1, before, 0.0), axis=1, keepdims=True)
    r2 = jnp.sum(jnp.where(oh2, before, 0.0), axis=1, keepdims=True)
    cnt_ref[0:1, :] = cnt_ref[0:1, :] + jnp.sum(both, axis=0, keepdims=True)

    e1 = i1 - ROUTE_LANE0
    e2 = i2 - ROUTE_LANE0
    route = jnp.where(lane == 0, e1, jnp.where(lane == 1, e2, jnp.where(
        lane == 2, r1, jnp.where(lane == 3, r2, jnp.where(lane == 4, g1, jnp.where(lane == 5, g2, 0.0))))))
    route_ref[...] = route
    meta_ref[...] = jnp.transpose(route)[0:SUBLANES, :].astype(I32)


def _outproj_router(a, w_all, layer, h, g, wrh, wrl, br, *, tm):
    t, kdim = a.shape
    d = h.shape[1]
    row = lambda i: (i, 0)
    const = lambda i: (0, 0)
    kern = functools.partial(_outproj_router_kernel, layer=layer)
    return pl.pallas_call(
        kern,
        grid=(t // tm,),
        in_specs=[pl.BlockSpec((tm, kdim), row), pl.BlockSpec(memory_space=pl.ANY), pl.BlockSpec((tm, d), row),
                  pl.BlockSpec((1, d), const), pl.BlockSpec((d, LANES), const), pl.BlockSpec((d, LANES), const),
                  pl.BlockSpec((1, LANES), const)],
        out_specs=[pl.BlockSpec((tm, d), row), pl.BlockSpec((tm * d // LANES, LANES), row),
                   pl.BlockSpec((tm, LANES), row), pl.BlockSpec((None, SUBLANES, tm), lambda i: (i, 0, 0)),
                   pl.BlockSpec((SUBLANES, LANES), const)],
        out_shape=[jax.ShapeDtypeStruct((t, d), F32), jax.ShapeDtypeStruct((t * d // LANES, LANES), F32),
                   jax.ShapeDtypeStruct((t, LANES), F32), jax.ShapeDtypeStruct((t // tm, SUBLANES, tm), I32),
                   jax.ShapeDtypeStruct((SUBLANES, LANES), F32)],
        scratch_shapes=_weight_scratch(kdim, d),
        compiler_params=_cparams("arbitrary"),
        name="outproj_router",
    )(a, w_all, h, g, wrh, wrl, br)


def _pad_copies(counts_ref, pstart_ref, zero_ref, xs_ref, sem, *, bm):
    nslab = SUBLANES
    out = []
    for e in range(N_EXPERTS):
        cnt = counts_ref[e]
        npad = (bm - (cnt & (bm - 1))) & (bm - 1)
        pos = pstart_ref[e] + cnt
        bit = 1
        while bit < bm:
            out.append(((npad & bit) != 0, pltpu.make_async_copy(
                _slab_rows(zero_ref, 0, bit, nslab), _slab_rows(xs_ref, pos, bit, nslab), sem)))
            pos = pos + (npad & bit)
            bit *= 2
    return out


def _dispatch_kernel(counts_ref, pstart_ref, nused_ref, meta_ref, hn_ref, xs_ref, zero_ref, sem, psem, *, bm):
    nslab = SUBLANES
    tiles = meta_ref.shape[0]
    tm = hn_ref.shape[0] // (nslab * tiles)
    half = bm // 2

    @pl.when(pl.program_id(0) == 0)
    def _():
        zero_ref[...] = jnp.zeros_like(zero_ref)
        pads = _pad_copies(counts_ref, pstart_ref, zero_ref, xs_ref, psem, bm=bm)
        for pred, cp in pads:
            pl.when(pred)(cp.start)

        def tail_copy(j):
            return pltpu.make_async_copy(zero_ref, _slab_rows(xs_ref, j * half, half, nslab), psem)

        first = nused_ref[0] * 2
        last = xs_ref.shape[0] // (half * nslab)
        lax.fori_loop(first, last, lambda j, c: (tail_copy(j).start(), c)[1], 0)
        for pred, cp in pads:
            pl.when(pred)(cp.wait)
        lax.fori_loop(first, last, lambda j, c: (tail_copy(j).wait(), c)[1], 0)

    def row_copy(j, t, k):
        dst = meta_ref[j, 0, TOP_K * t + k]
        return pltpu.make_async_copy(_slab_rows(hn_ref, j * tm + t, 1, nslab), _slab_at(xs_ref, dst, nslab), sem)

    def issue(j):
        def body(t, carry):
            for k in range(TOP_K):
                row_copy(j, t, k).start(priority=k)
            return carry
        return body

    def drain(j):
        def body(t, carry):
            for k in range(TOP_K):
                row_copy(j, t, k).wait()
            return carry
        return body

    for j in range(tiles):
        lax.fori_loop(0, tm, issue(j), 0, unroll=8)
    for j in range(tiles):
        lax.fori_loop(0, tm, drain(j), 0, unroll=8)


def _dispatch(counts, pstart, n_used, meta, hn, n_rows, *, bm):
    nslab = SUBLANES
    t = hn.shape[0] // nslab
    tm = meta.shape[2] // TOP_K
    tiles = 2 if meta.shape[0] % 2 == 0 else 1
    kern = functools.partial(_dispatch_kernel, bm=bm)
    grid_spec = pltpu.PrefetchScalarGridSpec(
        num_scalar_prefetch=3,
        grid=(t // (tm * tiles),),
        in_specs=[pl.BlockSpec((tiles, 1, TOP_K * tm), lambda i, *_: (i, 0, 0), memory_space=pltpu.SMEM),
                  pl.BlockSpec((tiles * tm * nslab, LANES), lambda i, *_: (i, 0))],
        out_specs=pl.BlockSpec(memory_space=pl.ANY),
        scratch_shapes=[pltpu.VMEM((bm // 2 * nslab, LANES), F32), pltpu.SemaphoreType.DMA(()),
                        pltpu.SemaphoreType.DMA(())],
    )
    return pl.pallas_call(
        kern,
        grid_spec=grid_spec,
        out_shape=jax.ShapeDtypeStruct((n_rows * nslab, LANES), F32),
        compiler_params=_cparams("arbitrary"),
        name="moe_dispatch",
    )(counts, pstart, n_used, meta, hn)


def _experts_kernel(be_ref, nu_ref, x_ref, wgu_ref, wd_ref, y_ref, wgu_b, wd_b):
    i = pl.program_id(0)
    d = wgu_b.shape[0]
    bm = x_ref.shape[0] * LANES // d
    prev = be_ref[jnp.maximum(i - 1, 0)]

    @pl.when((i < nu_ref[0]) & ((i == 0) | (be_ref[i] != prev)))
    def _():
        wgu_b[...] = wgu_ref[...].astype(BF16)
        wd_b[...] = wd_ref[...].astype(BF16)

    @pl.when(i < nu_ref[0])
    def _():
        x = _slabs_to_rows(x_ref, bm, d // LANES).astype(BF16)
        hcat = jnp.dot(x, wgu_b[...], preferred_element_type=F32)
        ff = hcat.shape[1] // 2
        a = hcat[:, :ff]
        b = hcat[:, ff:]
        act = ((a * _sigmoid(a)) * b).astype(BF16)
        _rows_to_slabs(y_ref, jnp.dot(act, wd_b[...], preferred_element_type=F32))

    @pl.when(i >= nu_ref[0])
    def _():
        y_ref[...] = jnp.zeros_like(y_ref)


def _experts(block_e, n_used, xs, wgu_all, wd_all, layer, *, bm):
    d = wgu_all.shape[2]
    nslab = d // LANES
    n_rows = xs.shape[0] // nslab
    ff2 = wgu_all.shape[3]
    grid_spec = pltpu.PrefetchScalarGridSpec(
        num_scalar_prefetch=2,
        grid=(n_rows // bm,),
        in_specs=[pl.BlockSpec((bm * nslab, LANES), lambda i, be, nu: (jnp.minimum(i, nu[0] - 1), 0)),
                  pl.BlockSpec((None, None, d, ff2), lambda i, be, nu: (layer, be[i], 0, 0)),
                  pl.BlockSpec((None, None, ff2 // 2, d), lambda i, be, nu: (layer, be[i], 0, 0))],
        out_specs=pl.BlockSpec((bm * nslab, LANES), lambda i, be, nu: (i, 0)),
        scratch_shapes=[pltpu.VMEM((d, ff2), BF16), pltpu.VMEM((ff2 // 2, d), BF16)],
    )
    return pl.pallas_call(
        _experts_kernel,
        grid_spec=grid_spec,
        out_shape=jax.ShapeDtypeStruct((n_rows * nslab, LANES), F32),
        compiler_params=_cparams("arbitrary"),
        name="moe_experts",
    )(block_e, n_used, xs, wgu_all, wd_all)


def _final_norm_kernel(meta_cur, meta_nxt, h1_ref, route_ref, ys_ref, gfin_ref, o_ref, buf_ref, gsem):
    _issue_residual_gathers(meta_cur, meta_nxt, ys_ref, buf_ref, gsem, tm=h1_ref.shape[0], sub=1)
    o_ref[...] = _rms(_gathered_residual(h1_ref, route_ref, buf_ref, gsem), gfin_ref[...], EPS)


def _final_norm(src, gfin):
    t, d = src.h1.shape
    tm = src.meta.shape[2] // TOP_K
    _, _, operands, in_specs, out_specs, out_shape, scratch = _residual_specs(src, tm, d)
    grid_spec = pltpu.PrefetchScalarGridSpec(
        num_scalar_prefetch=0,
        grid=(t // tm,),
        in_specs=in_specs + [pl.BlockSpec((1, d), lambda i: (0, 0))],
        out_specs=out_specs[0],
        scratch_shapes=scratch,
    )
    return pl.pallas_call(
        _final_norm_kernel,
        grid_spec=grid_spec,
        out_shape=out_shape[0],
        compiler_params=_cparams("arbitrary"),
        name="moe_combine_final_norm",
    )(*operands, gfin)


def _moe(a, w_out_all, mixer_layer, h, layer, g_ffn, wr_g, br_g, wr_e, br_e, w_gu_all, w_down_all):
    t, d = h.shape
    bm = MOE_BLOCK
    tm = min(ROW_TILE, t)
    wr = jnp.concatenate([wr_g, wr_e, jnp.zeros((d, LANES - N_GROUPS - N_EXPERTS), F32)], axis=1)
    wrh = wr.astype(BF16)
    wrl = (wr - wrh.astype(F32)).astype(BF16)
    br = jnp.concatenate([br_g, br_e, jnp.zeros((LANES - N_GROUPS - N_EXPERTS,), F32)])[None, :]
    h1, hn, route, meta, cnt = _outproj_router(a, w_out_all, mixer_layer, h, g_ffn[None, :], wrh, wrl, br, tm=tm)

    counts = cnt[0, ROUTE_LANE0:ROUTE_LANE0 + N_EXPERTS].astype(I32)
    nb = (counts + bm - 1) // bm
    nb_end = jnp.cumsum(nb)
    pstart = ((nb_end - nb) * bm).astype(I32)
    n_blocks = (t * TOP_K + N_EXPERTS * (bm - 1)) // bm
    j = jnp.arange(n_blocks, dtype=I32)
    block_e = jnp.minimum(jnp.sum((nb_end[None, :] <= j[:, None]).astype(I32), axis=1), N_EXPERTS - 1)
    n_used = nb_end[-1:].astype(I32)

    eid = meta[:, 0:TOP_K, :]
    base = jnp.sum(jnp.where(eid[..., None] == jnp.arange(N_EXPERTS, dtype=I32), pstart, 0), axis=-1)
    rows = (meta[:, TOP_K:2 * TOP_K, :] + base) * (d // LANES)
    rows = rows.transpose(0, 2, 1).reshape(rows.shape[0], 1, TOP_K * rows.shape[2])

    xs = _dispatch(counts, pstart, n_used, rows, hn, n_blocks * bm, bm=bm)
    ys = _experts(block_e, n_used, xs, w_gu_all, w_down_all, layer, bm=bm)
    return _Pending(rows, h1, route, ys)


def _fox_mixer(src, g, w_in_all, layer, b_f, *, batch, seq):
    d = w_in_all.shape[1]
    nh = FOX_HEADS
    wf = w_in_all[layer, :, 4 * d:4 * d + nh]
    wf = jnp.concatenate([wf] * 6 + [jnp.zeros((d, LANES - 6 * nh), F32)], axis=1).astype(BF16)
    bfr = jnp.concatenate([b_f] * 6 + [jnp.zeros((LANES - 6 * nh,), F32)])[None, :]
    h, q, kb, v, gate, a = _fox_inproj(src, g[None, :], jnp.swapaxes(w_in_all, 1, 2), layer, wf, bfr, seq=seq)
    return h, _fox_attn(q, a, kb, v, gate, batch=batch, seq=seq)


def _ret_mixer(src, g, w_in_all, layer, gn_gain, *, batch, seq):
    d = w_in_all.shape[1]
    nh = RET_HEADS
    dk = d // nh
    c = RET_CHUNK
    inv = 1.0 / (ROPE_BASE ** jnp.linspace(0.0, 1.0, dk // 2, dtype=F32))
    ang = jnp.arange(seq).astype(F32)[:, None] * inv[None, :]
    cos, sin = jnp.cos(ang), jnp.sin(ang)
    log_g = jnp.log(1.0 - 2.0 ** (-5.0 - jnp.arange(nh, dtype=F32)))
    idx = jnp.arange(c, dtype=F32)
    rel = idx[:, None] - idx[None, :]
    intra = jnp.where(rel >= 0, jnp.exp(log_g[:, None, None] * jnp.maximum(rel, 0.0)), 0.0)
    qd = jnp.exp(log_g[:, None] * (idx + 1.0))[:, :, None]
    kd = jnp.exp(log_g[:, None] * (c - 1.0 - idx))[:, :, None]
    cd = jnp.exp(log_g * c)[:, None, None]
    h, q, k, v, gate = _ret_inproj(src, g[None, :], w_in_all, layer, cos, sin, seq=seq)
    return h, _retention(q, k, v, gate, gn_gain[None, :], intra, qd, kd, cd, batch=batch, seq=seq)


def kernel(x, fox_w_in, fox_b_f, fox_w_out, ret_w_in, ret_gn_gain, ret_w_out, norm_mix, norm_ffn,
           router_group_w, router_group_b, router_expert_w, router_expert_b, expert_w_gu, expert_w_down,
           norm_final):
    batch, seq, d = x.shape
    depth = norm_mix.shape[0]
    src = x.reshape(batch * seq, d)
    for i in range(depth):
        j = i // 2
        if i % 2 == 0:
            h, a = _fox_mixer(src, norm_mix[i], fox_w_in, j, fox_b_f[j], batch=batch, seq=seq)
            w_out_all = fox_w_out
        else:
            h, a = _ret_mixer(src, norm_mix[i], ret_w_in, j, ret_gn_gain[j], batch=batch, seq=seq)
            w_out_all = ret_w_out
        src = _moe(a, w_out_all, j, h, i, norm_ffn[i], router_group_w[i], router_group_b[i], router_expert_w[i],
                   router_expert_b[i], expert_w_gu, expert_w_down)
    return _final_norm(src, norm_final[None, :]).reshape(batch, seq, d)
```
